```python
import math
import jax, jax.numpy as jnp
from jax import lax
import numpy as np

D_MODEL = 1024
BATCH = 8
SEQ = 8192
DEPTH = 2

N_MIXERS = 2
N_ATTN_LAYERS = (DEPTH + N_MIXERS - 1) // N_MIXERS
N_RG_LAYERS = DEPTH // N_MIXERS

N_HEADS = 16
HEAD_DIM = D_MODEL // N_HEADS
Q_BLOCK = 128

D_RNN = 1344
RG_BLOCKS = 16
RG_BLOCK_W = D_RNN // RG_BLOCKS
RG_CONV_W = 4
RG_C = 8.0

D_FF = 2816
FFN_CONV_W = 3

NORM_EPS = 1e-6

kernel_name = "hybrid_stickbreak_rglru_convffn"


def rms_norm(x, g):
    xf = x.astype(jnp.float32)
    inv = lax.rsqrt(jnp.mean(xf * xf, axis=-1, keepdims=True) + NORM_EPS)
    return (xf * inv * g.astype(jnp.float32)).astype(x.dtype)


def causal_depthwise_conv(x, w, b):
    k = w.shape[0]
    out = lax.conv_general_dilated(
        x, w[:, None, :].astype(x.dtype), window_strides=(1,),
        padding=[(k - 1, 0)], dimension_numbers=("NWC", "WIO", "NWC"),
        feature_group_count=x.shape[-1])
    return out + b.astype(x.dtype)


def stick_breaking_attention(q, k, v):
    seq = q.shape[2]
    scale = HEAD_DIM ** -0.5
    outs = []
    for qb in range(seq // Q_BLOCK):
        q0 = qb * Q_BLOCK
        kend = q0 + Q_BLOCK
        z = jnp.einsum("bhtd,bhsd->bhts", q[:, :, q0:kend], k[:, :, :kend]).astype(jnp.float32) * scale
        t_pos = q0 + jnp.arange(Q_BLOCK)[:, None]
        s_pos = jnp.arange(kend)[None, :]
        causal = s_pos < t_pos
        log_beta = jax.nn.log_sigmoid(z)
        log_1m_beta = jnp.where(causal, jax.nn.log_sigmoid(-z), 0.0)
        log_stick = lax.cumsum(log_1m_beta, axis=3, reverse=True) - log_1m_beta
        w = jnp.where(causal, jnp.exp(log_beta + log_stick), 0.0)
        outs.append(jnp.einsum("bhts,bhsd->bhtd", w.astype(v.dtype), v[:, :, :kend]))
    return jnp.concatenate(outs, axis=2)


def attention_mixer(h, w_qkv, w_o):
    b, s, _ = h.shape
    qkv = (h @ w_qkv).reshape(b, s, 3, N_HEADS, HEAD_DIM)
    qkv = jnp.transpose(qkv, (2, 0, 3, 1, 4))
    o = stick_breaking_attention(qkv[0], qkv[1], qkv[2])
    o = jnp.transpose(o, (0, 2, 1, 3)).reshape(b, s, D_MODEL)
    return o @ w_o


def block_diag_linear(x, w, bias):
    b, s, _ = x.shape
    xb = x.reshape(b, s, RG_BLOCKS, RG_BLOCK_W)
    y = jnp.einsum("bsnc,ncd->bsnd", xb, w).reshape(b, s, D_RNN)
    return y + bias


def rg_lru(x, w_a, b_a, w_x, b_x, lam):
    r = jax.nn.sigmoid(block_diag_linear(x, w_a, b_a).astype(jnp.float32))
    i = jax.nn.sigmoid(block_diag_linear(x, w_x, b_x).astype(jnp.float32))
    log_a = -RG_C * r * jax.nn.softplus(-lam.astype(jnp.float32))
    a = jnp.exp(log_a)
    mult = jnp.sqrt(-jnp.expm1(2.0 * log_a))
    u = mult * i * x.astype(jnp.float32)

    def combine(left, right):
        a_l, b_l = left
        a_r, b_r = right
        return a_l * a_r, a_r * b_l + b_r

    _, h = lax.associative_scan(combine, (a, u), axis=1)
    return h.astype(x.dtype)


def recurrent_mixer(h, w_in, conv_w, conv_b, w_a, b_a, w_x, b_x, lam, w_out):
    proj = h @ w_in
    gate_branch, rec_branch = jnp.split(proj, 2, axis=-1)
    gate = jax.nn.gelu(gate_branch, approximate=True)
    rec = causal_depthwise_conv(rec_branch, conv_w, conv_b)
    rec = rg_lru(rec, w_a, b_a, w_x, b_x, lam)
    return (gate * rec) @ w_out


def conv_ffn(h, w_up, conv_w, conv_b, w_down):
    u = causal_depthwise_conv(h @ w_up, conv_w, conv_b)
    g, val = jnp.split(u, 2, axis=-1)
    return (jax.nn.gelu(g, approximate=True) * val) @ w_down


def _fwd_setup_inputs(seed: int = 0) -> dict:
    key = jax.random.key(seed)
    ks = iter(jax.random.split(key, 32))

    def nrm(shape, fan_in):
        return jax.random.normal(next(ks), shape, jnp.float32) * (fan_in ** -0.5)

    def gain(shape):
        return 1.0 + 0.02 * jax.random.normal(next(ks), shape, jnp.float32)

    def small(shape):
        return 0.01 * jax.random.normal(next(ks), shape, jnp.float32)

    x = jax.random.normal(next(ks), (BATCH, SEQ, D_MODEL), jnp.float32)

    attn_w_qkv = nrm((N_ATTN_LAYERS, D_MODEL, 3 * D_MODEL), D_MODEL)
    attn_w_o = nrm((N_ATTN_LAYERS, D_MODEL, D_MODEL), D_MODEL)

    rg_w_in = nrm((N_RG_LAYERS, D_MODEL, 2 * D_RNN), D_MODEL)
    rg_conv_w = nrm((N_RG_LAYERS, RG_CONV_W, D_RNN), RG_CONV_W)
    rg_conv_b = small((N_RG_LAYERS, D_RNN))
    rg_w_a = nrm((N_RG_LAYERS, RG_BLOCKS, RG_BLOCK_W, RG_BLOCK_W), RG_BLOCK_W)
    rg_b_a = small((N_RG_LAYERS, D_RNN))
    rg_w_x = nrm((N_RG_LAYERS, RG_BLOCKS, RG_BLOCK_W, RG_BLOCK_W), RG_BLOCK_W)
    rg_b_x = small((N_RG_LAYERS, D_RNN))
    u = jax.random.uniform(next(ks), (N_RG_LAYERS, D_RNN), jnp.float32, 0.9, 0.999)
    a0 = u ** (1.0 / RG_C)
    rg_lambda = jnp.log(a0) - jnp.log1p(-a0)
    rg_w_out = nrm((N_RG_LAYERS, D_RNN, D_MODEL), D_RNN)

    ffn_w_up = nrm((DEPTH, D_MODEL, 2 * D_FF), D_MODEL)
    ffn_conv_w = nrm((DEPTH, FFN_CONV_W, 2 * D_FF), FFN_CONV_W)
    ffn_conv_b = small((DEPTH, 2 * D_FF))
    ffn_w_down = nrm((DEPTH, D_FF, D_MODEL), D_FF)

    mix_pre_g = gain((DEPTH, D_MODEL))
    mix_post_g = gain((DEPTH, D_MODEL))
    ffn_pre_g = gain((DEPTH, D_MODEL))
    ffn_post_g = gain((DEPTH, D_MODEL))

    return {
        "x": x,
        "attn_w_qkv": attn_w_qkv, "attn_w_o": attn_w_o,
        "rg_w_in": rg_w_in, "rg_conv_w": rg_conv_w, "rg_conv_b": rg_conv_b,
        "rg_w_a": rg_w_a, "rg_b_a": rg_b_a, "rg_w_x": rg_w_x, "rg_b_x": rg_b_x,
        "rg_lambda": rg_lambda, "rg_w_out": rg_w_out,
        "ffn_w_up": ffn_w_up, "ffn_conv_w": ffn_conv_w, "ffn_conv_b": ffn_conv_b,
        "ffn_w_down": ffn_w_down,
        "mix_pre_g": mix_pre_g, "mix_post_g": mix_post_g,
        "ffn_pre_g": ffn_pre_g, "ffn_post_g": ffn_post_g,
    }


def _fwd_reference(x, attn_w_qkv, attn_w_o, rg_w_in, rg_conv_w, rg_conv_b, rg_w_a, rg_b_a,
              rg_w_x, rg_b_x, rg_lambda, rg_w_out, ffn_w_up, ffn_conv_w, ffn_conv_b,
              ffn_w_down, mix_pre_g, mix_post_g, ffn_pre_g, ffn_post_g):
    for i in range(DEPTH):
        j = i // N_MIXERS
        h = rms_norm(x, mix_pre_g[i])
        if i % N_MIXERS == 0:
            h = attention_mixer(h, attn_w_qkv[j], attn_w_o[j])
        else:
            h = recurrent_mixer(h, rg_w_in[j], rg_conv_w[j], rg_conv_b[j], rg_w_a[j], rg_b_a[j],
                                rg_w_x[j], rg_b_x[j], rg_lambda[j], rg_w_out[j])
        x = x + rms_norm(h, mix_post_g[i])
        h = conv_ffn(rms_norm(x, ffn_pre_g[i]), ffn_w_up[i], ffn_conv_w[i], ffn_conv_b[i], ffn_w_down[i])
        x = x + rms_norm(h, ffn_post_g[i])
    return x


import jax as _jax
import jax.numpy as _jnp

TWIN_FORMAT = 'train_step'
FWD_PARAMS = ['x', 'attn_w_qkv', 'attn_w_o', 'rg_w_in', 'rg_conv_w', 'rg_conv_b', 'rg_w_a', 'rg_b_a', 'rg_w_x', 'rg_b_x', 'rg_lambda', 'rg_w_out', 'ffn_w_up', 'ffn_conv_w', 'ffn_conv_b', 'ffn_w_down', 'mix_pre_g', 'mix_post_g', 'ffn_pre_g', 'ffn_post_g']
TWIN_WEIGHTS = ['attn_w_qkv', 'attn_w_o', 'rg_w_in', 'rg_conv_w', 'rg_conv_b', 'rg_w_a', 'rg_b_a', 'rg_w_x', 'rg_b_x', 'rg_lambda', 'rg_w_out', 'ffn_w_up', 'ffn_conv_w', 'ffn_conv_b', 'ffn_w_down', 'mix_pre_g', 'mix_post_g', 'ffn_pre_g', 'ffn_post_g']
TWIN_DIFF_INPUT = 'x'
TWIN_INPUTS = ['x', 'attn_w_qkv', 'attn_w_o', 'rg_w_in', 'rg_conv_w', 'rg_conv_b', 'rg_w_a', 'rg_b_a', 'rg_w_x', 'rg_b_x', 'rg_lambda', 'rg_w_out', 'ffn_w_up', 'ffn_conv_w', 'ffn_conv_b', 'ffn_w_down', 'mix_pre_g', 'mix_post_g', 'ffn_pre_g', 'ffn_post_g', 'loss_target', 'm_attn_w_qkv', 'm_attn_w_o', 'm_rg_w_in', 'm_rg_conv_w', 'm_rg_conv_b', 'm_rg_w_a', 'm_rg_b_a', 'm_rg_w_x', 'm_rg_b_x', 'm_rg_lambda', 'm_rg_w_out', 'm_ffn_w_up', 'm_ffn_conv_w', 'm_ffn_conv_b', 'm_ffn_w_down', 'm_mix_pre_g', 'm_mix_post_g', 'm_ffn_pre_g', 'm_ffn_post_g', 'v_attn_w_qkv', 'v_attn_w_o', 'v_rg_w_in', 'v_rg_conv_w', 'v_rg_conv_b', 'v_rg_w_a', 'v_rg_b_a', 'v_rg_w_x', 'v_rg_b_x', 'v_rg_lambda', 'v_rg_w_out', 'v_ffn_w_up', 'v_ffn_conv_w', 'v_ffn_conv_b', 'v_ffn_w_down', 'v_mix_pre_g', 'v_mix_post_g', 'v_ffn_pre_g', 'v_ffn_post_g']
TWIN_OUTPUTS = ['loss', 'grad_x', 'grad_attn_w_qkv', 'grad_attn_w_o', 'grad_rg_w_in', 'grad_rg_conv_w', 'grad_rg_conv_b', 'grad_rg_w_a', 'grad_rg_b_a', 'grad_rg_w_x', 'grad_rg_b_x', 'grad_rg_lambda', 'grad_rg_w_out', 'grad_ffn_w_up', 'grad_ffn_conv_w', 'grad_ffn_conv_b', 'grad_ffn_w_down', 'grad_mix_pre_g', 'grad_mix_post_g', 'grad_ffn_pre_g', 'grad_ffn_post_g', 'delta_attn_w_qkv', 'delta_attn_w_o', 'delta_rg_w_in', 'delta_rg_conv_w', 'delta_rg_conv_b', 'delta_rg_w_a', 'delta_rg_b_a', 'delta_rg_w_x', 'delta_rg_b_x', 'delta_rg_lambda', 'delta_rg_w_out', 'delta_ffn_w_up', 'delta_ffn_conv_w', 'delta_ffn_conv_b', 'delta_ffn_w_down', 'delta_mix_pre_g', 'delta_mix_post_g', 'delta_ffn_pre_g', 'delta_ffn_post_g', 'new_m_attn_w_qkv', 'new_m_attn_w_o', 'new_m_rg_w_in', 'new_m_rg_conv_w', 'new_m_rg_conv_b', 'new_m_rg_w_a', 'new_m_rg_b_a', 'new_m_rg_w_x', 'new_m_rg_b_x', 'new_m_rg_lambda', 'new_m_rg_w_out', 'new_m_ffn_w_up', 'new_m_ffn_conv_w', 'new_m_ffn_conv_b', 'new_m_ffn_w_down', 'new_m_mix_pre_g', 'new_m_mix_post_g', 'new_m_ffn_pre_g', 'new_m_ffn_post_g', 'new_v_attn_w_qkv', 'new_v_attn_w_o', 'new_v_rg_w_in', 'new_v_rg_conv_w', 'new_v_rg_conv_b', 'new_v_rg_w_a', 'new_v_rg_b_a', 'new_v_rg_w_x', 'new_v_rg_b_x', 'new_v_rg_lambda', 'new_v_rg_w_out', 'new_v_ffn_w_up', 'new_v_ffn_conv_w', 'new_v_ffn_conv_b', 'new_v_ffn_w_down', 'new_v_mix_pre_g', 'new_v_mix_post_g', 'new_v_ffn_pre_g', 'new_v_ffn_post_g']
TWIN_LEAF_KINDS = {'loss': 'loss', 'grad_x': 'grad_x', 'grad_attn_w_qkv': 'grad_w', 'grad_attn_w_o': 'grad_w', 'grad_rg_w_in': 'grad_w', 'grad_rg_conv_w': 'grad_w', 'grad_rg_conv_b': 'grad_w', 'grad_rg_w_a': 'grad_w', 'grad_rg_b_a': 'grad_w', 'grad_rg_w_x': 'grad_w', 'grad_rg_b_x': 'grad_w', 'grad_rg_lambda': 'grad_w', 'grad_rg_w_out': 'grad_w', 'grad_ffn_w_up': 'grad_w', 'grad_ffn_conv_w': 'grad_w', 'grad_ffn_conv_b': 'grad_w', 'grad_ffn_w_down': 'grad_w', 'grad_mix_pre_g': 'grad_w', 'grad_mix_post_g': 'grad_w', 'grad_ffn_pre_g': 'grad_w', 'grad_ffn_post_g': 'grad_w', 'delta_attn_w_qkv': 'delta_w', 'delta_attn_w_o': 'delta_w', 'delta_rg_w_in': 'delta_w', 'delta_rg_conv_w': 'delta_w', 'delta_rg_conv_b': 'delta_w', 'delta_rg_w_a': 'delta_w', 'delta_rg_b_a': 'delta_w', 'delta_rg_w_x': 'delta_w', 'delta_rg_b_x': 'delta_w', 'delta_rg_lambda': 'delta_w', 'delta_rg_w_out': 'delta_w', 'delta_ffn_w_up': 'delta_w', 'delta_ffn_conv_w': 'delta_w', 'delta_ffn_conv_b': 'delta_w', 'delta_ffn_w_down': 'delta_w', 'delta_mix_pre_g': 'delta_w', 'delta_mix_post_g': 'delta_w', 'delta_ffn_pre_g': 'delta_w', 'delta_ffn_post_g': 'delta_w', 'new_m_attn_w_qkv': 'new_m', 'new_m_attn_w_o': 'new_m', 'new_m_rg_w_in': 'new_m', 'new_m_rg_conv_w': 'new_m', 'new_m_rg_conv_b': 'new_m', 'new_m_rg_w_a': 'new_m', 'new_m_rg_b_a': 'new_m', 'new_m_rg_w_x': 'new_m', 'new_m_rg_b_x': 'new_m', 'new_m_rg_lambda': 'new_m', 'new_m_rg_w_out': 'new_m', 'new_m_ffn_w_up': 'new_m', 'new_m_ffn_conv_w': 'new_m', 'new_m_ffn_conv_b': 'new_m', 'new_m_ffn_w_down': 'new_m', 'new_m_mix_pre_g': 'new_m', 'new_m_mix_post_g': 'new_m', 'new_m_ffn_pre_g': 'new_m', 'new_m_ffn_post_g': 'new_m', 'new_v_attn_w_qkv': 'new_v', 'new_v_attn_w_o': 'new_v', 'new_v_rg_w_in': 'new_v', 'new_v_rg_conv_w': 'new_v', 'new_v_rg_conv_b': 'new_v', 'new_v_rg_w_a': 'new_v', 'new_v_rg_b_a': 'new_v', 'new_v_rg_w_x': 'new_v', 'new_v_rg_b_x': 'new_v', 'new_v_rg_lambda': 'new_v', 'new_v_rg_w_out': 'new_v', 'new_v_ffn_w_up': 'new_v', 'new_v_ffn_conv_w': 'new_v', 'new_v_ffn_conv_b': 'new_v', 'new_v_ffn_w_down': 'new_v', 'new_v_mix_pre_g': 'new_v', 'new_v_mix_post_g': 'new_v', 'new_v_ffn_pre_g': 'new_v', 'new_v_ffn_post_g': 'new_v'}


def _forward(args):
    return _fwd_reference(*[args[k] for k in FWD_PARAMS])


def _output_shape():
    def fwd():
        inp = _fwd_setup_inputs(0)
        return _fwd_reference(*[inp[k] for k in FWD_PARAMS])
    out = _jax.eval_shape(fwd)
    return out.shape, out.dtype

N_MICROBATCH = 1
ADAM_LR = 0.001
ADAM_B1 = 0.9
ADAM_B2 = 0.999
ADAM_EPS = 1e-08
ADAM_WD = 0.01
ADAM_STEP = 10
PER_EXAMPLE_BATCH_AXIS = {'x': 0, 'loss_target': 0}
SHARED_INPUTS = []
_WEIGHT_DTYPES = {'attn_w_qkv': _jnp.float32, 'attn_w_o': _jnp.float32, 'rg_w_in': _jnp.float32, 'rg_conv_w': _jnp.float32, 'rg_conv_b': _jnp.float32, 'rg_w_a': _jnp.float32, 'rg_b_a': _jnp.float32, 'rg_w_x': _jnp.float32, 'rg_b_x': _jnp.float32, 'rg_lambda': _jnp.float32, 'rg_w_out': _jnp.float32, 'ffn_w_up': _jnp.float32, 'ffn_conv_w': _jnp.float32, 'ffn_conv_b': _jnp.float32, 'ffn_w_down': _jnp.float32, 'mix_pre_g': _jnp.float32, 'mix_post_g': _jnp.float32, 'ffn_pre_g': _jnp.float32, 'ffn_post_g': _jnp.float32}
MOMENT_SCALE = {'attn_w_qkv': 9.450654e-01, 'attn_w_o': 1.532180e+00, 'rg_w_in': 8.074688e-01, 'rg_conv_w': 9.990800e-01, 'rg_conv_b': 2.427788e+01, 'rg_w_a': 6.062332e-01, 'rg_b_a': 3.307855e-01, 'rg_w_x': 1.138904e+00, 'rg_b_x': 2.780751e-01, 'rg_lambda': 5.307385e-01, 'rg_w_out': 1.441289e+00, 'ffn_w_up': 5.660291e-01, 'ffn_conv_w': 6.279689e-01, 'ffn_conv_b': 3.452888e+00, 'ffn_w_down': 1.133722e+00, 'mix_pre_g': 1.496115e+00, 'mix_post_g': 6.409695e+01, 'ffn_pre_g': 1.468241e+00, 'ffn_post_g': 6.394861e+01}


def _to_microbatches(a, axis):
    t = _jnp.moveaxis(a, axis, 0)
    t = t.reshape((N_MICROBATCH, t.shape[0] // N_MICROBATCH) + t.shape[1:])
    return _jnp.moveaxis(t, 1, axis + 1)


def setup_inputs(seed: int = 0) -> dict:
    inp = _fwd_setup_inputs(seed)
    key = _jax.random.fold_in(_jax.random.key(seed), 7919)
    shape, _ = _output_shape()
    out = dict(inp)
    out["loss_target"] = _jax.random.normal(_jax.random.fold_in(key, 0), shape, _jnp.float32)
    for i, name in enumerate(TWIN_WEIGHTS):
        w = inp[name].astype(_jnp.float32)
        if MOMENT_SCALE is None:
            s = _jnp.sqrt(_jnp.mean(_jnp.square(w)) + 1e-30)
        else:
            s = MOMENT_SCALE[name]
        km, kv = _jax.random.split(_jax.random.fold_in(key, i + 1))
        out[name] = w
        out["m_" + name] = s * _jax.random.normal(km, w.shape, _jnp.float32)
        out["v_" + name] = (s * s) * _jax.random.uniform(kv, w.shape, _jnp.float32, 0.5, 1.5)
    if N_MICROBATCH > 1:
        for name, axis in PER_EXAMPLE_BATCH_AXIS.items():
            out[name] = _to_microbatches(out[name], axis)
    return {'x': out['x'], 'attn_w_qkv': out['attn_w_qkv'], 'attn_w_o': out['attn_w_o'], 'rg_w_in': out['rg_w_in'], 'rg_conv_w': out['rg_conv_w'], 'rg_conv_b': out['rg_conv_b'], 'rg_w_a': out['rg_w_a'], 'rg_b_a': out['rg_b_a'], 'rg_w_x': out['rg_w_x'], 'rg_b_x': out['rg_b_x'], 'rg_lambda': out['rg_lambda'], 'rg_w_out': out['rg_w_out'], 'ffn_w_up': out['ffn_w_up'], 'ffn_conv_w': out['ffn_conv_w'], 'ffn_conv_b': out['ffn_conv_b'], 'ffn_w_down': out['ffn_w_down'], 'mix_pre_g': out['mix_pre_g'], 'mix_post_g': out['mix_post_g'], 'ffn_pre_g': out['ffn_pre_g'], 'ffn_post_g': out['ffn_post_g'], 'loss_target': out['loss_target'], 'm_attn_w_qkv': out['m_attn_w_qkv'], 'm_attn_w_o': out['m_attn_w_o'], 'm_rg_w_in': out['m_rg_w_in'], 'm_rg_conv_w': out['m_rg_conv_w'], 'm_rg_conv_b': out['m_rg_conv_b'], 'm_rg_w_a': out['m_rg_w_a'], 'm_rg_b_a': out['m_rg_b_a'], 'm_rg_w_x': out['m_rg_w_x'], 'm_rg_b_x': out['m_rg_b_x'], 'm_rg_lambda': out['m_rg_lambda'], 'm_rg_w_out': out['m_rg_w_out'], 'm_ffn_w_up': out['m_ffn_w_up'], 'm_ffn_conv_w': out['m_ffn_conv_w'], 'm_ffn_conv_b': out['m_ffn_conv_b'], 'm_ffn_w_down': out['m_ffn_w_down'], 'm_mix_pre_g': out['m_mix_pre_g'], 'm_mix_post_g': out['m_mix_post_g'], 'm_ffn_pre_g': out['m_ffn_pre_g'], 'm_ffn_post_g': out['m_ffn_post_g'], 'v_attn_w_qkv': out['v_attn_w_qkv'], 'v_attn_w_o': out['v_attn_w_o'], 'v_rg_w_in': out['v_rg_w_in'], 'v_rg_conv_w': out['v_rg_conv_w'], 'v_rg_conv_b': out['v_rg_conv_b'], 'v_rg_w_a': out['v_rg_w_a'], 'v_rg_b_a': out['v_rg_b_a'], 'v_rg_w_x': out['v_rg_w_x'], 'v_rg_b_x': out['v_rg_b_x'], 'v_rg_lambda': out['v_rg_lambda'], 'v_rg_w_out': out['v_rg_w_out'], 'v_ffn_w_up': out['v_ffn_w_up'], 'v_ffn_conv_w': out['v_ffn_conv_w'], 'v_ffn_conv_b': out['v_ffn_conv_b'], 'v_ffn_w_down': out['v_ffn_w_down'], 'v_mix_pre_g': out['v_mix_pre_g'], 'v_mix_post_g': out['v_mix_post_g'], 'v_ffn_pre_g': out['v_ffn_pre_g'], 'v_ffn_post_g': out['v_ffn_post_g']}


def _loss(weights, diff, rest, loss_target):
    with _jax.named_scope("forward"):
        args = {**rest, TWIN_DIFF_INPUT: diff, **{k: w.astype(_WEIGHT_DTYPES[k]) for k, w in weights.items()}}
        y = _forward(args)
    with _jax.named_scope("loss_head"):
        err = _jnp.square(y.astype(_jnp.float32) - loss_target)
        return 0.5 * _jnp.sum(_jnp.mean(err, axis=-1)) if err.ndim else 0.5 * err


def _adamw(w, g, m, v):
    m = ADAM_B1 * m + (1.0 - ADAM_B1) * g
    v = ADAM_B2 * v + (1.0 - ADAM_B2) * _jnp.square(g)
    m_hat = m / (1.0 - ADAM_B1 ** ADAM_STEP)
    v_hat = v / (1.0 - ADAM_B2 ** ADAM_STEP)
    delta = -ADAM_LR * (m_hat / (_jnp.sqrt(v_hat) + ADAM_EPS) + ADAM_WD * w)
    return delta, m, v


def reference(x, attn_w_qkv, attn_w_o, rg_w_in, rg_conv_w, rg_conv_b, rg_w_a, rg_b_a, rg_w_x, rg_b_x, rg_lambda, rg_w_out, ffn_w_up, ffn_conv_w, ffn_conv_b, ffn_w_down, mix_pre_g, mix_post_g, ffn_pre_g, ffn_post_g, loss_target, m_attn_w_qkv, m_attn_w_o, m_rg_w_in, m_rg_conv_w, m_rg_conv_b, m_rg_w_a, m_rg_b_a, m_rg_w_x, m_rg_b_x, m_rg_lambda, m_rg_w_out, m_ffn_w_up, m_ffn_conv_w, m_ffn_conv_b, m_ffn_w_down, m_mix_pre_g, m_mix_post_g, m_ffn_pre_g, m_ffn_post_g, v_attn_w_qkv, v_attn_w_o, v_rg_w_in, v_rg_conv_w, v_rg_conv_b, v_rg_w_a, v_rg_b_a, v_rg_w_x, v_rg_b_x, v_rg_lambda, v_rg_w_out, v_ffn_w_up, v_ffn_conv_w, v_ffn_conv_b, v_ffn_w_down, v_mix_pre_g, v_mix_post_g, v_ffn_pre_g, v_ffn_post_g):
    given = dict(x=x, attn_w_qkv=attn_w_qkv, attn_w_o=attn_w_o, rg_w_in=rg_w_in, rg_conv_w=rg_conv_w, rg_conv_b=rg_conv_b, rg_w_a=rg_w_a, rg_b_a=rg_b_a, rg_w_x=rg_w_x, rg_b_x=rg_b_x, rg_lambda=rg_lambda, rg_w_out=rg_w_out, ffn_w_up=ffn_w_up, ffn_conv_w=ffn_conv_w, ffn_conv_b=ffn_conv_b, ffn_w_down=ffn_w_down, mix_pre_g=mix_pre_g, mix_post_g=mix_post_g, ffn_pre_g=ffn_pre_g, ffn_post_g=ffn_post_g, loss_target=loss_target, m_attn_w_qkv=m_attn_w_qkv, m_attn_w_o=m_attn_w_o, m_rg_w_in=m_rg_w_in, m_rg_conv_w=m_rg_conv_w, m_rg_conv_b=m_rg_conv_b, m_rg_w_a=m_rg_w_a, m_rg_b_a=m_rg_b_a, m_rg_w_x=m_rg_w_x, m_rg_b_x=m_rg_b_x, m_rg_lambda=m_rg_lambda, m_rg_w_out=m_rg_w_out, m_ffn_w_up=m_ffn_w_up, m_ffn_conv_w=m_ffn_conv_w, m_ffn_conv_b=m_ffn_conv_b, m_ffn_w_down=m_ffn_w_down, m_mix_pre_g=m_mix_pre_g, m_mix_post_g=m_mix_post_g, m_ffn_pre_g=m_ffn_pre_g, m_ffn_post_g=m_ffn_post_g, v_attn_w_qkv=v_attn_w_qkv, v_attn_w_o=v_attn_w_o, v_rg_w_in=v_rg_w_in, v_rg_conv_w=v_rg_conv_w, v_rg_conv_b=v_rg_conv_b, v_rg_w_a=v_rg_w_a, v_rg_b_a=v_rg_b_a, v_rg_w_x=v_rg_w_x, v_rg_b_x=v_rg_b_x, v_rg_lambda=v_rg_lambda, v_rg_w_out=v_rg_w_out, v_ffn_w_up=v_ffn_w_up, v_ffn_conv_w=v_ffn_conv_w, v_ffn_conv_b=v_ffn_conv_b, v_ffn_w_down=v_ffn_w_down, v_mix_pre_g=v_mix_pre_g, v_mix_post_g=v_mix_post_g, v_ffn_pre_g=v_ffn_pre_g, v_ffn_post_g=v_ffn_post_g)
    weights = {n: given[n] for n in TWIN_WEIGHTS}
    shared = {n: given[n] for n in SHARED_INPUTS}
    per_example = {n: given[n] for n in ['x']}
    grad_fn = _jax.value_and_grad(_loss, argnums=(0, 1))

    def one_microbatch(ex, loss_target):
        ex = dict(ex)
        diff = ex.pop(TWIN_DIFF_INPUT)
        return grad_fn(weights, diff, {**shared, **ex}, loss_target)

    if N_MICROBATCH == 1:
        loss, (grad_w, grad_x) = one_microbatch(per_example, given["loss_target"])
    else:
        def body(carry, xs):
            loss_sum, grad_sum = carry
            l_k, (gw_k, gx_k) = one_microbatch(xs[0], xs[1])
            with _jax.named_scope("update"):
                return (loss_sum + l_k, _jax.tree.map(_jnp.add, grad_sum, gw_k)), gx_k

        init = (_jnp.zeros((), _jnp.float32), _jax.tree.map(_jnp.zeros_like, weights))
        (loss, grad_w), grad_x = _jax.lax.scan(body, init, (per_example, given["loss_target"]))
    with _jax.named_scope("update"):
        delta_w, new_m, new_v = {}, {}, {}
        for n in TWIN_WEIGHTS:
            delta_w[n], new_m[n], new_v[n] = _adamw(weights[n], grad_w[n], given["m_" + n], given["v_" + n])
    return (loss, grad_x, *[grad_w[n] for n in TWIN_WEIGHTS], *[delta_w[n] for n in TWIN_WEIGHTS],
            *[new_m[n] for n in TWIN_WEIGHTS], *[new_v[n] for n in TWIN_WEIGHTS])
```

```python
import functools
import math

import jax
import jax.numpy as jnp
from jax import lax
from jax.experimental import pallas as pl
from jax.experimental.pallas import tpu as pltpu

F32 = jnp.float32
BF16 = jnp.bfloat16

D_MODEL = 1024
N_HEADS = 16
HEAD_DIM = 64
D_RNN = 1344
D_RNN_PAD = 1408
RG_BLOCKS = 16
RG_BLOCK_W = 84
RG_CONV_W = 4
RG_C = 8.0
D_FF = 2816
FFN_CONV_W = 3
FFN_COL = 256
NORM_EPS = 1e-6
N_DEV = 8
LANES = 128
HALO = 8

ADAM_LR = 0.001
ADAM_B1 = 0.9
ADAM_B2 = 0.999
ADAM_EPS = 1e-08
ADAM_WD = 0.01
ADAM_STEP = 10

ATT_BLK = 128
VMEM_LIMIT = 56 * 1024 * 1024

_GELU_C = math.sqrt(2.0 / math.pi)


def _params(sem=None):
    return pltpu.CompilerParams(dimension_semantics=sem, vmem_limit_bytes=VMEM_LIMIT)


def _tile(n, cap, mult=8):
    if n <= cap:
        return n
    best = None
    for t in range(mult, cap + 1, mult):
        if n % t == 0:
            best = t
    assert best is not None, (n, cap, mult)
    return best


def _gelu_and_grad(x):
    x2 = x * x
    th = jnp.tanh(_GELU_C * (x + 0.044715 * (x2 * x)))
    cdf = 0.5 * (1.0 + th)
    gel = x * cdf
    dgel = cdf + 0.5 * x * (1.0 - th * th) * (_GELU_C * (1.0 + 3.0 * 0.044715 * x2))
    return gel, dgel


def _gelu(x):
    th = jnp.tanh(_GELU_C * (x + 0.044715 * (x * x * x)))
    return x * (0.5 * (1.0 + th))


def _matmul(a, b, *, ta=False, tb=False, out_dtype=F32, name):
    if ta:
        kdim, m = a.shape
    else:
        m, kdim = a.shape
    if tb:
        n, kb_ = b.shape
    else:
        kb_, n = b.shape
    assert kdim == kb_, (a.shape, b.shape, ta, tb)
    wide = lambda d: 1408 if d % 1408 == 0 else 1024
    tm = _tile(m, wide(m), 128) if ta else _tile(m, 512)
    tn = _tile(n, wide(n), 128)
    tk = _tile(kdim, 512, 128) if ta else _tile(kdim, wide(kdim), 128)
    nk = kdim // tk
    dims = (((0 if ta else 1,), (1 if tb else 0,)), ((), ()))

    def body(a_ref, b_ref, o_ref, acc_ref):
        part = lax.dot_general(a_ref[...], b_ref[...], dims, preferred_element_type=F32)
        if nk == 1:
            o_ref[...] = part.astype(out_dtype)
        else:
            k = pl.program_id(2)

            @pl.when(k == 0)
            def _():
                acc_ref[...] = part

            @pl.when(k > 0)
            def _():
                acc_ref[...] += part

            @pl.when(k == nk - 1)
            def _():
                o_ref[...] = acc_ref[...].astype(out_dtype)

    a_spec = pl.BlockSpec((tk, tm), lambda i, j, k: (k, i)) if ta else pl.BlockSpec((tm, tk), lambda i, j, k: (i, k))
    b_spec = pl.BlockSpec((tn, tk), lambda i, j, k: (j, k)) if tb else pl.BlockSpec((tk, tn), lambda i, j, k: (k, j))
    return pl.pallas_call(
        body,
        grid=(m // tm, n // tn, nk),
        in_specs=[a_spec, b_spec],
        out_specs=pl.BlockSpec((tm, tn), lambda i, j, k: (i, j)),
        out_shape=jax.ShapeDtypeStruct((m, n), out_dtype),
        scratch_shapes=[pltpu.VMEM((tm, tn), F32)],
        compiler_params=_params(("parallel", "parallel", "arbitrary")),
        name=name,
    )(a, b)


def _rms(xv, g):
    inv = lax.rsqrt(jnp.mean(xv * xv, axis=-1, keepdims=True) + NORM_EPS)
    return xv * inv * g


def _rms_fwd(x, g, name):
    s, d = x.shape
    t = _tile(s, 512)

    def body(x_ref, g_ref, o_ref):
        o_ref[...] = _rms(x_ref[...], g_ref[...]).astype(BF16)

    return pl.pallas_call(
        body, grid=(s // t,),
        in_specs=[pl.BlockSpec((t, d), lambda i: (i, 0)), pl.BlockSpec((1, d), lambda i: (0, 0))],
        out_specs=pl.BlockSpec((t, d), lambda i: (i, 0)),
        out_shape=jax.ShapeDtypeStruct((s, d), BF16),
        compiler_params=_params(("parallel",)), name=name,
    )(x, g)


def _resid_norm(x, f, post_g, pre_g, name):
    s, d = x.shape
    t = _tile(s, 512)

    def body(x_ref, f_ref, pg_ref, ng_ref, xo_ref, hn_ref):
        xn = x_ref[...] + _rms(f_ref[...], pg_ref[...])
        xo_ref[...] = xn
        hn_ref[...] = _rms(xn, ng_ref[...]).astype(BF16)

    row = pl.BlockSpec((t, d), lambda i: (i, 0))
    vec = pl.BlockSpec((1, d), lambda i: (0, 0))
    return pl.pallas_call(
        body, grid=(s // t,), in_specs=[row, row, vec, vec], out_specs=[row, row],
        out_shape=[jax.ShapeDtypeStruct((s, d), F32), jax.ShapeDtypeStruct((s, d), BF16)],
        compiler_params=_params(("parallel",)), name=name,
    )(x, f, post_g, pre_g)


def _resid_loss(x, f, post_g, tgt, name):
    s, d = x.shape
    t = _tile(s, 512)

    def body(x_ref, f_ref, pg_ref, t_ref, dy_ref, loss_ref):
        @pl.when(pl.program_id(0) == 0)
        def _():
            loss_ref[...] = jnp.zeros_like(loss_ref)

        err = x_ref[...] + _rms(f_ref[...], pg_ref[...]) - t_ref[...]
        dy_ref[...] = err * (1.0 / d)
        sq = jnp.sum(jnp.sum(err * err, axis=1, keepdims=True), axis=0, keepdims=True)
        loss_ref[...] += sq

    row = pl.BlockSpec((t, d), lambda i: (i, 0))
    vec = pl.BlockSpec((1, d), lambda i: (0, 0))
    return pl.pallas_call(
        body, grid=(s // t,), in_specs=[row, row, vec, row],
        out_specs=[row, pl.BlockSpec((8, LANES), lambda i: (0, 0))],
        out_shape=[jax.ShapeDtypeStruct((s, d), F32), jax.ShapeDtypeStruct((8, LANES), F32)],
        compiler_params=_params(("arbitrary",)), name=name,
    )(x, f, post_g, tgt)


def _rms_bwd(xin, g, dy, add, out_dtype, name):
    s, d = xin.shape
    t = _tile(s, 512)
    has_add = add is not None

    def body(*refs):
        if has_add:
            x_ref, g_ref, dy_ref, add_ref, dx_ref, dg_ref = refs
        else:
            x_ref, g_ref, dy_ref, dx_ref, dg_ref = refs

        @pl.when(pl.program_id(0) == 0)
        def _():
            dg_ref[...] = jnp.zeros_like(dg_ref)

        xv = x_ref[...]
        dyv = dy_ref[...].astype(F32)
        inv = lax.rsqrt(jnp.mean(xv * xv, axis=-1, keepdims=True) + NORM_EPS)
        xh = xv * inv
        gy = dyv * g_ref[...]
        dx = inv * (gy - xh * jnp.mean(xh * gy, axis=-1, keepdims=True))
        if has_add:
            dx = dx + add_ref[...]
        dx_ref[...] = dx.astype(out_dtype)
        dg_ref[...] += jnp.sum(dyv * xh, axis=0, keepdims=True)

    row = pl.BlockSpec((t, d), lambda i: (i, 0))
    vec = pl.BlockSpec((1, d), lambda i: (0, 0))
    args = [xin, g, dy] + ([add] if has_add else [])
    specs = [row, vec, row] + ([row] if has_add else [])
    return pl.pallas_call(
        body, grid=(s // t,), in_specs=specs, out_specs=[row, vec],
        out_shape=[jax.ShapeDtypeStruct((s, d), out_dtype), jax.ShapeDtypeStruct((1, d), F32)],
        compiler_params=_params(("arbitrary",)), name=name,
    )(*args)


def _split3(x):
    hi = x.astype(BF16)
    r = x - hi.astype(F32)
    mid = r.astype(BF16)
    lo = (r - mid.astype(F32)).astype(BF16)
    return hi, mid, lo


def _tri_dot(x, tri):
    hi, mid, lo = _split3(x)
    nn = (((1,), (0,)), ((), ()))
    out = lax.dot_general(hi, tri, nn, preferred_element_type=F32)
    out += lax.dot_general(mid, tri, nn, preferred_element_type=F32)
    out += lax.dot_general(lo, tri, nn, preferred_element_type=F32)
    return out


_NT = (((1,), (1,)), ((), ()))
_NN = (((1,), (0,)), ((), ()))
_TN = (((0,), (0,)), ((), ()))


def _log_sigmoids(z):
    l1p = jnp.log1p(jnp.exp(-jnp.abs(z)))
    return jnp.minimum(z, 0.0) - l1p, jnp.minimum(-z, 0.0) - l1p


def _attn_fwd(qkv, name):
    s = qkv.shape[0]
    blk = ATT_BLK
    nq = s // blk
    npair = N_HEADS // 2
    scale = HEAD_DIM ** -0.5

    def body(q_ref, k_ref, v_ref, o_ref, l_ref):
        qi = pl.program_id(1)
        row = lax.broadcasted_iota(jnp.int32, (blk, blk), 0)
        col = lax.broadcasted_iota(jnp.int32, (blk, blk), 1)
        causal = col < row
        tri = (row > col).astype(BF16)
        lane_head = col // HEAD_DIM
        q2 = q_ref[...]
        out = jnp.zeros((blk, blk), F32)
        lout = jnp.zeros((blk, blk), F32)
        for hh in range(2):
            mask = lane_head == hh
            qh = jnp.where(mask, q2, jnp.zeros_like(q2))

            def block(kb, carry, diag):
                rsum, acc = carry
                r0 = pl.multiple_of(kb * blk, blk)
                k2 = k_ref[pl.ds(r0, blk), :]
                v2 = v_ref[pl.ds(r0, blk), :]
                vh = jnp.where(mask, v2, jnp.zeros_like(v2))
                z = lax.dot_general(qh, k2, _NT, preferred_element_type=F32) * scale
                lb, lm = _log_sigmoids(z)
                if diag:
                    lm = jnp.where(causal, lm, 0.0)
                a = lb + _tri_dot(lm, tri) + rsum
                w = jnp.exp(a)
                if diag:
                    w = jnp.where(causal, w, 0.0)
                acc = acc + lax.dot_general(w.astype(BF16), vh, _NN, preferred_element_type=F32)
                rsum = rsum + jnp.sum(lm, axis=1, keepdims=True)
                return rsum, acc

            carry = block(qi, (jnp.zeros((blk, 1), F32), jnp.zeros((blk, blk), F32)), True)
            rsum, acc = lax.fori_loop(0, qi, lambda i, c: block(qi - 1 - i, c, False), carry)
            out = out + acc
            lout = jnp.where(mask, rsum, lout)
        o_ref[...] = out.astype(BF16)
        l_ref[...] = lout

    return pl.pallas_call(
        body, grid=(npair, nq),
        in_specs=[pl.BlockSpec((blk, LANES), lambda h, i: (i, h)),
                  pl.BlockSpec((s, LANES), lambda h, i: (0, npair + h)),
                  pl.BlockSpec((s, LANES), lambda h, i: (0, 2 * npair + h))],
        out_specs=[pl.BlockSpec((blk, LANES), lambda h, i: (i, h)), pl.BlockSpec((blk, LANES), lambda h, i: (i, h))],
        out_shape=[jax.ShapeDtypeStruct((s, D_MODEL), BF16), jax.ShapeDtypeStruct((s, D_MODEL), F32)],
        compiler_params=_params(("parallel", "arbitrary")), name=name,
    )(qkv, qkv, qkv)


def _attn_bwd(qkv, do, lsum, name):
    s = qkv.shape[0]
    blk = ATT_BLK
    nq = s // blk
    npair = N_HEADS // 2
    scale = HEAD_DIM ** -0.5

    def body(q_ref, k_ref, v_ref, do_ref, l_ref, dq_ref, dk_ref, dv_ref):
        qi = pl.program_id(1)

        @pl.when(qi == 0)
        def _():
            dk_ref[...] = jnp.zeros_like(dk_ref)
            dv_ref[...] = jnp.zeros_like(dv_ref)

        row = lax.broadcasted_iota(jnp.int32, (blk, blk), 0)
        col = lax.broadcasted_iota(jnp.int32, (blk, blk), 1)
        causal = col < row
        tri_in = (row <= col).astype(BF16)
        tri_ex = (row < col).astype(BF16)
        lane_head = col // HEAD_DIM
        q2 = q_ref[...]
        do2 = do_ref[...]
        l2 = l_ref[...]
        masks = [lane_head == hh for hh in range(2)]
        qh = [jnp.where(m, q2, jnp.zeros_like(q2)) for m in masks]
        doh = [jnp.where(m, do2, jnp.zeros_like(do2)) for m in masks]
        ltot = [l2[:, hh * HEAD_DIM:hh * HEAD_DIM + 1] for hh in range(2)]

        def block(kb, carry, diag):
            r0 = pl.multiple_of(kb * blk, blk)
            k2 = k_ref[pl.ds(r0, blk), :]
            v2 = v_ref[pl.ds(r0, blk), :]
            dq, state = carry
            new_state = []
            dk_blk = jnp.zeros((blk, blk), F32)
            dv_blk = jnp.zeros((blk, blk), F32)
            for hh in range(2):
                pre_lm, pre_da = state[hh]
                kh = jnp.where(masks[hh], k2, jnp.zeros_like(k2))
                z = lax.dot_general(qh[hh], k2, _NT, preferred_element_type=F32) * scale
                lb, lm = _log_sigmoids(z)
                if diag:
                    lm = jnp.where(causal, lm, 0.0)
                a = lb + (ltot[hh] - pre_lm) - _tri_dot(lm, tri_in)
                w = jnp.exp(a)
                if diag:
                    w = jnp.where(causal, w, 0.0)
                dw = lax.dot_general(doh[hh], v2, _NT, preferred_element_type=F32)
                da = w * dw
                cpre = pre_da + _tri_dot(da, tri_ex)
                sig = jnp.exp(lb)
                dz = da - (da + cpre) * sig
                if diag:
                    dz = jnp.where(causal, dz, 0.0)
                dzb = (dz * scale).astype(BF16)
                wb = w.astype(BF16)
                dq = dq + lax.dot_general(dzb, kh, _NN, preferred_element_type=F32)
                dk_blk = dk_blk + lax.dot_general(dzb, qh[hh], _TN, preferred_element_type=F32)
                dv_blk = dv_blk + lax.dot_general(wb, doh[hh], _TN, preferred_element_type=F32)
                new_state.append((pre_lm + jnp.sum(lm, axis=1, keepdims=True),
                                  pre_da + jnp.sum(da, axis=1, keepdims=True)))
            dk_ref[pl.ds(r0, blk), :] += dk_blk
            dv_ref[pl.ds(r0, blk), :] += dv_blk
            return dq, tuple(new_state)

        zero = jnp.zeros((blk, 1), F32)
        init = (jnp.zeros((blk, blk), F32), ((zero, zero), (zero, zero)))
        carry = lax.fori_loop(0, qi, lambda i, c: block(i, c, False), init)
        dq, _ = block(qi, carry, True)
        dq_ref[...] = dq.astype(BF16)

    qblk = pl.BlockSpec((blk, LANES), lambda h, i: (i, h))
    full = pl.BlockSpec((s, LANES), lambda h, i: (0, h))
    return pl.pallas_call(
        body, grid=(npair, nq),
        in_specs=[qblk,
                  pl.BlockSpec((s, LANES), lambda h, i: (0, npair + h)),
                  pl.BlockSpec((s, LANES), lambda h, i: (0, 2 * npair + h)),
                  qblk, qblk],
        out_specs=[qblk, full, full],
        out_shape=[jax.ShapeDtypeStruct((s, D_MODEL), BF16), jax.ShapeDtypeStruct((s, D_MODEL), F32),
                   jax.ShapeDtypeStruct((s, D_MODEL), F32)],
        compiler_params=_params(("parallel", "arbitrary")), name=name,
    )(qkv, qkv, qkv, do, lsum)


def _prev_halo_spec(t, c, col):
    return pl.BlockSpec((HALO, c), lambda *g: (jnp.maximum(g[0] * (t // HALO) - 1, 0), col(*g)))


def _conv_taps(scr, w_ref, b_ref, kw, rows, lead):
    out = b_ref[...] + w_ref[kw - 1:kw, :] * scr[pl.ds(lead, rows), :]
    for k in range(kw - 1):
        out = out + w_ref[k:k + 1, :] * scr[pl.ds(lead - (kw - 1) + k, rows), :]
    return out


def _ffn_act_fwd(pre, cw, cb, name):
    s, c2 = pre.shape
    t = _tile(s, 256)
    cb2 = 2 * FFN_COL
    ncol = c2 // cb2

    def body(x_ref, halo_ref, w_ref, b_ref, o_ref, scr):
        i = pl.program_id(0)
        scr[0:HALO, :] = jnp.where(i > 0, halo_ref[...], 0.0)
        scr[HALO:HALO + t, :] = x_ref[...]
        u = _conv_taps(scr, w_ref, b_ref, FFN_CONV_W, t, HALO)
        o_ref[...] = (_gelu(u[:, :FFN_COL]) * u[:, FFN_COL:]).astype(BF16)

    return pl.pallas_call(
        body, grid=(s // t, ncol),
        in_specs=[pl.BlockSpec((t, cb2), lambda i, j: (i, j)),
                  _prev_halo_spec(t, cb2, lambda i, j: j),
                  pl.BlockSpec((FFN_CONV_W, cb2), lambda i, j: (0, j)),
                  pl.BlockSpec((1, cb2), lambda i, j: (0, j))],
        out_specs=pl.BlockSpec((t, FFN_COL), lambda i, j: (i, j)),
        out_shape=jax.ShapeDtypeStruct((s, c2 // 2), BF16),
        scratch_shapes=[pltpu.VMEM((t + HALO, cb2), F32)],
        compiler_params=_params(("parallel", "parallel")), name=name,
    )(pre, pre, cw, cb)


def _ffn_act_bwd(pre, dact, cw, cb, name):
    s, c2 = pre.shape
    t = _tile(s, 256)
    cb2 = 2 * FFN_COL
    ncol = c2 // cb2
    nt = s // t
    kw = FFN_CONV_W
    ext = t + HALO

    def body(x_ref, prev_ref, next_ref, da_ref, dan_ref, w_ref, b_ref, dx_ref, dw_ref, db_ref, xs, das, dus):
        i = pl.program_id(1)

        @pl.when(i == 0)
        def _():
            dw_ref[...] = jnp.zeros_like(dw_ref)
            db_ref[...] = jnp.zeros_like(db_ref)

        xs[0:HALO, :] = jnp.where(i > 0, prev_ref[...], 0.0)
        xs[HALO:HALO + t, :] = x_ref[...]
        xs[HALO + t:HALO + ext, :] = next_ref[...]
        das[0:t, :] = da_ref[...]
        das[t:ext, :] = jnp.where(i < nt - 1, dan_ref[...], 0.0)
        u = _conv_taps(xs, w_ref, b_ref, kw, ext, HALO)
        gel, dgel = _gelu_and_grad(u[:, :FFN_COL])
        dav = das[...]
        dus[:, :FFN_COL] = dav * u[:, FFN_COL:] * dgel
        dus[:, FFN_COL:] = dav * gel
        dx = w_ref[kw - 1:kw, :] * dus[pl.ds(0, t), :]
        for k in range(kw - 1):
            dx = dx + w_ref[k:k + 1, :] * dus[pl.ds(kw - 1 - k, t), :]
        dx_ref[...] = dx.astype(BF16)
        du = dus[pl.ds(0, t), :]
        db_ref[...] += jnp.sum(du, axis=0, keepdims=True)
        for k in range(kw):
            dw_ref[k:k + 1, :] += jnp.sum(du * xs[pl.ds(HALO - (kw - 1) + k, t), :], axis=0, keepdims=True)

    nxt = lambda c: pl.BlockSpec((HALO, c), lambda j, i: (jnp.minimum((i + 1) * (t // HALO), s // HALO - 1), j))
    return pl.pallas_call(
        body, grid=(ncol, nt),
        in_specs=[pl.BlockSpec((t, cb2), lambda j, i: (i, j)),
                  pl.BlockSpec((HALO, cb2), lambda j, i: (jnp.maximum(i * (t // HALO) - 1, 0), j)),
                  nxt(cb2),
                  pl.BlockSpec((t, FFN_COL), lambda j, i: (i, j)),
                  nxt(FFN_COL),
                  pl.BlockSpec((kw, cb2), lambda j, i: (0, j)),
                  pl.BlockSpec((1, cb2), lambda j, i: (0, j))],
        out_specs=[pl.BlockSpec((t, cb2), lambda j, i: (i, j)),
                   pl.BlockSpec((kw, cb2), lambda j, i: (0, j)),
                   pl.BlockSpec((1, cb2), lambda j, i: (0, j))],
        out_shape=[jax.ShapeDtypeStruct((s, c2), BF16), jax.ShapeDtypeStruct((kw, c2), F32),
                   jax.ShapeDtypeStruct((1, c2), F32)],
        scratch_shapes=[pltpu.VMEM((ext + HALO, cb2), F32), pltpu.VMEM((ext, FFN_COL), F32),
                        pltpu.VMEM((ext, cb2), F32)],
        compiler_params=_params(("parallel", "arbitrary")), name=name,
    )(pre, pre, pre, dact, dact, cw, cb)


def _rg_conv_fwd(proj, cw, cb, name):
    s = proj.shape[0]
    c = D_RNN_PAD
    t = _tile(s, 256)

    def body(x_ref, halo_ref, w_ref, b_ref, o_ref, ob_ref, scr):
        i = pl.program_id(0)
        scr[0:HALO, :] = jnp.where(i > 0, halo_ref[...], 0.0)
        scr[HALO:HALO + t, :] = x_ref[...]
        u = _conv_taps(scr, w_ref, b_ref, RG_CONV_W, t, HALO)
        o_ref[...] = u
        ob_ref[...] = u.astype(BF16)

    row = pl.BlockSpec((t, c), lambda i: (i, 0))
    return pl.pallas_call(
        body, grid=(s // t,),
        in_specs=[pl.BlockSpec((t, c), lambda i: (i, 1)),
                  _prev_halo_spec(t, c, lambda i: 1),
                  pl.BlockSpec((RG_CONV_W, c), lambda i: (0, 0)),
                  pl.BlockSpec((1, c), lambda i: (0, 0))],
        out_specs=[row, row],
        out_shape=[jax.ShapeDtypeStruct((s, c), F32), jax.ShapeDtypeStruct((s, c), BF16)],
        scratch_shapes=[pltpu.VMEM((t + HALO, c), F32)],
        compiler_params=_params(("parallel",)), name=name,
    )(proj, proj, cw, cb)


def _neg_expm1(x):
    y = jnp.exp(x)
    ly = jnp.log(y)
    safe = jnp.where(ly == 0.0, 1.0, ly)
    em = jnp.where(y == 1.0, x, (y - 1.0) * x / safe)
    em = jnp.where(x < -30.0, -1.0, em)
    return -em


def _rg_gates(pre_ax, rec, ba, bx, lam):
    c = D_RNN_PAD
    r = jax.nn.sigmoid(pre_ax[:, :c] + ba)
    ig = jax.nn.sigmoid(pre_ax[:, c:] + bx)
    sp = jnp.maximum(-lam, 0.0) + jnp.log1p(jnp.exp(-jnp.abs(lam)))
    log_a = (-RG_C) * r * sp
    a = jnp.exp(log_a)
    m2 = _neg_expm1(2.0 * log_a)
    mult = jnp.sqrt(m2)
    return r, ig, sp, a, m2, mult


def _rg_scan_fwd(pre_ax, rec, proj, ba, bx, lam, name):
    s = rec.shape[0]
    c = D_RNN_PAD
    t = _tile(s, 256)

    def body(pax_ref, rec_ref, gate_ref, ba_ref, bx_ref, lam_ref, h_ref, gh_ref, a_scr, u_scr, carry):
        @pl.when(pl.program_id(0) == 0)
        def _():
            carry[...] = jnp.zeros_like(carry)

        recv = rec_ref[...]
        _, ig, _, a, _, mult = _rg_gates(pax_ref[...], recv, ba_ref[...], bx_ref[...], lam_ref[...])
        a_scr[...] = a
        u_scr[...] = mult * ig * recv
        rowid = lax.broadcasted_iota(jnp.int32, (8, c), 0)

        def group(gi, h):
            r0 = pl.multiple_of(gi * 8, 8)
            a8 = a_scr[pl.ds(r0, 8), :]
            u8 = u_scr[pl.ds(r0, 8), :]
            out = jnp.zeros((8, c), F32)
            for j in range(8):
                h = jnp.broadcast_to(a8[j:j + 1, :], (8, c)) * h + jnp.broadcast_to(u8[j:j + 1, :], (8, c))
                out = jnp.where(rowid == j, h, out)
            h_ref[pl.ds(r0, 8), :] = out
            return h

        carry[...] = lax.fori_loop(0, t // 8, group, carry[...])
        gh_ref[...] = (_gelu(gate_ref[...]) * h_ref[...]).astype(BF16)

    row = pl.BlockSpec((t, c), lambda i: (i, 0))
    vec = pl.BlockSpec((1, c), lambda i: (0, 0))
    return pl.pallas_call(
        body, grid=(s // t,),
        in_specs=[pl.BlockSpec((t, 2 * c), lambda i: (i, 0)), row, row, vec, vec, vec],
        out_specs=[row, row],
        out_shape=[jax.ShapeDtypeStruct((s, c), F32), jax.ShapeDtypeStruct((s, c), BF16)],
        scratch_shapes=[pltpu.VMEM((t, c), F32), pltpu.VMEM((t, c), F32), pltpu.VMEM((8, c), F32)],
        compiler_params=_params(("arbitrary",)), name=name,
    )(pre_ax, rec, proj, ba, bx, lam)


def _rg_scan_bwd(dgh, proj, h, pre_ax, rec, ba, bx, lam, name):
    s = rec.shape[0]
    c = D_RNN_PAD
    t = _tile(s, 256)
    nt = s // t

    def body(dgh_ref, gate_ref, h_ref, hprev_ref, pax_ref, rec_ref, ba_ref, bx_ref, lam_ref,
             dgate_ref, dpax_ref, drec_ref, dba_ref, dbx_ref, dlam_ref, a_scr, dh_scr, g_scr, hp_scr, carry):
        i = pl.program_id(0)
        ti = nt - 1 - i

        @pl.when(i == 0)
        def _():
            carry[...] = jnp.zeros_like(carry)
            dba_ref[...] = jnp.zeros_like(dba_ref)
            dbx_ref[...] = jnp.zeros_like(dbx_ref)
            dlam_ref[...] = jnp.zeros_like(dlam_ref)

        recv = rec_ref[...]
        lam = lam_ref[...]
        r, ig, sp, a, m2, mult = _rg_gates(pax_ref[...], recv, ba_ref[...], bx_ref[...], lam)
        gel, dgel = _gelu_and_grad(gate_ref[...])
        dghv = dgh_ref[...]
        hv = h_ref[...]
        dgate_ref[...] = (dghv * hv * dgel).astype(BF16)
        a_scr[...] = a
        dh_scr[...] = dghv * gel
        rowid = lax.broadcasted_iota(jnp.int32, (8, c), 0)

        def group(gi, cr):
            r0 = pl.multiple_of((t // 8 - 1 - gi) * 8, 8)
            a8 = a_scr[pl.ds(r0, 8), :]
            d8 = dh_scr[pl.ds(r0, 8), :]
            out = jnp.zeros((8, c), F32)
            for j in range(7, -1, -1):
                g = jnp.broadcast_to(d8[j:j + 1, :], (8, c)) + cr
                out = jnp.where(rowid == j, g, out)
                cr = jnp.broadcast_to(a8[j:j + 1, :], (8, c)) * g
            g_scr[pl.ds(r0, 8), :] = out
            return cr

        carry[...] = lax.fori_loop(0, t // 8, group, carry[...])
        g = g_scr[...]
        hp_scr[0:HALO, :] = jnp.where(ti > 0, hprev_ref[...], 0.0)
        hp_scr[HALO:HALO + t, :] = hv
        da = g * hp_scr[pl.ds(HALO - 1, t), :]
        dmult = g * ig * recv
        dig = g * mult * recv
        drec_ref[...] = g * mult * ig
        dm2 = dmult * 0.5 / mult
        dloga = da * a - 2.0 * (1.0 - m2) * dm2
        dr = dloga * ((-RG_C) * sp)
        dsp = jnp.sum(dloga * ((-RG_C) * r), axis=0, keepdims=True)
        dpa = dr * r * (1.0 - r)
        dpx = dig * ig * (1.0 - ig)
        dpax_ref[:, :c] = dpa.astype(BF16)
        dpax_ref[:, c:] = dpx.astype(BF16)
        dba_ref[...] += jnp.sum(dpa, axis=0, keepdims=True)
        dbx_ref[...] += jnp.sum(dpx, axis=0, keepdims=True)
        dlam_ref[...] += dsp * (-jnp.exp(-(lam + sp)))

    row = pl.BlockSpec((t, c), lambda i: (nt - 1 - i, 0))
    vec = pl.BlockSpec((1, c), lambda i: (0, 0))
    return pl.pallas_call(
        body, grid=(nt,),
        in_specs=[row, row, row,
                  pl.BlockSpec((HALO, c), lambda i: (jnp.maximum((nt - 1 - i) * (t // HALO) - 1, 0), 0)),
                  pl.BlockSpec((t, 2 * c), lambda i: (nt - 1 - i, 0)), row, vec, vec, vec],
        out_specs=[row, pl.BlockSpec((t, 2 * c), lambda i: (nt - 1 - i, 0)), row, vec, vec, vec],
        out_shape=[jax.ShapeDtypeStruct((s, c), BF16), jax.ShapeDtypeStruct((s, 2 * c), BF16),
                   jax.ShapeDtypeStruct((s, c), F32), jax.ShapeDtypeStruct((1, c), F32),
                   jax.ShapeDtypeStruct((1, c), F32), jax.ShapeDtypeStruct((1, c), F32)],
        scratch_shapes=[pltpu.VMEM((t, c), F32), pltpu.VMEM((t, c), F32), pltpu.VMEM((t, c), F32),
                        pltpu.VMEM((t + HALO, c), F32), pltpu.VMEM((8, c), F32)],
        compiler_params=_params(("arbitrary",)), name=name,
    )(dgh, proj, h, h, pre_ax, rec, ba, bx, lam)


def _rg_conv_bwd(drec_a, drec_b, proj, cw, name):
    s = drec_a.shape[0]
    c = D_RNN_PAD
    t = _tile(s, 256)
    nt = s // t
    kw = RG_CONV_W
    ext = t + HALO

    def body(da_ref, dan_ref, db_ref, dbn_ref, x_ref, prev_ref, w_ref, dx_ref, dw_ref, dbias_ref, xs, dus):
        i = pl.program_id(0)

        @pl.when(i == 0)
        def _():
            dw_ref[...] = jnp.zeros_like(dw_ref)
            dbias_ref[...] = jnp.zeros_like(dbias_ref)

        xs[0:HALO, :] = jnp.where(i > 0, prev_ref[...], 0.0)
        xs[HALO:HALO + t, :] = x_ref[...]
        du = da_ref[...] + db_ref[...]
        dus[0:t, :] = du
        dus[t:ext, :] = jnp.where(i < nt - 1, dan_ref[...] + dbn_ref[...], 0.0)
        dx = w_ref[kw - 1:kw, :] * du
        for k in range(kw - 1):
            dx = dx + w_ref[k:k + 1, :] * dus[pl.ds(kw - 1 - k, t), :]
        dx_ref[...] = dx.astype(BF16)
        dbias_ref[...] += jnp.sum(du, axis=0, keepdims=True)
        for k in range(kw):
            dw_ref[k:k + 1, :] += jnp.sum(du * xs[pl.ds(HALO - (kw - 1) + k, t), :], axis=0, keepdims=True)

    row = pl.BlockSpec((t, c), lambda i: (i, 0))
    nxt = pl.BlockSpec((HALO, c), lambda i: (jnp.minimum((i + 1) * (t // HALO), s // HALO - 1), 0))
    return pl.pallas_call(
        body, grid=(nt,),
        in_specs=[row, nxt, row, nxt,
                  pl.BlockSpec((t, c), lambda i: (i, 1)),
                  _prev_halo_spec(t, c, lambda i: 1),
                  pl.BlockSpec((kw, c), lambda i: (0, 0))],
        out_specs=[row, pl.BlockSpec((kw, c), lambda i: (0, 0)), pl.BlockSpec((1, c), lambda i: (0, 0))],
        out_shape=[jax.ShapeDtypeStruct((s, c), BF16), jax.ShapeDtypeStruct((kw, c), F32),
                   jax.ShapeDtypeStruct((1, c), F32)],
        scratch_shapes=[pltpu.VMEM((t + HALO, c), F32), pltpu.VMEM((ext, c), F32)],
        compiler_params=_params(("arbitrary",)), name=name,
    )(drec_a, drec_a, drec_b, drec_b, proj, proj, cw)


def _exchange(operands, name):
    nop = len(operands)
    sames = [same for _, same in operands]

    def body(*refs):
        srcs = refs[:nop]
        outs = refs[nop:2 * nop]
        send_sems, recv_sems, local_sems = refs[2 * nop:]
        x, y, c = lax.axis_index("x"), lax.axis_index("y"), lax.axis_index("c")
        me = 4 * x + 2 * y + c
        local = []
        for o in range(nop):
            src = srcs[o] if sames[o] else srcs[o].at[me]
            cp = pltpu.make_async_copy(src, outs[o].at[me], local_sems.at[o])
            cp.start()
            local.append(cp)
        copies = []
        for k in range(1, N_DEV):
            px = (x + ((k >> 2) & 1)) % 2
            py = (y + ((k >> 1) & 1)) % 2
            pc = (c + (k & 1)) % 2
            peer = 4 * px + 2 * py + pc
            for o in range(nop):
                sem = (k - 1) * nop + o
                send = pltpu.make_async_remote_copy(
                    src_ref=srcs[o] if sames[o] else srcs[o].at[peer], dst_ref=outs[o].at[me],
                    send_sem=send_sems.at[sem], recv_sem=recv_sems.at[sem],
                    device_id=(px, py, pc), device_id_type=pl.DeviceIdType.MESH)
                send.start()
                recv = pltpu.make_async_remote_copy(
                    src_ref=srcs[o] if sames[o] else srcs[o].at[peer], dst_ref=outs[o].at[peer],
                    send_sem=send_sems.at[sem], recv_sem=recv_sems.at[sem],
                    device_id=(px, py, pc), device_id_type=pl.DeviceIdType.MESH)
                copies.append((send, recv))
        for send, recv in copies:
            send.wait_send()
            recv.wait_recv()
        for cp in local:
            cp.wait()

    anyspec = pl.BlockSpec(memory_space=pl.ANY)
    shapes = []
    for arr, same in operands:
        n = arr.shape[0] if same else arr.shape[1]
        shapes.append(jax.ShapeDtypeStruct((N_DEV, n, LANES), arr.dtype))
    return pl.pallas_call(
        body, in_specs=[anyspec] * nop, out_specs=[anyspec] * nop, out_shape=shapes,
        scratch_shapes=[pltpu.SemaphoreType.DMA(((N_DEV - 1) * nop,)), pltpu.SemaphoreType.DMA(((N_DEV - 1) * nop,)),
                        pltpu.SemaphoreType.DMA((nop,))],
        compiler_params=pltpu.CompilerParams(has_side_effects=True), name=name,
    )(*[arr for arr, _ in operands])


def _adam(parts, w, m, v, name):
    n = w.shape[0]
    t = _tile(n, 512)
    c1 = 1.0 - ADAM_B1 ** ADAM_STEP
    c2 = 1.0 - ADAM_B2 ** ADAM_STEP

    def body(p_ref, w_ref, m_ref, v_ref, g_ref, d_ref, mo_ref, vo_ref):
        g = p_ref[0]
        for k in range(1, N_DEV):
            g = g + p_ref[k]
        mn = ADAM_B1 * m_ref[...] + (1.0 - ADAM_B1) * g
        vn = ADAM_B2 * v_ref[...] + (1.0 - ADAM_B2) * (g * g)
        m_hat = mn / c1
        v_hat = vn / c2
        g_ref[...] = g
        d_ref[...] = (-ADAM_LR) * (m_hat / (jnp.sqrt(v_hat) + ADAM_EPS) + ADAM_WD * w_ref[...])
        mo_ref[...] = mn
        vo_ref[...] = vn

    row = pl.BlockSpec((t, LANES), lambda i: (i, 0))
    out = jax.ShapeDtypeStruct((n, LANES), F32)
    return pl.pallas_call(
        body, grid=(n // t,),
        in_specs=[pl.BlockSpec((N_DEV, t, LANES), lambda i: (0, i, 0)), row, row, row],
        out_specs=[row, row, row, row], out_shape=[out, out, out, out],
        compiler_params=_params(("parallel",)), name=name,
    )(parts, w, m, v)


def _pack(pieces, row_mult, lead=False):
    if lead:
        flat = jnp.concatenate([p.reshape(N_DEV, -1) for p in pieces], axis=1)
        n = flat.shape[1]
    else:
        flat = jnp.concatenate([p.reshape(-1) for p in pieces])
        n = flat.shape[0]
    group = row_mult * LANES
    total = -(-n // group) * group
    if lead:
        flat = jnp.pad(flat, ((0, 0), (0, total - n)))
        return flat.reshape(N_DEV, total // LANES, LANES)
    flat = jnp.pad(flat, (0, total - n))
    return flat.reshape(total // LANES, LANES)


def _unpack(buf, shapes, lead=False):
    flat = buf.reshape(N_DEV, -1) if lead else buf.reshape(-1)
    out = []
    off = 0
    for shp in shapes:
        size = math.prod(shp)
        if lead:
            out.append(flat[:, off:off + size].reshape((N_DEV,) + tuple(shp)))
        else:
            out.append(flat[off:off + size].reshape(shp))
        off += size
    return out


BIG = ["attn_w_qkv", "attn_w_o", "rg_w_in", "rg_w_out", "ffn_w_up", "ffn_w_down"]
SMALL = ["rg_conv_w", "rg_conv_b", "rg_b_a", "rg_b_x", "rg_lambda", "ffn_conv_w"]
REPL = ["rg_w_a", "rg_w_x", "ffn_conv_b", "mix_pre_g", "mix_post_g", "ffn_pre_g", "ffn_post_g"]
ORDER = ["attn_w_qkv", "attn_w_o", "rg_w_in", "rg_conv_w", "rg_conv_b", "rg_w_a", "rg_b_a", "rg_w_x", "rg_b_x",
         "rg_lambda", "rg_w_out", "ffn_w_up", "ffn_conv_w", "ffn_conv_b", "ffn_w_down", "mix_pre_g",
         "mix_post_g", "ffn_pre_g", "ffn_post_g"]
SHARD_AXIS = {"attn_w_qkv": 2, "attn_w_o": 1, "rg_w_in": 2, "rg_w_out": 1, "ffn_w_up": 2, "ffn_w_down": 1,
              "rg_conv_w": 2, "rg_conv_b": 1, "rg_b_a": 1, "rg_b_x": 1, "rg_lambda": 1, "ffn_conv_w": 2}
BIG_ROWS = 1024
SMALL_ROWS = 64
REPL_ROWS = 128


def _assemble(stacked, axis):
    moved = jnp.moveaxis(stacked, 0, axis)
    shp = list(moved.shape)
    shp[axis:axis + 2] = [shp[axis] * shp[axis + 1]]
    return moved.reshape(shp)


def _split(full, axis):
    shp = list(full.shape)
    shp[axis:axis + 1] = [N_DEV, shp[axis] // N_DEV]
    return jnp.moveaxis(full.reshape(shp), axis, 0)


def _block_diag_pad(w):
    out = jnp.zeros((D_RNN_PAD, D_RNN_PAD), w.dtype)
    for n in range(RG_BLOCKS):
        o = n * RG_BLOCK_W
        out = lax.dynamic_update_slice(out, w[n], (o, o))
    return out


def _block_diag_take(dense):
    return jnp.stack([dense[n * RG_BLOCK_W:(n + 1) * RG_BLOCK_W, n * RG_BLOCK_W:(n + 1) * RG_BLOCK_W]
                      for n in range(RG_BLOCKS)])


def _pad_cols(a, width):
    return jnp.pad(a, ((0, 0), (0, width - a.shape[1])))


def _interleave(a):
    lead = a.shape[:-1]
    nb = D_FF // FFN_COL
    return jnp.swapaxes(a.reshape(lead + (2, nb, FFN_COL)), -3, -2).reshape(lead + (2 * D_FF,))


def _deinterleave(a):
    lead = a.shape[:-1]
    nb = D_FF // FFN_COL
    return jnp.swapaxes(a.reshape(lead + (nb, 2, FFN_COL)), -3, -2).reshape(lead + (2 * D_FF,))


def _ffn_fwd(hn, wp, i):
    pre = _matmul(hn, wp["up"][i], name=f"ffn{i}_up")
    act = _ffn_act_fwd(pre, wp["ffn_cw"][i], wp["ffn_cb"][i], name=f"ffn{i}_act")
    f = _matmul(act, wp["down"][i], name=f"ffn{i}_down")
    return pre, act, f


def _ffn_bwd(df, hn, pre, act, wp, i):
    dact = _matmul(df, wp["down"][i], tb=True, name=f"ffn{i}_dact")
    d_down = _matmul(act, df, ta=True, name=f"ffn{i}_ddown")
    dpre, dcw, dcb = _ffn_act_bwd(pre, dact, wp["ffn_cw"][i], wp["ffn_cb"][i], name=f"ffn{i}_act_bwd")
    d_up = _matmul(hn, dpre, ta=True, name=f"ffn{i}_dup")
    dhn = _matmul(dpre, wp["up"][i], tb=True, name=f"ffn{i}_dhn")
    return dhn, d_up, d_down, dcw, dcb


def _local_step(x, tgt, wp):
    g = wp["gains"]
    gr = {}

    hn0 = _rms_fwd(x, g["mix_pre"][0], name="l0_mix_pre")
    qkv = _matmul(hn0, wp["qkv"], out_dtype=BF16, name="attn_qkv")
    o, lsum = _attn_fwd(qkv, name="attn_fwd")
    f0 = _matmul(o, wp["wo"], name="attn_out")
    x1, hn1 = _resid_norm(x, f0, g["mix_post"][0], g["ffn_pre"][0], name="l0_mix_post")
    pre0, act0, f1 = _ffn_fwd(hn1, wp, 0)
    x2, hn2 = _resid_norm(x1, f1, g["ffn_post"][0], g["mix_pre"][1], name="l0_ffn_post")
    proj = _matmul(hn2, wp["w_in"], name="rg_in")
    rec, rec_b = _rg_conv_fwd(proj, wp["rg_cw"], wp["rg_cb"], name="rg_conv")
    pre_ax = _matmul(rec_b, wp["w_ax"], name="rg_gates")
    h, gh = _rg_scan_fwd(pre_ax, rec, proj, wp["rg_ba"], wp["rg_bx"], wp["rg_lam"], name="rg_scan")
    f2 = _matmul(gh, wp["w_out"], name="rg_out")
    x3, hn3 = _resid_norm(x2, f2, g["mix_post"][1], g["ffn_pre"][1], name="l1_mix_post")
    pre1, act1, f3 = _ffn_fwd(hn3, wp, 1)
    dy, sq = _resid_loss(x3, f3, g["ffn_post"][1], tgt, name="l1_ffn_post_loss")

    ffn_dup, ffn_ddown, ffn_dcw, ffn_dcb = [None, None], [None, None], [None, None], [None, None]
    d_ffn_post, d_ffn_pre, d_mix_post, d_mix_pre = [None, None], [None, None], [None, None], [None, None]

    df3, d_ffn_post[1] = _rms_bwd(f3, g["ffn_post"][1], dy, None, BF16, name="l1_ffn_post_bwd")
    dhn3, ffn_dup[1], ffn_ddown[1], ffn_dcw[1], ffn_dcb[1] = _ffn_bwd(df3, hn3, pre1, act1, wp, 1)
    dx3, d_ffn_pre[1] = _rms_bwd(x3, g["ffn_pre"][1], dhn3, dy, F32, name="l1_ffn_pre_bwd")

    df2, d_mix_post[1] = _rms_bwd(f2, g["mix_post"][1], dx3, None, BF16, name="l1_mix_post_bwd")
    dgh = _matmul(df2, wp["w_out"], tb=True, name="rg_dgh")
    gr["w_out"] = _matmul(gh, df2, ta=True, name="rg_dwout")
    dgate, dpax, drec_u, gr["rg_ba"], gr["rg_bx"], gr["rg_lam"] = _rg_scan_bwd(
        dgh, proj, h, pre_ax, rec, wp["rg_ba"], wp["rg_bx"], wp["rg_lam"], name="rg_scan_bwd")
    drec_g = _matmul(dpax, wp["w_ax"], tb=True, name="rg_drec")
    gr["w_ax"] = _matmul(rec_b, dpax, ta=True, name="rg_dwax")
    dproj_rec, gr["rg_cw"], gr["rg_cb"] = _rg_conv_bwd(drec_u, drec_g, proj, wp["rg_cw"], name="rg_conv_bwd")
    dproj = jnp.concatenate([dgate, dproj_rec], axis=1)
    gr["w_in"] = _matmul(hn2, dproj, ta=True, name="rg_dwin")
    dhn2 = _matmul(dproj, wp["w_in"], tb=True, name="rg_dhn")
    dx2, d_mix_pre[1] = _rms_bwd(x2, g["mix_pre"][1], dhn2, dx3, F32, name="l1_mix_pre_bwd")

    df1, d_ffn_post[0] = _rms_bwd(f1, g["ffn_post"][0], dx2, None, BF16, name="l0_ffn_post_bwd")
    dhn1, ffn_dup[0], ffn_ddown[0], ffn_dcw[0], ffn_dcb[0] = _ffn_bwd(df1, hn1, pre0, act0, wp, 0)
    dx1, d_ffn_pre[0] = _rms_bwd(x1, g["ffn_pre"][0], dhn1, dx2, F32, name="l0_ffn_pre_bwd")

    df0, d_mix_post[0] = _rms_bwd(f0, g["mix_post"][0], dx1, None, BF16, name="l0_mix_post_bwd")
    do = _matmul(df0, wp["wo"], tb=True, out_dtype=BF16, name="attn_do")
    gr["wo"] = _matmul(o, df0, ta=True, name="attn_dwo")
    dq, dk, dv = _attn_bwd(qkv, do, lsum, name="attn_bwd")
    dqkv = jnp.concatenate([dq, dk.astype(BF16), dv.astype(BF16)], axis=1)
    gr["qkv"] = _matmul(hn0, dqkv, ta=True, name="attn_dwqkv")
    dhn0 = _matmul(dqkv, wp["qkv"], tb=True, name="attn_dhn")
    dx0, d_mix_pre[0] = _rms_bwd(x, g["mix_pre"][0], dhn0, dx1, F32, name="l0_mix_pre_bwd")

    gr["up"] = ffn_dup
    gr["down"] = ffn_ddown
    gr["ffn_cw"] = ffn_dcw
    gr["ffn_cb"] = ffn_dcb
    gr["gains"] = {"mix_pre": d_mix_pre, "mix_post": d_mix_post, "ffn_pre": d_ffn_pre, "ffn_post": d_ffn_post}
    return sq[0, 0], dx0, gr


def _prepare(full):
    c, cp = D_RNN, D_RNN_PAD
    w_in = full["rg_w_in"][0]
    w_a = _block_diag_pad(full["rg_w_a"][0].astype(BF16))
    w_x = _block_diag_pad(full["rg_w_x"][0].astype(BF16))
    vec = lambda a: _pad_cols(a.reshape(1, c), cp)
    return {
        "qkv": full["attn_w_qkv"][0],
        "wo": full["attn_w_o"][0],
        "w_in": jnp.concatenate([_pad_cols(w_in[:, :c], cp), _pad_cols(w_in[:, c:], cp)], axis=1),
        "w_ax": jnp.concatenate([w_a, w_x], axis=1),
        "w_out": jnp.pad(full["rg_w_out"][0], ((0, cp - c), (0, 0))),
        "rg_cw": _pad_cols(full["rg_conv_w"][0], cp),
        "rg_cb": vec(full["rg_conv_b"][0]),
        "rg_ba": vec(full["rg_b_a"][0]),
        "rg_bx": vec(full["rg_b_x"][0]),
        "rg_lam": vec(full["rg_lambda"][0]),
        "up": [_interleave(full["ffn_w_up"][i]) for i in range(2)],
        "down": [full["ffn_w_down"][i] for i in range(2)],
        "ffn_cw": [_interleave(full["ffn_conv_w"][i]) for i in range(2)],
        "ffn_cb": [_interleave(full["ffn_conv_b"][i].reshape(1, -1)) for i in range(2)],
        "gains": {k: [full[k + "_g"][i].reshape(1, D_MODEL) for i in range(2)]
                  for k in ("mix_pre", "mix_post", "ffn_pre", "ffn_post")},
    }


def _natural_grads(gr):
    c, cp = D_RNN, D_RNN_PAD
    gg = gr["gains"]
    stack2 = lambda pair: jnp.stack([pair[0].reshape(-1), pair[1].reshape(-1)])
    return {
        "attn_w_qkv": gr["qkv"][None],
        "attn_w_o": gr["wo"][None],
        "rg_w_in": jnp.concatenate([gr["w_in"][:, :c], gr["w_in"][:, cp:cp + c]], axis=1)[None],
        "rg_conv_w": gr["rg_cw"][:, :c][None],
        "rg_conv_b": gr["rg_cb"][:, :c],
        "rg_w_a": _block_diag_take(gr["w_ax"][:, :cp])[None],
        "rg_b_a": gr["rg_ba"][:, :c],
        "rg_w_x": _block_diag_take(gr["w_ax"][:, cp:])[None],
        "rg_b_x": gr["rg_bx"][:, :c],
        "rg_lambda": gr["rg_lam"][:, :c],
        "rg_w_out": gr["w_out"][:c][None],
        "ffn_w_up": jnp.stack([_deinterleave(gr["up"][i]) for i in range(2)]),
        "ffn_conv_w": jnp.stack([_deinterleave(gr["ffn_cw"][i]) for i in range(2)]),
        "ffn_conv_b": jnp.stack([_deinterleave(gr["ffn_cb"][i]).reshape(-1) for i in range(2)]),
        "ffn_w_down": jnp.stack(gr["down"]),
        "mix_pre_g": stack2(gg["mix_pre"]),
        "mix_post_g": stack2(gg["mix_post"]),
        "ffn_pre_g": stack2(gg["ffn_pre"]),
        "ffn_post_g": stack2(gg["ffn_post"]),
    }


def _step(x, loss_target, w, m, v):
    big_shapes = [w[n].shape for n in BIG]
    small_shapes = [w[n].shape for n in SMALL]
    repl_shapes = [w[n].shape for n in REPL]
    wbig = _pack([w[n].astype(BF16) for n in BIG], BIG_ROWS)
    wsmall = _pack([w[n] for n in SMALL], SMALL_ROWS)
    gbig, gsmall = _exchange([(wbig, True), (wsmall, True)], name="gather_weights")
    full = {}
    for n, st in zip(BIG, _unpack(gbig, big_shapes, lead=True)):
        full[n] = _assemble(st, SHARD_AXIS[n])
    for n, st in zip(SMALL, _unpack(gsmall, small_shapes, lead=True)):
        full[n] = _assemble(st, SHARD_AXIS[n])
    for n in REPL:
        full[n] = w[n]

    sq, dx, gr = _local_step(x[0], loss_target[0], _prepare(full))
    grads = _natural_grads(gr)

    pbig = _pack([_split(grads[n], SHARD_AXIS[n]) for n in BIG], BIG_ROWS, lead=True)
    psmall = _pack([_split(grads[n], SHARD_AXIS[n]) for n in SMALL], SMALL_ROWS, lead=True)
    prepl = _pack([grads[n] for n in REPL], REPL_ROWS)
    rbig, rsmall, rrepl = _exchange([(pbig, False), (psmall, False), (prepl, True)], name="exchange_grads")

    out = {}
    for names, parts, rows, shapes, tag in ((BIG, rbig, BIG_ROWS, big_shapes, "big"),
                                            (SMALL, rsmall, SMALL_ROWS, small_shapes, "small"),
                                            (REPL, rrepl, REPL_ROWS, repl_shapes, "repl")):
        packs = [_pack([d[n] for n in names], rows) for d in (w, m, v)]
        res = _adam(parts, *packs, name="adamw_" + tag)
        for kind, buf in zip(("grad", "delta", "new_m", "new_v"), res):
            for n, a in zip(names, _unpack(buf, shapes)):
                out[(kind, n)] = a

    loss = lax.psum(sq * (0.5 / D_MODEL), ("x", "y", "c"))
    return loss, dx[None], out


def kernel(x, attn_w_qkv, attn_w_o, rg_w_in, rg_conv_w, rg_conv_b, rg_w_a, rg_b_a, rg_w_x, rg_b_x, rg_lambda, rg_w_out, ffn_w_up, ffn_conv_w, ffn_conv_b, ffn_w_down, mix_pre_g, mix_post_g, ffn_pre_g, ffn_post_g, loss_target, m_attn_w_qkv, m_attn_w_o, m_rg_w_in, m_rg_conv_w, m_rg_conv_b, m_rg_w_a, m_rg_b_a, m_rg_w_x, m_rg_b_x, m_rg_lambda, m_rg_w_out, m_ffn_w_up, m_ffn_conv_w, m_ffn_conv_b, m_ffn_w_down, m_mix_pre_g, m_mix_post_g, m_ffn_pre_g, m_ffn_post_g, v_attn_w_qkv, v_attn_w_o, v_rg_w_in, v_rg_conv_w, v_rg_conv_b, v_rg_w_a, v_rg_b_a, v_rg_w_x, v_rg_b_x, v_rg_lambda, v_rg_w_out, v_ffn_w_up, v_ffn_conv_w, v_ffn_conv_b, v_ffn_w_down, v_mix_pre_g, v_mix_post_g, v_ffn_pre_g, v_ffn_post_g):
    w = dict(zip(ORDER, (attn_w_qkv, attn_w_o, rg_w_in, rg_conv_w, rg_conv_b, rg_w_a, rg_b_a, rg_w_x, rg_b_x,
                         rg_lambda, rg_w_out, ffn_w_up, ffn_conv_w, ffn_conv_b, ffn_w_down, mix_pre_g,
                         mix_post_g, ffn_pre_g, ffn_post_g)))
    m = dict(zip(ORDER, (m_attn_w_qkv, m_attn_w_o, m_rg_w_in, m_rg_conv_w, m_rg_conv_b, m_rg_w_a, m_rg_b_a,
                         m_rg_w_x, m_rg_b_x, m_rg_lambda, m_rg_w_out, m_ffn_w_up, m_ffn_conv_w, m_ffn_conv_b,
                         m_ffn_w_down, m_mix_pre_g, m_mix_post_g, m_ffn_pre_g, m_ffn_post_g)))
    v = dict(zip(ORDER, (v_attn_w_qkv, v_attn_w_o, v_rg_w_in, v_rg_conv_w, v_rg_conv_b, v_rg_w_a, v_rg_b_a,
                         v_rg_w_x, v_rg_b_x, v_rg_lambda, v_rg_w_out, v_ffn_w_up, v_ffn_conv_w, v_ffn_conv_b,
                         v_ffn_w_down, v_mix_pre_g, v_mix_post_g, v_ffn_pre_g, v_ffn_post_g)))
    loss, dx, out = _step(x, loss_target, w, m, v)
    return (loss, dx, *[out[("grad", n)] for n in ORDER], *[out[("delta", n)] for n in ORDER],
            *[out[("new_m", n)] for n in ORDER], *[out[("new_v", n)] for n in ORDER])
```

```python
import functools
import math

import jax
import jax.numpy as jnp
from jax import lax
from jax.experimental import pallas as pl
from jax.experimental.pallas import tpu as pltpu

F32 = jnp.float32
BF16 = jnp.bfloat16

D_MODEL = 1024
N_HEADS = 16
HEAD_DIM = 64
D_RNN = 1344
D_RNN_PAD = 1408
RG_BLOCKS = 16
RG_BLOCK_W = 84
RG_CONV_W = 4
RG_C = 8.0
D_FF = 2816
FFN_CONV_W = 3
FFN_COL = 256
NORM_EPS = 1e-6
N_DEV = 8
LANES = 128
HALO = 8

ADAM_LR = 0.001
ADAM_B1 = 0.9
ADAM_B2 = 0.999
ADAM_EPS = 1e-08
ADAM_WD = 0.01
ADAM_STEP = 10

ATT_BLK = 256
VMEM_LIMIT = 56 * 1024 * 1024

_GELU_C = math.sqrt(2.0 / math.pi)


def _params(sem=None):
    return pltpu.CompilerParams(dimension_semantics=sem, vmem_limit_bytes=VMEM_LIMIT)


def _tile(n, cap, mult=8):
    if n <= cap:
        return n
    best = None
    for t in range(mult, cap + 1, mult):
        if n % t == 0:
            best = t
    assert best is not None, (n, cap, mult)
    return best


def _gelu_and_grad(x):
    x2 = x * x
    th = jnp.tanh(_GELU_C * (x + 0.044715 * (x2 * x)))
    cdf = 0.5 * (1.0 + th)
    gel = x * cdf
    dgel = cdf + 0.5 * x * (1.0 - th * th) * (_GELU_C * (1.0 + 3.0 * 0.044715 * x2))
    return gel, dgel


def _gelu(x):
    th = jnp.tanh(_GELU_C * (x + 0.044715 * (x * x * x)))
    return x * (0.5 * (1.0 + th))


def _matmul(a, b, *, ta=False, tb=False, out_dtype=F32, name):
    if ta:
        kdim, m = a.shape
    else:
        m, kdim = a.shape
    if tb:
        n, kb_ = b.shape
    else:
        kb_, n = b.shape
    assert kdim == kb_, (a.shape, b.shape, ta, tb)
    wide = lambda d: 1408 if d % 1408 == 0 else 1024
    tm = _tile(m, wide(m), 128) if ta else _tile(m, 512)
    tn = _tile(n, wide(n), 128)
    tk = _tile(kdim, 512, 128) if ta else _tile(kdim, wide(kdim), 128)
    nk = kdim // tk
    dims = (((0 if ta else 1,), (1 if tb else 0,)), ((), ()))

    def body(a_ref, b_ref, o_ref, acc_ref):
        part = lax.dot_general(a_ref[...], b_ref[...], dims, preferred_element_type=F32)
        if nk == 1:
            o_ref[...] = part.astype(out_dtype)
        else:
            k = pl.program_id(2)

            @pl.when(k == 0)
            def _():
                acc_ref[...] = part

            @pl.when(k > 0)
            def _():
                acc_ref[...] += part

            @pl.when(k == nk - 1)
            def _():
                o_ref[...] = acc_ref[...].astype(out_dtype)

    a_spec = pl.BlockSpec((tk, tm), lambda i, j, k: (k, i)) if ta else pl.BlockSpec((tm, tk), lambda i, j, k: (i, k))
    b_spec = pl.BlockSpec((tn, tk), lambda i, j, k: (j, k)) if tb else pl.BlockSpec((tk, tn), lambda i, j, k: (k, j))
    return pl.pallas_call(
        body,
        grid=(m // tm, n // tn, nk),
        in_specs=[a_spec, b_spec],
        out_specs=pl.BlockSpec((tm, tn), lambda i, j, k: (i, j)),
        out_shape=jax.ShapeDtypeStruct((m, n), out_dtype),
        scratch_shapes=[pltpu.VMEM((tm, tn), F32)],
        compiler_params=_params(("parallel", "parallel", "arbitrary")),
        name=name,
    )(a, b)


def _rms(xv, g):
    inv = lax.rsqrt(jnp.mean(xv * xv, axis=-1, keepdims=True) + NORM_EPS)
    return xv * inv * g


def _rms_fwd(x, g, name):
    s, d = x.shape
    t = _tile(s, 512)

    def body(x_ref, g_ref, o_ref):
        o_ref[...] = _rms(x_ref[...], g_ref[...]).astype(BF16)

    return pl.pallas_call(
        body, grid=(s // t,),
        in_specs=[pl.BlockSpec((t, d), lambda i: (i, 0)), pl.BlockSpec((1, d), lambda i: (0, 0))],
        out_specs=pl.BlockSpec((t, d), lambda i: (i, 0)),
        out_shape=jax.ShapeDtypeStruct((s, d), BF16),
        compiler_params=_params(("parallel",)), name=name,
    )(x, g)


def _resid_norm(x, f, post_g, pre_g, name):
    s, d = x.shape
    t = _tile(s, 512)

    def body(x_ref, f_ref, pg_ref, ng_ref, xo_ref, hn_ref):
        xn = x_ref[...] + _rms(f_ref[...], pg_ref[...])
        xo_ref[...] = xn
        hn_ref[...] = _rms(xn, ng_ref[...]).astype(BF16)

    row = pl.BlockSpec((t, d), lambda i: (i, 0))
    vec = pl.BlockSpec((1, d), lambda i: (0, 0))
    return pl.pallas_call(
        body, grid=(s // t,), in_specs=[row, row, vec, vec], out_specs=[row, row],
        out_shape=[jax.ShapeDtypeStruct((s, d), F32), jax.ShapeDtypeStruct((s, d), BF16)],
        compiler_params=_params(("parallel",)), name=name,
    )(x, f, post_g, pre_g)


def _resid_loss(x, f, post_g, tgt, name):
    s, d = x.shape
    t = _tile(s, 512)

    def body(x_ref, f_ref, pg_ref, t_ref, dy_ref, loss_ref):
        @pl.when(pl.program_id(0) == 0)
        def _():
            loss_ref[...] = jnp.zeros_like(loss_ref)

        err = x_ref[...] + _rms(f_ref[...], pg_ref[...]) - t_ref[...]
        dy_ref[...] = err * (1.0 / d)
        sq = jnp.sum(jnp.sum(err * err, axis=1, keepdims=True), axis=0, keepdims=True)
        loss_ref[...] += sq

    row = pl.BlockSpec((t, d), lambda i: (i, 0))
    vec = pl.BlockSpec((1, d), lambda i: (0, 0))
    return pl.pallas_call(
        body, grid=(s // t,), in_specs=[row, row, vec, row],
        out_specs=[row, pl.BlockSpec((8, LANES), lambda i: (0, 0))],
        out_shape=[jax.ShapeDtypeStruct((s, d), F32), jax.ShapeDtypeStruct((8, LANES), F32)],
        compiler_params=_params(("arbitrary",)), name=name,
    )(x, f, post_g, tgt)


def _rms_bwd(xin, g, dy, add, out_dtype, name):
    s, d = xin.shape
    t = _tile(s, 512)
    has_add = add is not None

    def body(*refs):
        if has_add:
            x_ref, g_ref, dy_ref, add_ref, dx_ref, dg_ref = refs
        else:
            x_ref, g_ref, dy_ref, dx_ref, dg_ref = refs

        @pl.when(pl.program_id(0) == 0)
        def _():
            dg_ref[...] = jnp.zeros_like(dg_ref)

        xv = x_ref[...]
        dyv = dy_ref[...].astype(F32)
        inv = lax.rsqrt(jnp.mean(xv * xv, axis=-1, keepdims=True) + NORM_EPS)
        xh = xv * inv
        gy = dyv * g_ref[...]
        dx = inv * (gy - xh * jnp.mean(xh * gy, axis=-1, keepdims=True))
        if has_add:
            dx = dx + add_ref[...]
        dx_ref[...] = dx.astype(out_dtype)
        dg_ref[...] += jnp.sum(dyv * xh, axis=0, keepdims=True)

    row = pl.BlockSpec((t, d), lambda i: (i, 0))
    vec = pl.BlockSpec((1, d), lambda i: (0, 0))
    args = [xin, g, dy] + ([add] if has_add else [])
    specs = [row, vec, row] + ([row] if has_add else [])
    return pl.pallas_call(
        body, grid=(s // t,), in_specs=specs, out_specs=[row, vec],
        out_shape=[jax.ShapeDtypeStruct((s, d), out_dtype), jax.ShapeDtypeStruct((1, d), F32)],
        compiler_params=_params(("arbitrary",)), name=name,
    )(*args)


_NT = (((1,), (1,)), ((), ()))
_NN = (((1,), (0,)), ((), ()))
_TN = (((0,), (0,)), ((), ()))


def _tri_dot(x, tri):
    hi = x.astype(BF16)
    lo = (x - hi.astype(F32)).astype(BF16)
    out = lax.dot_general(hi, tri, _NN, preferred_element_type=F32)
    return out + lax.dot_general(lo, tri, _NN, preferred_element_type=F32)


def _log_sigmoids(z):
    lb = jnp.minimum(z, 0.0) - jnp.log(1.0 + jnp.exp(-jnp.abs(z)))
    return lb, lb - z


def _att_blk(s):
    return min(ATT_BLK, s)


def _attn_fwd(qkv, name):
    s = qkv.shape[0]
    blk = _att_blk(s)
    nq = s // blk
    npair = N_HEADS // 2
    scale = HEAD_DIM ** -0.5

    def body(q_ref, k_ref, v_ref, o_ref, l_ref):
        qi = pl.program_id(1)
        row = lax.broadcasted_iota(jnp.int32, (blk, blk), 0)
        col = lax.broadcasted_iota(jnp.int32, (blk, blk), 1)
        causal = col < row
        tri = (row > col).astype(BF16)
        lane_head = lax.broadcasted_iota(jnp.int32, (blk, LANES), 1) // HEAD_DIM
        masks = [lane_head == hh for hh in range(2)]
        qs = q_ref[...] * scale
        qh = [jnp.where(m, qs, jnp.zeros_like(qs)) for m in masks]

        def block(kb, carry, diag):
            rsums, acc = carry
            r0 = pl.multiple_of(kb * blk, blk)
            k2 = k_ref[pl.ds(r0, blk), :]
            v2 = v_ref[pl.ds(r0, blk), :]
            new_rsums = []
            for hh in range(2):
                z = lax.dot_general(qh[hh], k2, _NT, preferred_element_type=F32)
                lb, lm = _log_sigmoids(z)
                if diag:
                    lm = jnp.where(causal, lm, 0.0)
                cs = _tri_dot(lm, tri)
                w = jnp.exp(lb + cs + rsums[hh])
                if diag:
                    w = jnp.where(causal, w, 0.0)
                vh = jnp.where(masks[hh], v2, jnp.zeros_like(v2))
                acc = acc + lax.dot_general(w.astype(BF16), vh, _NN, preferred_element_type=F32)
                new_rsums.append(rsums[hh] + (cs[:, 0:1] + lm[:, 0:1]))
            return tuple(new_rsums), acc

        zero = jnp.zeros((blk, 1), F32)
        carry = block(qi, ((zero, zero), jnp.zeros((blk, LANES), F32)), True)
        rsums, acc = lax.fori_loop(0, qi, lambda i, c: block(qi - 1 - i, c, False), carry)
        o_ref[...] = acc.astype(BF16)
        l_ref[...] = jnp.where(masks[0], rsums[0], rsums[1])

    return pl.pallas_call(
        body, grid=(npair, nq),
        in_specs=[pl.BlockSpec((blk, LANES), lambda h, i: (i, h)),
                  pl.BlockSpec((s, LANES), lambda h, i: (0, npair + h)),
                  pl.BlockSpec((s, LANES), lambda h, i: (0, 2 * npair + h))],
        out_specs=[pl.BlockSpec((blk, LANES), lambda h, i: (i, h)), pl.BlockSpec((blk, LANES), lambda h, i: (i, h))],
        out_shape=[jax.ShapeDtypeStruct((s, D_MODEL), BF16), jax.ShapeDtypeStruct((s, D_MODEL), F32)],
        compiler_params=_params(("parallel", "arbitrary")), name=name,
    )(qkv, qkv, qkv)


def _attn_bwd(qkv, do, lsum, name):
    s = qkv.shape[0]
    blk = _att_blk(s)
    nq = s // blk
    npair = N_HEADS // 2
    scale = HEAD_DIM ** -0.5

    def body(q_ref, k_ref, v_ref, do_ref, l_ref, dq_ref, dk_ref, dv_ref):
        qi = pl.program_id(1)

        @pl.when(qi == 0)
        def _():
            dk_ref[...] = jnp.zeros_like(dk_ref)
            dv_ref[...] = jnp.zeros_like(dv_ref)

        row = lax.broadcasted_iota(jnp.int32, (blk, blk), 0)
        col = lax.broadcasted_iota(jnp.int32, (blk, blk), 1)
        causal = col < row
        tri_in = (row <= col).astype(BF16)
        tri_ex = (row < col).astype(BF16)
        lane_head = lax.broadcasted_iota(jnp.int32, (blk, LANES), 1) // HEAD_DIM
        masks = [lane_head == hh for hh in range(2)]
        q2 = q_ref[...]
        qs = q2 * scale
        do2 = do_ref[...]
        l2 = l_ref[...]
        qh = [jnp.where(m, q2, jnp.zeros_like(q2)) for m in masks]
        qsh = [jnp.where(m, qs, jnp.zeros_like(qs)) for m in masks]
        doh = [jnp.where(m, do2, jnp.zeros_like(do2)) for m in masks]
        ltot = [l2[:, hh * HEAD_DIM:hh * HEAD_DIM + 1] for hh in range(2)]

        def block(kb, carry, diag):
            r0 = pl.multiple_of(kb * blk, blk)
            k2 = k_ref[pl.ds(r0, blk), :]
            v2 = v_ref[pl.ds(r0, blk), :]
            dq, state = carry
            new_state = []
            dk_blk = jnp.zeros((blk, LANES), F32)
            dv_blk = jnp.zeros((blk, LANES), F32)
            for hh in range(2):
                pre_lm, pre_da = state[hh]
                z = lax.dot_general(qsh[hh], k2, _NT, preferred_element_type=F32)
                lb, lm = _log_sigmoids(z)
                if diag:
                    lm = jnp.where(causal, lm, 0.0)
                pin = _tri_dot(lm, tri_in)
                w = jnp.exp(lb + (ltot[hh] - pre_lm) - pin)
                if diag:
                    w = jnp.where(causal, w, 0.0)
                da = w * lax.dot_general(doh[hh], v2, _NT, preferred_element_type=F32)
                cex = _tri_dot(da, tri_ex)
                dz = da - (da + (cex + pre_da)) * jnp.exp(lb)
                if diag:
                    dz = jnp.where(causal, dz, 0.0)
                dzb = (dz * scale).astype(BF16)
                kh = jnp.where(masks[hh], k2, jnp.zeros_like(k2))
                dq = dq + lax.dot_general(dzb, kh, _NN, preferred_element_type=F32)
                dk_blk = dk_blk + lax.dot_general(dzb, qh[hh], _TN, preferred_element_type=F32)
                dv_blk = dv_blk + lax.dot_general(w.astype(BF16), doh[hh], _TN, preferred_element_type=F32)
                last = slice(blk - 1, blk)
                new_state.append((pre_lm + pin[:, last], pre_da + (cex[:, last] + da[:, last])))
            dk_ref[pl.ds(r0, blk), :] += dk_blk
            dv_ref[pl.ds(r0, blk), :] += dv_blk
            return dq, tuple(new_state)

        zero = jnp.zeros((blk, 1), F32)
        init = (jnp.zeros((blk, LANES), F32), ((zero, zero), (zero, zero)))
        carry = lax.fori_loop(0, qi, lambda i, c: block(i, c, False), init)
        dq, _ = block(qi, carry, True)
        dq_ref[...] = dq.astype(BF16)

    qblk = pl.BlockSpec((blk, LANES), lambda h, i: (i, h))
    full = pl.BlockSpec((s, LANES), lambda h, i: (0, h))
    return pl.pallas_call(
        body, grid=(npair, nq),
        in_specs=[qblk,
                  pl.BlockSpec((s, LANES), lambda h, i: (0, npair + h)),
                  pl.BlockSpec((s, LANES), lambda h, i: (0, 2 * npair + h)),
                  qblk, qblk],
        out_specs=[qblk, full, full],
        out_shape=[jax.ShapeDtypeStruct((s, D_MODEL), BF16), jax.ShapeDtypeStruct((s, D_MODEL), F32),
                   jax.ShapeDtypeStruct((s, D_MODEL), F32)],
        compiler_params=_params(("parallel", "arbitrary")), name=name,
    )(qkv, qkv, qkv, do, lsum)


def _prev_halo_spec(t, c, col):
    return pl.BlockSpec((HALO, c), lambda *g: (jnp.maximum(g[0] * (t // HALO) - 1, 0), col(*g)))


def _conv_taps(scr, w_ref, b_ref, kw, rows, lead):
    out = b_ref[...] + w_ref[kw - 1:kw, :] * scr[pl.ds(lead, rows), :]
    for k in range(kw - 1):
        out = out + w_ref[k:k + 1, :] * scr[pl.ds(lead - (kw - 1) + k, rows), :]
    return out


def _ffn_act_fwd(pre, cw, cb, name):
    s, c2 = pre.shape
    t = _tile(s, 256)
    cb2 = 2 * FFN_COL
    ncol = c2 // cb2

    def body(x_ref, halo_ref, w_ref, b_ref, o_ref, scr):
        i = pl.program_id(0)
        scr[0:HALO, :] = jnp.where(i > 0, halo_ref[...], 0.0)
        scr[HALO:HALO + t, :] = x_ref[...]
        u = _conv_taps(scr, w_ref, b_ref, FFN_CONV_W, t, HALO)
        o_ref[...] = (_gelu(u[:, :FFN_COL]) * u[:, FFN_COL:]).astype(BF16)

    return pl.pallas_call(
        body, grid=(s // t, ncol),
        in_specs=[pl.BlockSpec((t, cb2), lambda i, j: (i, j)),
                  _prev_halo_spec(t, cb2, lambda i, j: j),
                  pl.BlockSpec((FFN_CONV_W, cb2), lambda i, j: (0, j)),
                  pl.BlockSpec((1, cb2), lambda i, j: (0, j))],
        out_specs=pl.BlockSpec((t, FFN_COL), lambda i, j: (i, j)),
        out_shape=jax.ShapeDtypeStruct((s, c2 // 2), BF16),
        scratch_shapes=[pltpu.VMEM((t + HALO, cb2), F32)],
        compiler_params=_params(("parallel", "parallel")), name=name,
    )(pre, pre, cw, cb)


def _ffn_act_bwd(pre, dact, cw, cb, name):
    s, c2 = pre.shape
    t = _tile(s, 256)
    cb2 = 2 * FFN_COL
    ncol = c2 // cb2
    nt = s // t
    kw = FFN_CONV_W
    ext = t + HALO

    def body(x_ref, prev_ref, next_ref, da_ref, dan_ref, w_ref, b_ref, dx_ref, dw_ref, db_ref, xs, das, dus):
        i = pl.program_id(1)

        @pl.when(i == 0)
        def _():
            dw_ref[...] = jnp.zeros_like(dw_ref)
            db_ref[...] = jnp.zeros_like(db_ref)

        xs[0:HALO, :] = jnp.where(i > 0, prev_ref[...], 0.0)
        xs[HALO:HALO + t, :] = x_ref[...]
        xs[HALO + t:HALO + ext, :] = next_ref[...]
        das[0:t, :] = da_ref[...]
        das[t:ext, :] = jnp.where(i < nt - 1, dan_ref[...], 0.0)
        u = _conv_taps(xs, w_ref, b_ref, kw, ext, HALO)
        gel, dgel = _gelu_and_grad(u[:, :FFN_COL])
        dav = das[...]
        dus[:, :FFN_COL] = dav * u[:, FFN_COL:] * dgel
        dus[:, FFN_COL:] = dav * gel
        dx = w_ref[kw - 1:kw, :] * dus[pl.ds(0, t), :]
        for k in range(kw - 1):
            dx = dx + w_ref[k:k + 1, :] * dus[pl.ds(kw - 1 - k, t), :]
        dx_ref[...] = dx.astype(BF16)
        du = dus[pl.ds(0, t), :]
        db_ref[...] += jnp.sum(du, axis=0, keepdims=True)
        for k in range(kw):
            dw_ref[k:k + 1, :] += jnp.sum(du * xs[pl.ds(HALO - (kw - 1) + k, t), :], axis=0, keepdims=True)

    nxt = lambda c: pl.BlockSpec((HALO, c), lambda j, i: (jnp.minimum((i + 1) * (t // HALO), s // HALO - 1), j))
    return pl.pallas_call(
        body, grid=(ncol, nt),
        in_specs=[pl.BlockSpec((t, cb2), lambda j, i: (i, j)),
                  pl.BlockSpec((HALO, cb2), lambda j, i: (jnp.maximum(i * (t // HALO) - 1, 0), j)),
                  nxt(cb2),
                  pl.BlockSpec((t, FFN_COL), lambda j, i: (i, j)),
                  nxt(FFN_COL),
                  pl.BlockSpec((kw, cb2), lambda j, i: (0, j)),
                  pl.BlockSpec((1, cb2), lambda j, i: (0, j))],
        out_specs=[pl.BlockSpec((t, cb2), lambda j, i: (i, j)),
                   pl.BlockSpec((kw, cb2), lambda j, i: (0, j)),
                   pl.BlockSpec((1, cb2), lambda j, i: (0, j))],
        out_shape=[jax.ShapeDtypeStruct((s, c2), BF16), jax.ShapeDtypeStruct((kw, c2), F32),
                   jax.ShapeDtypeStruct((1, c2), F32)],
        scratch_shapes=[pltpu.VMEM((ext + HALO, cb2), F32), pltpu.VMEM((ext, FFN_COL), F32),
                        pltpu.VMEM((ext, cb2), F32)],
        compiler_params=_params(("parallel", "arbitrary")), name=name,
    )(pre, pre, pre, dact, dact, cw, cb)


def _rg_conv_fwd(proj, cw, cb, name):
    s = proj.shape[0]
    c = D_RNN_PAD
    t = _tile(s, 256)

    def body(x_ref, halo_ref, w_ref, b_ref, o_ref, ob_ref, scr):
        i = pl.program_id(0)
        scr[0:HALO, :] = jnp.where(i > 0, halo_ref[...], 0.0)
        scr[HALO:HALO + t, :] = x_ref[...]
        u = _conv_taps(scr, w_ref, b_ref, RG_CONV_W, t, HALO)
        o_ref[...] = u
        ob_ref[...] = u.astype(BF16)

    row = pl.BlockSpec((t, c), lambda i: (i, 0))
    return pl.pallas_call(
        body, grid=(s // t,),
        in_specs=[pl.BlockSpec((t, c), lambda i: (i, 1)),
                  _prev_halo_spec(t, c, lambda i: 1),
                  pl.BlockSpec((RG_CONV_W, c), lambda i: (0, 0)),
                  pl.BlockSpec((1, c), lambda i: (0, 0))],
        out_specs=[row, row],
        out_shape=[jax.ShapeDtypeStruct((s, c), F32), jax.ShapeDtypeStruct((s, c), BF16)],
        scratch_shapes=[pltpu.VMEM((t + HALO, c), F32)],
        compiler_params=_params(("parallel",)), name=name,
    )(proj, proj, cw, cb)


def _neg_expm1(x):
    y = jnp.exp(x)
    ly = jnp.log(y)
    safe = jnp.where(ly == 0.0, 1.0, ly)
    em = jnp.where(y == 1.0, x, (y - 1.0) * x / safe)
    em = jnp.where(x < -30.0, -1.0, em)
    return -em


def _rg_gates(pre_ax, rec, ba, bx, lam):
    c = D_RNN_PAD
    r = jax.nn.sigmoid(pre_ax[:, :c] + ba)
    ig = jax.nn.sigmoid(pre_ax[:, c:] + bx)
    sp = jnp.maximum(-lam, 0.0) + jnp.log1p(jnp.exp(-jnp.abs(lam)))
    log_a = (-RG_C) * r * sp
    a = jnp.exp(log_a)
    m2 = _neg_expm1(2.0 * log_a)
    mult = jnp.sqrt(m2)
    return r, ig, sp, a, m2, mult


def _rg_scan_fwd(pre_ax, rec, proj, ba, bx, lam, name):
    s = rec.shape[0]
    c = D_RNN_PAD
    t = _tile(s, 256)

    def body(pax_ref, rec_ref, gate_ref, ba_ref, bx_ref, lam_ref, h_ref, gh_ref, a_scr, u_scr, carry):
        @pl.when(pl.program_id(0) == 0)
        def _():
            carry[...] = jnp.zeros_like(carry)

        recv = rec_ref[...]
        _, ig, _, a, _, mult = _rg_gates(pax_ref[...], recv, ba_ref[...], bx_ref[...], lam_ref[...])
        a_scr[...] = a
        u_scr[...] = mult * ig * recv
        rowid = lax.broadcasted_iota(jnp.int32, (8, c), 0)

        def group(gi, h):
            r0 = pl.multiple_of(gi * 8, 8)
            a8 = a_scr[pl.ds(r0, 8), :]
            u8 = u_scr[pl.ds(r0, 8), :]
            out = jnp.zeros((8, c), F32)
            for j in range(8):
                h = jnp.broadcast_to(a8[j:j + 1, :], (8, c)) * h + jnp.broadcast_to(u8[j:j + 1, :], (8, c))
                out = jnp.where(rowid == j, h, out)
            h_ref[pl.ds(r0, 8), :] = out
            return h

        carry[...] = lax.fori_loop(0, t // 8, group, carry[...])
        gh_ref[...] = (_gelu(gate_ref[...]) * h_ref[...]).astype(BF16)

    row = pl.BlockSpec((t, c), lambda i: (i, 0))
    vec = pl.BlockSpec((1, c), lambda i: (0, 0))
    return pl.pallas_call(
        body, grid=(s // t,),
        in_specs=[pl.BlockSpec((t, 2 * c), lambda i: (i, 0)), row, row, vec, vec, vec],
        out_specs=[row, row],
        out_shape=[jax.ShapeDtypeStruct((s, c), F32), jax.ShapeDtypeStruct((s, c), BF16)],
        scratch_shapes=[pltpu.VMEM((t, c), F32), pltpu.VMEM((t, c), F32), pltpu.VMEM((8, c), F32)],
        compiler_params=_params(("arbitrary",)), name=name,
    )(pre_ax, rec, proj, ba, bx, lam)


def _rg_scan_bwd(dgh, proj, h, pre_ax, rec, ba, bx, lam, name):
    s = rec.shape[0]
    c = D_RNN_PAD
    t = _tile(s, 256)
    nt = s // t

    def body(dgh_ref, gate_ref, h_ref, hprev_ref, pax_ref, rec_ref, ba_ref, bx_ref, lam_ref,
             dgate_ref, dpax_ref, drec_ref, dba_ref, dbx_ref, dlam_ref, a_scr, dh_scr, g_scr, hp_scr, carry):
        i = pl.program_id(0)
        ti = nt - 1 - i

        @pl.when(i == 0)
        def _():
            carry[...] = jnp.zeros_like(carry)
            dba_ref[...] = jnp.zeros_like(dba_ref)
            dbx_ref[...] = jnp.zeros_like(dbx_ref)
            dlam_ref[...] = jnp.zeros_like(dlam_ref)

        recv = rec_ref[...]
        lam = lam_ref[...]
        r, ig, sp, a, m2, mult = _rg_gates(pax_ref[...], recv, ba_ref[...], bx_ref[...], lam)
        gel, dgel = _gelu_and_grad(gate_ref[...])
        dghv = dgh_ref[...]
        hv = h_ref[...]
        dgate_ref[...] = (dghv * hv * dgel).astype(BF16)
        a_scr[...] = a
        dh_scr[...] = dghv * gel
        rowid = lax.broadcasted_iota(jnp.int32, (8, c), 0)

        def group(gi, cr):
            r0 = pl.multiple_of((t // 8 - 1 - gi) * 8, 8)
            a8 = a_scr[pl.ds(r0, 8), :]
            d8 = dh_scr[pl.ds(r0, 8), :]
            out = jnp.zeros((8, c), F32)
            for j in range(7, -1, -1):
                g = jnp.broadcast_to(d8[j:j + 1, :], (8, c)) + cr
                out = jnp.where(rowid == j, g, out)
                cr = jnp.broadcast_to(a8[j:j + 1, :], (8, c)) * g
            g_scr[pl.ds(r0, 8), :] = out
            return cr

        carry[...] = lax.fori_loop(0, t // 8, group, carry[...])
        g = g_scr[...]
        hp_scr[0:HALO, :] = jnp.where(ti > 0, hprev_ref[...], 0.0)
        hp_scr[HALO:HALO + t, :] = hv
        da = g * hp_scr[pl.ds(HALO - 1, t), :]
        dmult = g * ig * recv
        dig = g * mult * recv
        drec_ref[...] = g * mult * ig
        dm2 = dmult * 0.5 / mult
        dloga = da * a - 2.0 * (1.0 - m2) * dm2
        dr = dloga * ((-RG_C) * sp)
        dsp = jnp.sum(dloga * ((-RG_C) * r), axis=0, keepdims=True)
        dpa = dr * r * (1.0 - r)
        dpx = dig * ig * (1.0 - ig)
        dpax_ref[:, :c] = dpa.astype(BF16)
        dpax_ref[:, c:] = dpx.astype(BF16)
        dba_ref[...] += jnp.sum(dpa, axis=0, keepdims=True)
        dbx_ref[...] += jnp.sum(dpx, axis=0, keepdims=True)
        dlam_ref[...] += dsp * (-jnp.exp(-(lam + sp)))

    row = pl.BlockSpec((t, c), lambda i: (nt - 1 - i, 0))
    vec = pl.BlockSpec((1, c), lambda i: (0, 0))
    return pl.pallas_call(
        body, grid=(nt,),
        in_specs=[row, row, row,
                  pl.BlockSpec((HALO, c), lambda i: (jnp.maximum((nt - 1 - i) * (t // HALO) - 1, 0), 0)),
                  pl.BlockSpec((t, 2 * c), lambda i: (nt - 1 - i, 0)), row, vec, vec, vec],
        out_specs=[row, pl.BlockSpec((t, 2 * c), lambda i: (nt - 1 - i, 0)), row, vec, vec, vec],
        out_shape=[jax.ShapeDtypeStruct((s, c), BF16), jax.ShapeDtypeStruct((s, 2 * c), BF16),
                   jax.ShapeDtypeStruct((s, c), F32), jax.ShapeDtypeStruct((1, c), F32),
                   jax.ShapeDtypeStruct((1, c), F32), jax.ShapeDtypeStruct((1, c), F32)],
        scratch_shapes=[pltpu.VMEM((t, c), F32), pltpu.VMEM((t, c), F32), pltpu.VMEM((t, c), F32),
                        pltpu.VMEM((t + HALO, c), F32), pltpu.VMEM((8, c), F32)],
        compiler_params=_params(("arbitrary",)), name=name,
    )(dgh, proj, h, h, pre_ax, rec, ba, bx, lam)


def _rg_conv_bwd(drec_a, drec_b, proj, cw, name):
    s = drec_a.shape[0]
    c = D_RNN_PAD
    t = _tile(s, 256)
    nt = s // t
    kw = RG_CONV_W
    ext = t + HALO

    def body(da_ref, dan_ref, db_ref, dbn_ref, x_ref, prev_ref, w_ref, dx_ref, dw_ref, dbias_ref, xs, dus):
        i = pl.program_id(0)

        @pl.when(i == 0)
        def _():
            dw_ref[...] = jnp.zeros_like(dw_ref)
            dbias_ref[...] = jnp.zeros_like(dbias_ref)

        xs[0:HALO, :] = jnp.where(i > 0, prev_ref[...], 0.0)
        xs[HALO:HALO + t, :] = x_ref[...]
        du = da_ref[...] + db_ref[...]
        dus[0:t, :] = du
        dus[t:ext, :] = jnp.where(i < nt - 1, dan_ref[...] + dbn_ref[...], 0.0)
        dx = w_ref[kw - 1:kw, :] * du
        for k in range(kw - 1):
            dx = dx + w_ref[k:k + 1, :] * dus[pl.ds(kw - 1 - k, t), :]
        dx_ref[...] = dx.astype(BF16)
        dbias_ref[...] += jnp.sum(du, axis=0, keepdims=True)
        for k in range(kw):
            dw_ref[k:k + 1, :] += jnp.sum(du * xs[pl.ds(HALO - (kw - 1) + k, t), :], axis=0, keepdims=True)

    row = pl.BlockSpec((t, c), lambda i: (i, 0))
    nxt = pl.BlockSpec((HALO, c), lambda i: (jnp.minimum((i + 1) * (t // HALO), s // HALO - 1), 0))
    return pl.pallas_call(
        body, grid=(nt,),
        in_specs=[row, nxt, row, nxt,
                  pl.BlockSpec((t, c), lambda i: (i, 1)),
                  _prev_halo_spec(t, c, lambda i: 1),
                  pl.BlockSpec((kw, c), lambda i: (0, 0))],
        out_specs=[row, pl.BlockSpec((kw, c), lambda i: (0, 0)), pl.BlockSpec((1, c), lambda i: (0, 0))],
        out_shape=[jax.ShapeDtypeStruct((s, c), BF16), jax.ShapeDtypeStruct((kw, c), F32),
                   jax.ShapeDtypeStruct((1, c), F32)],
        scratch_shapes=[pltpu.VMEM((t + HALO, c), F32), pltpu.VMEM((ext, c), F32)],
        compiler_params=_params(("arbitrary",)), name=name,
    )(drec_a, drec_a, drec_b, drec_b, proj, proj, cw)


def _exchange(operands, name):
    nop = len(operands)
    sames = [same for _, same in operands]

    def body(*refs):
        srcs = refs[:nop]
        outs = refs[nop:2 * nop]
        send_sems, recv_sems, local_sems = refs[2 * nop:]
        x, y, c = lax.axis_index("x"), lax.axis_index("y"), lax.axis_index("c")
        me = 4 * x + 2 * y + c
        local = []
        for o in range(nop):
            src = srcs[o] if sames[o] else srcs[o].at[me]
            cp = pltpu.make_async_copy(src, outs[o].at[me], local_sems.at[o])
            cp.start()
            local.append(cp)
        copies = []
        for k in range(1, N_DEV):
            px = (x + ((k >> 2) & 1)) % 2
            py = (y + ((k >> 1) & 1)) % 2
            pc = (c + (k & 1)) % 2
            peer = 4 * px + 2 * py + pc
            for o in range(nop):
                sem = (k - 1) * nop + o
                send = pltpu.make_async_remote_copy(
                    src_ref=srcs[o] if sames[o] else srcs[o].at[peer], dst_ref=outs[o].at[me],
                    send_sem=send_sems.at[sem], recv_sem=recv_sems.at[sem],
                    device_id=(px, py, pc), device_id_type=pl.DeviceIdType.MESH)
                send.start()
                recv = pltpu.make_async_remote_copy(
                    src_ref=srcs[o] if sames[o] else srcs[o].at[peer], dst_ref=outs[o].at[peer],
                    send_sem=send_sems.at[sem], recv_sem=recv_sems.at[sem],
                    device_id=(px, py, pc), device_id_type=pl.DeviceIdType.MESH)
                copies.append((send, recv))
        for send, recv in copies:
            send.wait_send()
            recv.wait_recv()
        for cp in local:
            cp.wait()

    anyspec = pl.BlockSpec(memory_space=pl.ANY)
    shapes = []
    for arr, same in operands:
        n = arr.shape[0] if same else arr.shape[1]
        shapes.append(jax.ShapeDtypeStruct((N_DEV, n, LANES), arr.dtype))
    return pl.pallas_call(
        body, in_specs=[anyspec] * nop, out_specs=[anyspec] * nop, out_shape=shapes,
        scratch_shapes=[pltpu.SemaphoreType.DMA(((N_DEV - 1) * nop,)), pltpu.SemaphoreType.DMA(((N_DEV - 1) * nop,)),
                        pltpu.SemaphoreType.DMA((nop,))],
        compiler_params=pltpu.CompilerParams(has_side_effects=True), name=name,
    )(*[arr for arr, _ in operands])


def _adam(parts, w, m, v, name):
    n = w.shape[0]
    t = _tile(n, 512)
    c1 = 1.0 - ADAM_B1 ** ADAM_STEP
    c2 = 1.0 - ADAM_B2 ** ADAM_STEP

    def body(p_ref, w_ref, m_ref, v_ref, g_ref, d_ref, mo_ref, vo_ref):
        g = p_ref[0]
        for k in range(1, N_DEV):
            g = g + p_ref[k]
        mn = ADAM_B1 * m_ref[...] + (1.0 - ADAM_B1) * g
        vn = ADAM_B2 * v_ref[...] + (1.0 - ADAM_B2) * (g * g)
        m_hat = mn / c1
        v_hat = vn / c2
        g_ref[...] = g
        d_ref[...] = (-ADAM_LR) * (m_hat / (jnp.sqrt(v_hat) + ADAM_EPS) + ADAM_WD * w_ref[...])
        mo_ref[...] = mn
        vo_ref[...] = vn

    row = pl.BlockSpec((t, LANES), lambda i: (i, 0))
    out = jax.ShapeDtypeStruct((n, LANES), F32)
    return pl.pallas_call(
        body, grid=(n // t,),
        in_specs=[pl.BlockSpec((N_DEV, t, LANES), lambda i: (0, i, 0)), row, row, row],
        out_specs=[row, row, row, row], out_shape=[out, out, out, out],
        compiler_params=_params(("parallel",)), name=name,
    )(parts, w, m, v)


def _pack(pieces, row_mult, lead=False):
    if row_mult is None:
        assert all(math.prod(p.shape) % (N_DEV * 16 * LANES if lead else 16 * LANES) == 0 for p in pieces)
        if lead:
            return jnp.concatenate([p.reshape(N_DEV, -1, LANES) for p in pieces], axis=1)
        return jnp.concatenate([p.reshape(-1, LANES) for p in pieces], axis=0)
    if lead:
        flat = jnp.concatenate([p.reshape(N_DEV, -1) for p in pieces], axis=1)
        n = flat.shape[1]
    else:
        flat = jnp.concatenate([p.reshape(-1) for p in pieces])
        n = flat.shape[0]
    group = row_mult * LANES
    total = -(-n // group) * group
    if lead:
        flat = jnp.pad(flat, ((0, 0), (0, total - n)))
        return flat.reshape(N_DEV, total // LANES, LANES)
    flat = jnp.pad(flat, (0, total - n))
    return flat.reshape(total // LANES, LANES)


def _unpack(buf, shapes, lead=False, rows=False):
    out = []
    off = 0
    if rows:
        for shp in shapes:
            n = math.prod(shp) // LANES
            if lead:
                out.append(buf[:, off:off + n].reshape((N_DEV,) + tuple(shp)))
            else:
                out.append(buf[off:off + n].reshape(shp))
            off += n
        return out
    flat = buf.reshape(N_DEV, -1) if lead else buf.reshape(-1)
    for shp in shapes:
        size = math.prod(shp)
        if lead:
            out.append(flat[:, off:off + size].reshape((N_DEV,) + tuple(shp)))
        else:
            out.append(flat[off:off + size].reshape(shp))
        off += size
    return out


BIG = ["attn_w_qkv", "attn_w_o", "rg_w_in", "rg_w_out", "ffn_w_up", "ffn_w_down"]
SMALL = ["rg_conv_w", "rg_conv_b", "rg_b_a", "rg_b_x", "rg_lambda", "ffn_conv_w"]
REPL = ["rg_w_a", "rg_w_x", "ffn_conv_b", "mix_pre_g", "mix_post_g", "ffn_pre_g", "ffn_post_g"]
ORDER = ["attn_w_qkv", "attn_w_o", "rg_w_in", "rg_conv_w", "rg_conv_b", "rg_w_a", "rg_b_a", "rg_w_x", "rg_b_x",
         "rg_lambda", "rg_w_out", "ffn_w_up", "ffn_conv_w", "ffn_conv_b", "ffn_w_down", "mix_pre_g",
         "mix_post_g", "ffn_pre_g", "ffn_post_g"]
SHARD_AXIS = {"attn_w_qkv": 2, "attn_w_o": 1, "rg_w_in": 2, "rg_w_out": 1, "ffn_w_up": 2, "ffn_w_down": 1,
              "rg_conv_w": 2, "rg_conv_b": 1, "rg_b_a": 1, "rg_b_x": 1, "rg_lambda": 1, "ffn_conv_w": 2}
BIG_ROWS = None
SMALL_ROWS = 64
REPL_ROWS = 128


def _assemble(stacked, axis):
    moved = jnp.moveaxis(stacked, 0, axis)
    shp = list(moved.shape)
    shp[axis:axis + 2] = [shp[axis] * shp[axis + 1]]
    return moved.reshape(shp)


def _split(full, axis):
    shp = list(full.shape)
    shp[axis:axis + 1] = [N_DEV, shp[axis] // N_DEV]
    return jnp.moveaxis(full.reshape(shp), axis, 0)


def _block_diag_pad(w):
    out = jnp.zeros((D_RNN_PAD, D_RNN_PAD), w.dtype)
    for n in range(RG_BLOCKS):
        o = n * RG_BLOCK_W
        out = lax.dynamic_update_slice(out, w[n], (o, o))
    return out


def _block_diag_take(dense):
    return jnp.stack([dense[n * RG_BLOCK_W:(n + 1) * RG_BLOCK_W, n * RG_BLOCK_W:(n + 1) * RG_BLOCK_W]
                      for n in range(RG_BLOCKS)])


def _pad_cols(a, width):
    return jnp.pad(a, ((0, 0), (0, width - a.shape[1])))


def _interleave(a):
    lead = a.shape[:-1]
    nb = D_FF // FFN_COL
    return jnp.swapaxes(a.reshape(lead + (2, nb, FFN_COL)), -3, -2).reshape(lead + (2 * D_FF,))


def _deinterleave(a):
    lead = a.shape[:-1]
    nb = D_FF // FFN_COL
    return jnp.swapaxes(a.reshape(lead + (nb, 2, FFN_COL)), -3, -2).reshape(lead + (2 * D_FF,))


def _ffn_fwd(hn, wp, i):
    pre = _matmul(hn, wp["up"][i], name=f"ffn{i}_up")
    act = _ffn_act_fwd(pre, wp["ffn_cw"][i], wp["ffn_cb"][i], name=f"ffn{i}_act")
    f = _matmul(act, wp["down"][i], name=f"ffn{i}_down")
    return pre, act, f


def _ffn_bwd(df, hn, pre, act, wp, i):
    dact = _matmul(df, wp["down"][i], tb=True, name=f"ffn{i}_dact")
    d_down = _matmul(act, df, ta=True, name=f"ffn{i}_ddown")
    dpre, dcw, dcb = _ffn_act_bwd(pre, dact, wp["ffn_cw"][i], wp["ffn_cb"][i], name=f"ffn{i}_act_bwd")
    d_up = _matmul(hn, dpre, ta=True, name=f"ffn{i}_dup")
    dhn = _matmul(dpre, wp["up"][i], tb=True, name=f"ffn{i}_dhn")
    return dhn, d_up, d_down, dcw, dcb


def _local_step(x, tgt, wp):
    g = wp["gains"]
    gr = {}

    hn0 = _rms_fwd(x, g["mix_pre"][0], name="l0_mix_pre")
    qkv = _matmul(hn0, wp["qkv"], out_dtype=BF16, name="attn_qkv")
    o, lsum = _attn_fwd(qkv, name="attn_fwd")
    f0 = _matmul(o, wp["wo"], name="attn_out")
    x1, hn1 = _resid_norm(x, f0, g["mix_post"][0], g["ffn_pre"][0], name="l0_mix_post")
    pre0, act0, f1 = _ffn_fwd(hn1, wp, 0)
    x2, hn2 = _resid_norm(x1, f1, g["ffn_post"][0], g["mix_pre"][1], name="l0_ffn_post")
    proj = _matmul(hn2, wp["w_in"], name="rg_in")
    rec, rec_b = _rg_conv_fwd(proj, wp["rg_cw"], wp["rg_cb"], name="rg_conv")
    pre_ax = _matmul(rec_b, wp["w_ax"], name="rg_gates")
    h, gh = _rg_scan_fwd(pre_ax, rec, proj, wp["rg_ba"], wp["rg_bx"], wp["rg_lam"], name="rg_scan")
    f2 = _matmul(gh, wp["w_out"], name="rg_out")
    x3, hn3 = _resid_norm(x2, f2, g["mix_post"][1], g["ffn_pre"][1], name="l1_mix_post")
    pre1, act1, f3 = _ffn_fwd(hn3, wp, 1)
    dy, sq = _resid_loss(x3, f3, g["ffn_post"][1], tgt, name="l1_ffn_post_loss")

    ffn_dup, ffn_ddown, ffn_dcw, ffn_dcb = [None, None], [None, None], [None, None], [None, None]
    d_ffn_post, d_ffn_pre, d_mix_post, d_mix_pre = [None, None], [None, None], [None, None], [None, None]

    df3, d_ffn_post[1] = _rms_bwd(f3, g["ffn_post"][1], dy, None, BF16, name="l1_ffn_post_bwd")
    dhn3, ffn_dup[1], ffn_ddown[1], ffn_dcw[1], ffn_dcb[1] = _ffn_bwd(df3, hn3, pre1, act1, wp, 1)
    dx3, d_ffn_pre[1] = _rms_bwd(x3, g["ffn_pre"][1], dhn3, dy, F32, name="l1_ffn_pre_bwd")

    df2, d_mix_post[1] = _rms_bwd(f2, g["mix_post"][1], dx3, None, BF16, name="l1_mix_post_bwd")
    dgh = _matmul(df2, wp["w_out"], tb=True, name="rg_dgh")
    gr["w_out"] = _matmul(gh, df2, ta=True, name="rg_dwout")
    dgate, dpax, drec_u, gr["rg_ba"], gr["rg_bx"], gr["rg_lam"] = _rg_scan_bwd(
        dgh, proj, h, pre_ax, rec, wp["rg_ba"], wp["rg_bx"], wp["rg_lam"], name="rg_scan_bwd")
    drec_g = _matmul(dpax, wp["w_ax"], tb=True, name="rg_drec")
    gr["w_ax"] = _matmul(rec_b, dpax, ta=True, name="rg_dwax")
    dproj_rec, gr["rg_cw"], gr["rg_cb"] = _rg_conv_bwd(drec_u, drec_g, proj, wp["rg_cw"], name="rg_conv_bwd")
    dproj = jnp.concatenate([dgate, dproj_rec], axis=1)
    gr["w_in"] = _matmul(hn2, dproj, ta=True, name="rg_dwin")
    dhn2 = _matmul(dproj, wp["w_in"], tb=True, name="rg_dhn")
    dx2, d_mix_pre[1] = _rms_bwd(x2, g["mix_pre"][1], dhn2, dx3, F32, name="l1_mix_pre_bwd")

    df1, d_ffn_post[0] = _rms_bwd(f1, g["ffn_post"][0], dx2, None, BF16, name="l0_ffn_post_bwd")
    dhn1, ffn_dup[0], ffn_ddown[0], ffn_dcw[0], ffn_dcb[0] = _ffn_bwd(df1, hn1, pre0, act0, wp, 0)
    dx1, d_ffn_pre[0] = _rms_bwd(x1, g["ffn_pre"][0], dhn1, dx2, F32, name="l0_ffn_pre_bwd")

    df0, d_mix_post[0] = _rms_bwd(f0, g["mix_post"][0], dx1, None, BF16, name="l0_mix_post_bwd")
    do = _matmul(df0, wp["wo"], tb=True, out_dtype=BF16, name="attn_do")
    gr["wo"] = _matmul(o, df0, ta=True, name="attn_dwo")
    dq, dk, dv = _attn_bwd(qkv, do, lsum, name="attn_bwd")
    dqkv = jnp.concatenate([dq, dk.astype(BF16), dv.astype(BF16)], axis=1)
    gr["qkv"] = _matmul(hn0, dqkv, ta=True, name="attn_dwqkv")
    dhn0 = _matmul(dqkv, wp["qkv"], tb=True, name="attn_dhn")
    dx0, d_mix_pre[0] = _rms_bwd(x, g["mix_pre"][0], dhn0, dx1, F32, name="l0_mix_pre_bwd")

    gr["up"] = ffn_dup
    gr["down"] = ffn_ddown
    gr["ffn_cw"] = ffn_dcw
    gr["ffn_cb"] = ffn_dcb
    gr["gains"] = {"mix_pre": d_mix_pre, "mix_post": d_mix_post, "ffn_pre": d_ffn_pre, "ffn_post": d_ffn_post}
    return sq[0, 0], dx0, gr


def _prepare(full):
    c, cp = D_RNN, D_RNN_PAD
    w_in = full["rg_w_in"][0]
    w_a = _block_diag_pad(full["rg_w_a"][0].astype(BF16))
    w_x = _block_diag_pad(full["rg_w_x"][0].astype(BF16))
    vec = lambda a: _pad_cols(a.reshape(1, c), cp)
    return {
        "qkv": full["attn_w_qkv"][0],
        "wo": full["attn_w_o"][0],
        "w_in": jnp.concatenate([_pad_cols(w_in[:, :c], cp), _pad_cols(w_in[:, c:], cp)], axis=1),
        "w_ax": jnp.concatenate([w_a, w_x], axis=1),
        "w_out": jnp.pad(full["rg_w_out"][0], ((0, cp - c), (0, 0))),
        "rg_cw": _pad_cols(full["rg_conv_w"][0], cp),
        "rg_cb": vec(full["rg_conv_b"][0]),
        "rg_ba": vec(full["rg_b_a"][0]),
        "rg_bx": vec(full["rg_b_x"][0]),
        "rg_lam": vec(full["rg_lambda"][0]),
        "up": [_interleave(full["ffn_w_up"][i]) for i in range(2)],
        "down": [full["ffn_w_down"][i] for i in range(2)],
        "ffn_cw": [_interleave(full["ffn_conv_w"][i]) for i in range(2)],
        "ffn_cb": [_interleave(full["ffn_conv_b"][i].reshape(1, -1)) for i in range(2)],
        "gains": {k: [full[k + "_g"][i].reshape(1, D_MODEL) for i in range(2)]
                  for k in ("mix_pre", "mix_post", "ffn_pre", "ffn_post")},
    }


def _natural_grads(gr):
    c, cp = D_RNN, D_RNN_PAD
    gg = gr["gains"]
    stack2 = lambda pair: jnp.stack([pair[0].reshape(-1), pair[1].reshape(-1)])
    return {
        "attn_w_qkv": gr["qkv"][None],
        "attn_w_o": gr["wo"][None],
        "rg_w_in": jnp.concatenate([gr["w_in"][:, :c], gr["w_in"][:, cp:cp + c]], axis=1)[None],
        "rg_conv_w": gr["rg_cw"][:, :c][None],
        "rg_conv_b": gr["rg_cb"][:, :c],
        "rg_w_a": _block_diag_take(gr["w_ax"][:, :cp])[None],
        "rg_b_a": gr["rg_ba"][:, :c],
        "rg_w_x": _block_diag_take(gr["w_ax"][:, cp:])[None],
        "rg_b_x": gr["rg_bx"][:, :c],
        "rg_lambda": gr["rg_lam"][:, :c],
        "rg_w_out": gr["w_out"][:c][None],
        "ffn_w_up": jnp.stack([_deinterleave(gr["up"][i]) for i in range(2)]),
        "ffn_conv_w": jnp.stack([_deinterleave(gr["ffn_cw"][i]) for i in range(2)]),
        "ffn_conv_b": jnp.stack([_deinterleave(gr["ffn_cb"][i]).reshape(-1) for i in range(2)]),
        "ffn_w_down": jnp.stack(gr["down"]),
        "mix_pre_g": stack2(gg["mix_pre"]),
        "mix_post_g": stack2(gg["mix_post"]),
        "ffn_pre_g": stack2(gg["ffn_pre"]),
        "ffn_post_g": stack2(gg["ffn_post"]),
    }


def _step(x, loss_target, w, m, v):
    big_shapes = [w[n].shape for n in BIG]
    small_shapes = [w[n].shape for n in SMALL]
    repl_shapes = [w[n].shape for n in REPL]
    wbig = _pack([w[n].astype(BF16) for n in BIG], BIG_ROWS)
    wsmall = _pack([w[n] for n in SMALL], SMALL_ROWS)
    gbig, gsmall = _exchange([(wbig, True), (wsmall, True)], name="gather_weights")
    full = {}
    for n, st in zip(BIG, _unpack(gbig, big_shapes, lead=True, rows=True)):
        full[n] = _assemble(st, SHARD_AXIS[n])
    for n, st in zip(SMALL, _unpack(gsmall, small_shapes, lead=True)):
        full[n] = _assemble(st, SHARD_AXIS[n])
    for n in REPL:
        full[n] = w[n]

    sq, dx, gr = _local_step(x[0], loss_target[0], _prepare(full))
    grads = _natural_grads(gr)

    pbig = _pack([_split(grads[n], SHARD_AXIS[n]) for n in BIG], BIG_ROWS, lead=True)
    psmall = _pack([_split(grads[n], SHARD_AXIS[n]) for n in SMALL], SMALL_ROWS, lead=True)
    prepl = _pack([grads[n] for n in REPL], REPL_ROWS)
    rbig, rsmall, rrepl = _exchange([(pbig, False), (psmall, False), (prepl, True)], name="exchange_grads")

    out = {}
    for names, parts, rows, shapes, tag in ((BIG, rbig, BIG_ROWS, big_shapes, "big"),
                                            (SMALL, rsmall, SMALL_ROWS, small_shapes, "small"),
                                            (REPL, rrepl, REPL_ROWS, repl_shapes, "repl")):
        packs = [_pack([d[n] for n in names], rows) for d in (w, m, v)]
        res = _adam(parts, *packs, name="adamw_" + tag)
        for kind, buf in zip(("grad", "delta", "new_m", "new_v"), res):
            for n, a in zip(names, _unpack(buf, shapes, rows=rows is None)):
                out[(kind, n)] = a

    loss = lax.psum(sq * (0.5 / D_MODEL), ("x", "y", "c"))
    return loss, dx[None], out


def kernel(x, attn_w_qkv, attn_w_o, rg_w_in, rg_conv_w, rg_conv_b, rg_w_a, rg_b_a, rg_w_x, rg_b_x, rg_lambda, rg_w_out, ffn_w_up, ffn_conv_w, ffn_conv_b, ffn_w_down, mix_pre_g, mix_post_g, ffn_pre_g, ffn_post_g, loss_target, m_attn_w_qkv, m_attn_w_o, m_rg_w_in, m_rg_conv_w, m_rg_conv_b, m_rg_w_a, m_rg_b_a, m_rg_w_x, m_rg_b_x, m_rg_lambda, m_rg_w_out, m_ffn_w_up, m_ffn_conv_w, m_ffn_conv_b, m_ffn_w_down, m_mix_pre_g, m_mix_post_g, m_ffn_pre_g, m_ffn_post_g, v_attn_w_qkv, v_attn_w_o, v_rg_w_in, v_rg_conv_w, v_rg_conv_b, v_rg_w_a, v_rg_b_a, v_rg_w_x, v_rg_b_x, v_rg_lambda, v_rg_w_out, v_ffn_w_up, v_ffn_conv_w, v_ffn_conv_b, v_ffn_w_down, v_mix_pre_g, v_mix_post_g, v_ffn_pre_g, v_ffn_post_g):
    w = dict(zip(ORDER, (attn_w_qkv, attn_w_o, rg_w_in, rg_conv_w, rg_conv_b, rg_w_a, rg_b_a, rg_w_x, rg_b_x,
                         rg_lambda, rg_w_out, ffn_w_up, ffn_conv_w, ffn_conv_b, ffn_w_down, mix_pre_g,
                         mix_post_g, ffn_pre_g, ffn_post_g)))
    m = dict(zip(ORDER, (m_attn_w_qkv, m_attn_w_o, m_rg_w_in, m_rg_conv_w, m_rg_conv_b, m_rg_w_a, m_rg_b_a,
                         m_rg_w_x, m_rg_b_x, m_rg_lambda, m_rg_w_out, m_ffn_w_up, m_ffn_conv_w, m_ffn_conv_b,
                         m_ffn_w_down, m_mix_pre_g, m_mix_post_g, m_ffn_pre_g, m_ffn_post_g)))
    v = dict(zip(ORDER, (v_attn_w_qkv, v_attn_w_o, v_rg_w_in, v_rg_conv_w, v_rg_conv_b, v_rg_w_a, v_rg_b_a,
                         v_rg_w_x, v_rg_b_x, v_rg_lambda, v_rg_w_out, v_ffn_w_up, v_ffn_conv_w, v_ffn_conv_b,
                         v_ffn_w_down, v_mix_pre_g, v_mix_post_g, v_ffn_pre_g, v_ffn_post_g)))
    loss, dx, out = _step(x, loss_target, w, m, v)
    return (loss, dx, *[out[("grad", n)] for n in ORDER], *[out[("delta", n)] for n in ORDER],
            *[out[("new_m", n)] for n in ORDER], *[out[("new_v", n)] for n in ORDER])
```

```python
import functools
import math

import jax
import jax.numpy as jnp
from jax import lax
from jax.experimental import pallas as pl
from jax.experimental.pallas import tpu as pltpu

F32 = jnp.float32
BF16 = jnp.bfloat16

D_MODEL = 1024
N_HEADS = 16
HEAD_DIM = 64
D_RNN = 1344
D_RNN_PAD = 1408
RG_BLOCKS = 16
RG_BLOCK_W = 84
RG_CONV_W = 4
RG_C = 8.0
D_FF = 2816
FFN_CONV_W = 3
FFN_COL = 256
NORM_EPS = 1e-6
N_DEV = 8
LANES = 128
HALO = 8

ADAM_LR = 0.001
ADAM_B1 = 0.9
ADAM_B2 = 0.999
ADAM_EPS = 1e-08
ADAM_WD = 0.01
ADAM_STEP = 10

ATT_BLK = 256
ATT_HEADS = 4
ATT_STOP_LOG = -110.0
VMEM_LIMIT = 56 * 1024 * 1024

_GELU_C = math.sqrt(2.0 / math.pi)


def _params(sem=None):
    return pltpu.CompilerParams(dimension_semantics=sem, vmem_limit_bytes=VMEM_LIMIT)


def _tile(n, cap, mult=8):
    if n <= cap:
        return n
    best = None
    for t in range(mult, cap + 1, mult):
        if n % t == 0:
            best = t
    assert best is not None, (n, cap, mult)
    return best


def _gelu_and_grad(x):
    x2 = x * x
    th = jnp.tanh(_GELU_C * (x + 0.044715 * (x2 * x)))
    cdf = 0.5 * (1.0 + th)
    gel = x * cdf
    dgel = cdf + 0.5 * x * (1.0 - th * th) * (_GELU_C * (1.0 + 3.0 * 0.044715 * x2))
    return gel, dgel


def _gelu(x):
    th = jnp.tanh(_GELU_C * (x + 0.044715 * (x * x * x)))
    return x * (0.5 * (1.0 + th))


def _matmul(a, b, *, ta=False, tb=False, out_dtype=F32, name):
    if ta:
        kdim, m = a.shape
    else:
        m, kdim = a.shape
    if tb:
        n, kb_ = b.shape
    else:
        kb_, n = b.shape
    assert kdim == kb_, (a.shape, b.shape, ta, tb)
    wide = lambda d: 1408 if d % 1408 == 0 else 1024
    tm = _tile(m, wide(m), 128) if ta else _tile(m, 512)
    tn = _tile(n, wide(n), 128)
    tk = _tile(kdim, 512, 128) if ta else _tile(kdim, wide(kdim), 128)
    nk = kdim // tk
    dims = (((0 if ta else 1,), (1 if tb else 0,)), ((), ()))

    def body(a_ref, b_ref, o_ref, acc_ref):
        part = lax.dot_general(a_ref[...], b_ref[...], dims, preferred_element_type=F32)
        if nk == 1:
            o_ref[...] = part.astype(out_dtype)
        else:
            k = pl.program_id(2)

            @pl.when(k == 0)
            def _():
                acc_ref[...] = part

            @pl.when(k > 0)
            def _():
                acc_ref[...] += part

            @pl.when(k == nk - 1)
            def _():
                o_ref[...] = acc_ref[...].astype(out_dtype)

    a_spec = pl.BlockSpec((tk, tm), lambda i, j, k: (k, i)) if ta else pl.BlockSpec((tm, tk), lambda i, j, k: (i, k))
    b_spec = pl.BlockSpec((tn, tk), lambda i, j, k: (j, k)) if tb else pl.BlockSpec((tk, tn), lambda i, j, k: (k, j))
    return pl.pallas_call(
        body,
        grid=(m // tm, n // tn, nk),
        in_specs=[a_spec, b_spec],
        out_specs=pl.BlockSpec((tm, tn), lambda i, j, k: (i, j)),
        out_shape=jax.ShapeDtypeStruct((m, n), out_dtype),
        scratch_shapes=[pltpu.VMEM((tm, tn), F32)],
        compiler_params=_params(("parallel", "parallel", "arbitrary")),
        name=name,
    )(a, b)


def _rms(xv, g):
    inv = lax.rsqrt(jnp.mean(xv * xv, axis=-1, keepdims=True) + NORM_EPS)
    return xv * inv * g


def _rms_fwd(x, g, name):
    s, d = x.shape
    t = _tile(s, 512)

    def body(x_ref, g_ref, o_ref):
        o_ref[...] = _rms(x_ref[...], g_ref[...]).astype(BF16)

    return pl.pallas_call(
        body, grid=(s // t,),
        in_specs=[pl.BlockSpec((t, d), lambda i: (i, 0)), pl.BlockSpec((1, d), lambda i: (0, 0))],
        out_specs=pl.BlockSpec((t, d), lambda i: (i, 0)),
        out_shape=jax.ShapeDtypeStruct((s, d), BF16),
        compiler_params=_params(("parallel",)), name=name,
    )(x, g)


def _resid_norm(x, f, post_g, pre_g, name):
    s, d = x.shape
    t = _tile(s, 512)

    def body(x_ref, f_ref, pg_ref, ng_ref, xo_ref, hn_ref):
        xn = x_ref[...] + _rms(f_ref[...], pg_ref[...])
        xo_ref[...] = xn
        hn_ref[...] = _rms(xn, ng_ref[...]).astype(BF16)

    row = pl.BlockSpec((t, d), lambda i: (i, 0))
    vec = pl.BlockSpec((1, d), lambda i: (0, 0))
    return pl.pallas_call(
        body, grid=(s // t,), in_specs=[row, row, vec, vec], out_specs=[row, row],
        out_shape=[jax.ShapeDtypeStruct((s, d), F32), jax.ShapeDtypeStruct((s, d), BF16)],
        compiler_params=_params(("parallel",)), name=name,
    )(x, f, post_g, pre_g)


def _resid_loss(x, f, post_g, tgt, name):
    s, d = x.shape
    t = _tile(s, 512)

    def body(x_ref, f_ref, pg_ref, t_ref, dy_ref, loss_ref):
        @pl.when(pl.program_id(0) == 0)
        def _():
            loss_ref[...] = jnp.zeros_like(loss_ref)

        err = x_ref[...] + _rms(f_ref[...], pg_ref[...]) - t_ref[...]
        dy_ref[...] = err * (1.0 / d)
        sq = jnp.sum(jnp.sum(err * err, axis=1, keepdims=True), axis=0, keepdims=True)
        loss_ref[...] += sq

    row = pl.BlockSpec((t, d), lambda i: (i, 0))
    vec = pl.BlockSpec((1, d), lambda i: (0, 0))
    return pl.pallas_call(
        body, grid=(s // t,), in_specs=[row, row, vec, row],
        out_specs=[row, pl.BlockSpec((8, LANES), lambda i: (0, 0))],
        out_shape=[jax.ShapeDtypeStruct((s, d), F32), jax.ShapeDtypeStruct((8, LANES), F32)],
        compiler_params=_params(("arbitrary",)), name=name,
    )(x, f, post_g, tgt)


def _rms_bwd(xin, g, dy, add, out_dtype, name):
    s, d = xin.shape
    t = _tile(s, 512)
    has_add = add is not None

    def body(*refs):
        if has_add:
            x_ref, g_ref, dy_ref, add_ref, dx_ref, dg_ref = refs
        else:
            x_ref, g_ref, dy_ref, dx_ref, dg_ref = refs

        @pl.when(pl.program_id(0) == 0)
        def _():
            dg_ref[...] = jnp.zeros_like(dg_ref)

        xv = x_ref[...]
        dyv = dy_ref[...].astype(F32)
        inv = lax.rsqrt(jnp.mean(xv * xv, axis=-1, keepdims=True) + NORM_EPS)
        xh = xv * inv
        gy = dyv * g_ref[...]
        dx = inv * (gy - xh * jnp.mean(xh * gy, axis=-1, keepdims=True))
        if has_add:
            dx = dx + add_ref[...]
        dx_ref[...] = dx.astype(out_dtype)
        dg_ref[...] += jnp.sum(dyv * xh, axis=0, keepdims=True)

    row = pl.BlockSpec((t, d), lambda i: (i, 0))
    vec = pl.BlockSpec((1, d), lambda i: (0, 0))
    args = [xin, g, dy] + ([add] if has_add else [])
    specs = [row, vec, row] + ([row] if has_add else [])
    return pl.pallas_call(
        body, grid=(s // t,), in_specs=specs, out_specs=[row, vec],
        out_shape=[jax.ShapeDtypeStruct((s, d), out_dtype), jax.ShapeDtypeStruct((1, d), F32)],
        compiler_params=_params(("arbitrary",)), name=name,
    )(*args)


_NT = (((1,), (1,)), ((), ()))
_NN = (((1,), (0,)), ((), ()))
_TN = (((0,), (0,)), ((), ()))


def _tri_dot(x, tri):
    hi = x.astype(BF16)
    lo = (x - hi.astype(F32)).astype(BF16)
    out = lax.dot_general(hi, tri, _NN, preferred_element_type=F32)
    return out + lax.dot_general(lo, tri, _NN, preferred_element_type=F32)


def _log_sigmoids(z):
    lb = jnp.minimum(z, 0.0) - jnp.log(1.0 + jnp.exp(-jnp.abs(z)))
    return lb, lb - z


def _att_blk(s):
    return min(ATT_BLK, s)


def _attn_fwd(qkv, name):
    s = qkv.shape[0]
    blk = _att_blk(s)
    nq = s // blk
    nh = ATT_HEADS
    width = nh * HEAD_DIM
    ngrp = N_HEADS // nh
    hs = range(nh)
    scale = HEAD_DIM ** -0.5

    def body(q_ref, k_ref, v_ref, o_ref, l_ref, first_ref):
        qi = pl.program_id(1)
        row = lax.broadcasted_iota(jnp.int32, (blk, blk), 0)
        col = lax.broadcasted_iota(jnp.int32, (blk, blk), 1)
        causal = col < row
        tri = (row > col).astype(BF16)
        lane_head = lax.broadcasted_iota(jnp.int32, (blk, width), 1) // HEAD_DIM
        masks = [lane_head == hh for hh in hs]
        qs = q_ref[...] * scale
        qh = [jnp.where(m, qs, jnp.zeros_like(qs)) for m in masks]

        def block(kb, carry, diag):
            rsums, acc = carry
            r0 = pl.multiple_of(kb * blk, blk)
            k2 = k_ref[pl.ds(r0, blk), :]
            v2 = v_ref[pl.ds(r0, blk), :]
            z = [lax.dot_general(qh[hh], k2, _NT, preferred_element_type=F32) for hh in hs]
            lbm = [_log_sigmoids(z[hh]) for hh in hs]
            lb = [p[0] for p in lbm]
            lm = [jnp.where(causal, p[1], 0.0) if diag else p[1] for p in lbm]
            cs = [_tri_dot(lm[hh], tri) for hh in hs]
            w = [jnp.exp(lb[hh] + cs[hh] + rsums[hh]) for hh in hs]
            if diag:
                w = [jnp.where(causal, w[hh], 0.0) for hh in hs]
            for hh in hs:
                vh = jnp.where(masks[hh], v2, jnp.zeros_like(v2))
                acc = acc + lax.dot_general(w[hh].astype(BF16), vh, _NN, preferred_element_type=F32)
            return tuple(rsums[hh] + (cs[hh][:, 0:1] + lm[hh][:, 0:1]) for hh in hs), acc

        def largest(rsums):
            m = jnp.max(rsums[0])
            for hh in range(1, nh):
                m = jnp.maximum(m, jnp.max(rsums[hh]))
            return m

        def more(c):
            return jnp.logical_and(c[0] < qi, c[1] > ATT_STOP_LOG)

        def step(c):
            rsums, acc = block(qi - 1 - c[0], (c[2], c[3]), False)
            return c[0] + 1, largest(rsums), rsums, acc

        zero = jnp.zeros((blk, 1), F32)
        rsums, acc = block(qi, ((zero,) * nh, jnp.zeros((blk, width), F32)), True)
        done, _, rsums, acc = lax.while_loop(more, step, (jnp.int32(0), largest(rsums), rsums, acc))
        first_ref[pl.program_id(0) * nq + qi] = (qi - done).astype(F32)
        o_ref[...] = acc.astype(BF16)
        lout = rsums[nh - 1]
        for hh in range(nh - 2, -1, -1):
            lout = jnp.where(masks[hh], rsums[hh], lout)
        l_ref[...] = lout

    return pl.pallas_call(
        body, grid=(ngrp, nq),
        in_specs=[pl.BlockSpec((blk, width), lambda h, i: (i, h)),
                  pl.BlockSpec((s, width), lambda h, i: (0, ngrp + h)),
                  pl.BlockSpec((s, width), lambda h, i: (0, 2 * ngrp + h))],
        out_specs=[pl.BlockSpec((blk, width), lambda h, i: (i, h)), pl.BlockSpec((blk, width), lambda h, i: (i, h)),
                   pl.BlockSpec(memory_space=pltpu.SMEM)],
        out_shape=[jax.ShapeDtypeStruct((s, D_MODEL), BF16), jax.ShapeDtypeStruct((s, D_MODEL), F32),
                   jax.ShapeDtypeStruct((ngrp * nq,), F32)],
        compiler_params=_params(("arbitrary", "arbitrary")), name=name,
    )(qkv, qkv, qkv)


def _attn_bwd(qkv, do, lsum, first, name):
    s = qkv.shape[0]
    blk = _att_blk(s)
    nq = s // blk
    nh = ATT_HEADS
    width = nh * HEAD_DIM
    ngrp = N_HEADS // nh
    hs = range(nh)
    scale = HEAD_DIM ** -0.5

    def body(first_ref, q_ref, k_ref, v_ref, do_ref, l_ref, dq_ref, dk_ref, dv_ref):
        qi = pl.program_id(1)
        start = first_ref[pl.program_id(0) * nq + qi].astype(jnp.int32)

        @pl.when(qi == 0)
        def _():
            dk_ref[...] = jnp.zeros_like(dk_ref)
            dv_ref[...] = jnp.zeros_like(dv_ref)

        row = lax.broadcasted_iota(jnp.int32, (blk, blk), 0)
        col = lax.broadcasted_iota(jnp.int32, (blk, blk), 1)
        causal = col < row
        tri_in = (row <= col).astype(BF16)
        tri_ex = (row < col).astype(BF16)
        lane_head = lax.broadcasted_iota(jnp.int32, (blk, width), 1) // HEAD_DIM
        masks = [lane_head == hh for hh in hs]
        q2 = q_ref[...]
        qs = q2 * scale
        do2 = do_ref[...]
        l2 = l_ref[...]
        qh = [jnp.where(m, q2, jnp.zeros_like(q2)) for m in masks]
        qsh = [jnp.where(m, qs, jnp.zeros_like(qs)) for m in masks]
        doh = [jnp.where(m, do2, jnp.zeros_like(do2)) for m in masks]
        ltot = [l2[:, hh * HEAD_DIM:hh * HEAD_DIM + 1] for hh in hs]
        last = slice(blk - 1, blk)

        def block(kb, carry, diag):
            r0 = pl.multiple_of(kb * blk, blk)
            k2 = k_ref[pl.ds(r0, blk), :]
            v2 = v_ref[pl.ds(r0, blk), :]
            dq, pre_lm, pre_da = carry
            mask = (lambda a: jnp.where(causal, a, 0.0)) if diag else (lambda a: a)
            z = [lax.dot_general(qsh[hh], k2, _NT, preferred_element_type=F32) for hh in hs]
            dw = [lax.dot_general(doh[hh], v2, _NT, preferred_element_type=F32) for hh in hs]
            lbm = [_log_sigmoids(z[hh]) for hh in hs]
            lb = [p[0] for p in lbm]
            lm = [mask(p[1]) for p in lbm]
            pin = [_tri_dot(lm[hh], tri_in) for hh in hs]
            w = [mask(jnp.exp(lb[hh] + (ltot[hh] - pre_lm[hh]) - pin[hh])) for hh in hs]
            da = [w[hh] * dw[hh] for hh in hs]
            cex = [_tri_dot(da[hh], tri_ex) for hh in hs]
            dzb = [(mask(da[hh] - (da[hh] + (cex[hh] + pre_da[hh])) * jnp.exp(lb[hh])) * scale).astype(BF16)
                   for hh in hs]
            dk_blk = jnp.zeros((blk, width), F32)
            dv_blk = jnp.zeros((blk, width), F32)
            for hh in hs:
                kh = jnp.where(masks[hh], k2, jnp.zeros_like(k2))
                dq = dq + lax.dot_general(dzb[hh], kh, _NN, preferred_element_type=F32)
                dk_blk = dk_blk + lax.dot_general(dzb[hh], qh[hh], _TN, preferred_element_type=F32)
                dv_blk = dv_blk + lax.dot_general(w[hh].astype(BF16), doh[hh], _TN, preferred_element_type=F32)
            dk_ref[pl.ds(r0, blk), :] += dk_blk
            dv_ref[pl.ds(r0, blk), :] += dv_blk
            return (dq, tuple(pre_lm[hh] + pin[hh][:, last] for hh in hs),
                    tuple(pre_da[hh] + (cex[hh][:, last] + da[hh][:, last]) for hh in hs))

        zero = jnp.zeros((blk, 1), F32)
        init = (jnp.zeros((blk, width), F32), (zero,) * nh, (zero,) * nh)
        carry = lax.fori_loop(start, qi, lambda i, c: block(i, c, False), init)
        dq, _, _ = block(qi, carry, True)
        dq_ref[...] = dq.astype(BF16)

    qblk = pl.BlockSpec((blk, width), lambda h, i: (i, h))
    once = pl.Buffered(1)
    full = pl.BlockSpec((s, width), lambda h, i: (0, h), pipeline_mode=once)
    return pl.pallas_call(
        body, grid=(ngrp, nq),
        in_specs=[pl.BlockSpec(memory_space=pltpu.SMEM), qblk,
                  pl.BlockSpec((s, width), lambda h, i: (0, ngrp + h), pipeline_mode=once),
                  pl.BlockSpec((s, width), lambda h, i: (0, 2 * ngrp + h), pipeline_mode=once),
                  qblk, qblk],
        out_specs=[qblk, full, full],
        out_shape=[jax.ShapeDtypeStruct((s, D_MODEL), BF16), jax.ShapeDtypeStruct((s, D_MODEL), F32),
                   jax.ShapeDtypeStruct((s, D_MODEL), F32)],
        compiler_params=_params(("arbitrary", "arbitrary")), name=name,
    )(first, qkv, qkv, qkv, do, lsum)


def _prev_halo_spec(t, c, col):
    return pl.BlockSpec((HALO, c), lambda *g: (jnp.maximum(g[0] * (t // HALO) - 1, 0), col(*g)))


def _conv_taps(scr, w_ref, b_ref, kw, rows, lead):
    out = b_ref[...] + w_ref[kw - 1:kw, :] * scr[pl.ds(lead, rows), :]
    for k in range(kw - 1):
        out = out + w_ref[k:k + 1, :] * scr[pl.ds(lead - (kw - 1) + k, rows), :]
    return out


def _ffn_act_fwd(pre, cw, cb, name):
    s, c2 = pre.shape
    t = _tile(s, 256)
    cb2 = 2 * FFN_COL
    ncol = c2 // cb2

    def body(x_ref, halo_ref, w_ref, b_ref, o_ref, scr):
        i = pl.program_id(0)
        scr[0:HALO, :] = jnp.where(i > 0, halo_ref[...], 0.0)
        scr[HALO:HALO + t, :] = x_ref[...]
        u = _conv_taps(scr, w_ref, b_ref, FFN_CONV_W, t, HALO)
        o_ref[...] = (_gelu(u[:, :FFN_COL]) * u[:, FFN_COL:]).astype(BF16)

    return pl.pallas_call(
        body, grid=(s // t, ncol),
        in_specs=[pl.BlockSpec((t, cb2), lambda i, j: (i, j)),
                  _prev_halo_spec(t, cb2, lambda i, j: j),
                  pl.BlockSpec((FFN_CONV_W, cb2), lambda i, j: (0, j)),
                  pl.BlockSpec((1, cb2), lambda i, j: (0, j))],
        out_specs=pl.BlockSpec((t, FFN_COL), lambda i, j: (i, j)),
        out_shape=jax.ShapeDtypeStruct((s, c2 // 2), BF16),
        scratch_shapes=[pltpu.VMEM((t + HALO, cb2), F32)],
        compiler_params=_params(("parallel", "parallel")), name=name,
    )(pre, pre, cw, cb)


def _ffn_act_bwd(pre, dact, cw, cb, name):
    s, c2 = pre.shape
    t = _tile(s, 256)
    cb2 = 2 * FFN_COL
    ncol = c2 // cb2
    nt = s // t
    kw = FFN_CONV_W
    ext = t + HALO

    def body(x_ref, prev_ref, next_ref, da_ref, dan_ref, w_ref, b_ref, dx_ref, dw_ref, db_ref, xs, das, dus):
        i = pl.program_id(1)

        @pl.when(i == 0)
        def _():
            dw_ref[...] = jnp.zeros_like(dw_ref)
            db_ref[...] = jnp.zeros_like(db_ref)

        xs[0:HALO, :] = jnp.where(i > 0, prev_ref[...], 0.0)
        xs[HALO:HALO + t, :] = x_ref[...]
        xs[HALO + t:HALO + ext, :] = next_ref[...]
        das[0:t, :] = da_ref[...]
        das[t:ext, :] = jnp.where(i < nt - 1, dan_ref[...], 0.0)
        u = _conv_taps(xs, w_ref, b_ref, kw, ext, HALO)
        gel, dgel = _gelu_and_grad(u[:, :FFN_COL])
        dav = das[...]
        dus[:, :FFN_COL] = dav * u[:, FFN_COL:] * dgel
        dus[:, FFN_COL:] = dav * gel
        dx = w_ref[kw - 1:kw, :] * dus[pl.ds(0, t), :]
        for k in range(kw - 1):
            dx = dx + w_ref[k:k + 1, :] * dus[pl.ds(kw - 1 - k, t), :]
        dx_ref[...] = dx.astype(BF16)
        du = dus[pl.ds(0, t), :]
        db_ref[...] += jnp.sum(du, axis=0, keepdims=True)
        for k in range(kw):
            dw_ref[k:k + 1, :] += jnp.sum(du * xs[pl.ds(HALO - (kw - 1) + k, t), :], axis=0, keepdims=True)

    nxt = lambda c: pl.BlockSpec((HALO, c), lambda j, i: (jnp.minimum((i + 1) * (t // HALO), s // HALO - 1), j))
    return pl.pallas_call(
        body, grid=(ncol, nt),
        in_specs=[pl.BlockSpec((t, cb2), lambda j, i: (i, j)),
                  pl.BlockSpec((HALO, cb2), lambda j, i: (jnp.maximum(i * (t // HALO) - 1, 0), j)),
                  nxt(cb2),
                  pl.BlockSpec((t, FFN_COL), lambda j, i: (i, j)),
                  nxt(FFN_COL),
                  pl.BlockSpec((kw, cb2), lambda j, i: (0, j)),
                  pl.BlockSpec((1, cb2), lambda j, i: (0, j))],
        out_specs=[pl.BlockSpec((t, cb2), lambda j, i: (i, j)),
                   pl.BlockSpec((kw, cb2), lambda j, i: (0, j)),
                   pl.BlockSpec((1, cb2), lambda j, i: (0, j))],
        out_shape=[jax.ShapeDtypeStruct((s, c2), BF16), jax.ShapeDtypeStruct((kw, c2), F32),
                   jax.ShapeDtypeStruct((1, c2), F32)],
        scratch_shapes=[pltpu.VMEM((ext + HALO, cb2), F32), pltpu.VMEM((ext, FFN_COL), F32),
                        pltpu.VMEM((ext, cb2), F32)],
        compiler_params=_params(("parallel", "arbitrary")), name=name,
    )(pre, pre, pre, dact, dact, cw, cb)


def _rg_conv_fwd(proj, cw, cb, name):
    s = proj.shape[0]
    c = D_RNN_PAD
    t = _tile(s, 256)

    def body(x_ref, halo_ref, w_ref, b_ref, o_ref, ob_ref, scr):
        i = pl.program_id(0)
        scr[0:HALO, :] = jnp.where(i > 0, halo_ref[...], 0.0)
        scr[HALO:HALO + t, :] = x_ref[...]
        u = _conv_taps(scr, w_ref, b_ref, RG_CONV_W, t, HALO)
        o_ref[...] = u
        ob_ref[...] = u.astype(BF16)

    row = pl.BlockSpec((t, c), lambda i: (i, 0))
    return pl.pallas_call(
        body, grid=(s // t,),
        in_specs=[pl.BlockSpec((t, c), lambda i: (i, 1)),
                  _prev_halo_spec(t, c, lambda i: 1),
                  pl.BlockSpec((RG_CONV_W, c), lambda i: (0, 0)),
                  pl.BlockSpec((1, c), lambda i: (0, 0))],
        out_specs=[row, row],
        out_shape=[jax.ShapeDtypeStruct((s, c), F32), jax.ShapeDtypeStruct((s, c), BF16)],
        scratch_shapes=[pltpu.VMEM((t + HALO, c), F32)],
        compiler_params=_params(("parallel",)), name=name,
    )(proj, proj, cw, cb)


def _neg_expm1(x):
    y = jnp.exp(x)
    ly = jnp.log(y)
    safe = jnp.where(ly == 0.0, 1.0, ly)
    em = jnp.where(y == 1.0, x, (y - 1.0) * x / safe)
    em = jnp.where(x < -30.0, -1.0, em)
    return -em


def _rg_gates(pre_ax, rec, ba, bx, lam):
    c = D_RNN_PAD
    r = jax.nn.sigmoid(pre_ax[:, :c] + ba)
    ig = jax.nn.sigmoid(pre_ax[:, c:] + bx)
    sp = jnp.maximum(-lam, 0.0) + jnp.log1p(jnp.exp(-jnp.abs(lam)))
    log_a = (-RG_C) * r * sp
    a = jnp.exp(log_a)
    m2 = _neg_expm1(2.0 * log_a)
    mult = jnp.sqrt(m2)
    return r, ig, sp, a, m2, mult


def _rg_scan_fwd(pre_ax, rec, proj, ba, bx, lam, name):
    s = rec.shape[0]
    c = D_RNN_PAD
    t = _tile(s, 256)

    def body(pax_ref, rec_ref, gate_ref, ba_ref, bx_ref, lam_ref, h_ref, gh_ref, a_scr, u_scr, carry):
        @pl.when(pl.program_id(0) == 0)
        def _():
            carry[...] = jnp.zeros_like(carry)

        recv = rec_ref[...]
        _, ig, _, a, _, mult = _rg_gates(pax_ref[...], recv, ba_ref[...], bx_ref[...], lam_ref[...])
        a_scr[...] = a
        u_scr[...] = mult * ig * recv
        rowid = lax.broadcasted_iota(jnp.int32, (8, c), 0)

        def group(gi, h):
            r0 = pl.multiple_of(gi * 8, 8)
            a8 = a_scr[pl.ds(r0, 8), :]
            u8 = u_scr[pl.ds(r0, 8), :]
            out = jnp.zeros((8, c), F32)
            for j in range(8):
                h = jnp.broadcast_to(a8[j:j + 1, :], (8, c)) * h + jnp.broadcast_to(u8[j:j + 1, :], (8, c))
                out = jnp.where(rowid == j, h, out)
            h_ref[pl.ds(r0, 8), :] = out
            return h

        carry[...] = lax.fori_loop(0, t // 8, group, carry[...])
        gh_ref[...] = (_gelu(gate_ref[...]) * h_ref[...]).astype(BF16)

    row = pl.BlockSpec((t, c), lambda i: (i, 0))
    vec = pl.BlockSpec((1, c), lambda i: (0, 0))
    return pl.pallas_call(
        body, grid=(s // t,),
        in_specs=[pl.BlockSpec((t, 2 * c), lambda i: (i, 0)), row, row, vec, vec, vec],
        out_specs=[row, row],
        out_shape=[jax.ShapeDtypeStruct((s, c), F32), jax.ShapeDtypeStruct((s, c), BF16)],
        scratch_shapes=[pltpu.VMEM((t, c), F32), pltpu.VMEM((t, c), F32), pltpu.VMEM((8, c), F32)],
        compiler_params=_params(("arbitrary",)), name=name,
    )(pre_ax, rec, proj, ba, bx, lam)


def _rg_scan_bwd(dgh, proj, h, pre_ax, rec, ba, bx, lam, name):
    s = rec.shape[0]
    c = D_RNN_PAD
    t = _tile(s, 256)
    nt = s // t

    def body(dgh_ref, gate_ref, h_ref, hprev_ref, pax_ref, rec_ref, ba_ref, bx_ref, lam_ref,
             dgate_ref, dpax_ref, drec_ref, dba_ref, dbx_ref, dlam_ref, a_scr, dh_scr, g_scr, hp_scr, carry):
        i = pl.program_id(0)
        ti = nt - 1 - i

        @pl.when(i == 0)
        def _():
            carry[...] = jnp.zeros_like(carry)
            dba_ref[...] = jnp.zeros_like(dba_ref)
            dbx_ref[...] = jnp.zeros_like(dbx_ref)
            dlam_ref[...] = jnp.zeros_like(dlam_ref)

        recv = rec_ref[...]
        lam = lam_ref[...]
        r, ig, sp, a, m2, mult = _rg_gates(pax_ref[...], recv, ba_ref[...], bx_ref[...], lam)
        gel, dgel = _gelu_and_grad(gate_ref[...])
        dghv = dgh_ref[...]
        hv = h_ref[...]
        dgate_ref[...] = (dghv * hv * dgel).astype(BF16)
        a_scr[...] = a
        dh_scr[...] = dghv * gel
        rowid = lax.broadcasted_iota(jnp.int32, (8, c), 0)

        def group(gi, cr):
            r0 = pl.multiple_of((t // 8 - 1 - gi) * 8, 8)
            a8 = a_scr[pl.ds(r0, 8), :]
            d8 = dh_scr[pl.ds(r0, 8), :]
            out = jnp.zeros((8, c), F32)
            for j in range(7, -1, -1):
                g = jnp.broadcast_to(d8[j:j + 1, :], (8, c)) + cr
                out = jnp.where(rowid == j, g, out)
                cr = jnp.broadcast_to(a8[j:j + 1, :], (8, c)) * g
            g_scr[pl.ds(r0, 8), :] = out
            return cr

        carry[...] = lax.fori_loop(0, t // 8, group, carry[...])
        g = g_scr[...]
        hp_scr[0:HALO, :] = jnp.where(ti > 0, hprev_ref[...], 0.0)
        hp_scr[HALO:HALO + t, :] = hv
        da = g * hp_scr[pl.ds(HALO - 1, t), :]
        dmult = g * ig * recv
        dig = g * mult * recv
        drec_ref[...] = g * mult * ig
        dm2 = dmult * 0.5 / mult
        dloga = da * a - 2.0 * (1.0 - m2) * dm2
        dr = dloga * ((-RG_C) * sp)
        dsp = jnp.sum(dloga * ((-RG_C) * r), axis=0, keepdims=True)
        dpa = dr * r * (1.0 - r)
        dpx = dig * ig * (1.0 - ig)
        dpax_ref[:, :c] = dpa.astype(BF16)
        dpax_ref[:, c:] = dpx.astype(BF16)
        dba_ref[...] += jnp.sum(dpa, axis=0, keepdims=True)
        dbx_ref[...] += jnp.sum(dpx, axis=0, keepdims=True)
        dlam_ref[...] += dsp * (-jnp.exp(-(lam + sp)))

    row = pl.BlockSpec((t, c), lambda i: (nt - 1 - i, 0))
    vec = pl.BlockSpec((1, c), lambda i: (0, 0))
    return pl.pallas_call(
        body, grid=(nt,),
        in_specs=[row, row, row,
                  pl.BlockSpec((HALO, c), lambda i: (jnp.maximum((nt - 1 - i) * (t // HALO) - 1, 0), 0)),
                  pl.BlockSpec((t, 2 * c), lambda i: (nt - 1 - i, 0)), row, vec, vec, vec],
        out_specs=[row, pl.BlockSpec((t, 2 * c), lambda i: (nt - 1 - i, 0)), row, vec, vec, vec],
        out_shape=[jax.ShapeDtypeStruct((s, c), BF16), jax.ShapeDtypeStruct((s, 2 * c), BF16),
                   jax.ShapeDtypeStruct((s, c), F32), jax.ShapeDtypeStruct((1, c), F32),
                   jax.ShapeDtypeStruct((1, c), F32), jax.ShapeDtypeStruct((1, c), F32)],
        scratch_shapes=[pltpu.VMEM((t, c), F32), pltpu.VMEM((t, c), F32), pltpu.VMEM((t, c), F32),
                        pltpu.VMEM((t + HALO, c), F32), pltpu.VMEM((8, c), F32)],
        compiler_params=_params(("arbitrary",)), name=name,
    )(dgh, proj, h, h, pre_ax, rec, ba, bx, lam)


def _rg_conv_bwd(drec_a, drec_b, proj, cw, name):
    s = drec_a.shape[0]
    c = D_RNN_PAD
    t = _tile(s, 256)
    nt = s // t
    kw = RG_CONV_W
    ext = t + HALO

    def body(da_ref, dan_ref, db_ref, dbn_ref, x_ref, prev_ref, w_ref, dx_ref, dw_ref, dbias_ref, xs, dus):
        i = pl.program_id(0)

        @pl.when(i == 0)
        def _():
            dw_ref[...] = jnp.zeros_like(dw_ref)
            dbias_ref[...] = jnp.zeros_like(dbias_ref)

        xs[0:HALO, :] = jnp.where(i > 0, prev_ref[...], 0.0)
        xs[HALO:HALO + t, :] = x_ref[...]
        du = da_ref[...] + db_ref[...]
        dus[0:t, :] = du
        dus[t:ext, :] = jnp.where(i < nt - 1, dan_ref[...] + dbn_ref[...], 0.0)
        dx = w_ref[kw - 1:kw, :] * du
        for k in range(kw - 1):
            dx = dx + w_ref[k:k + 1, :] * dus[pl.ds(kw - 1 - k, t), :]
        dx_ref[...] = dx.astype(BF16)
        dbias_ref[...] += jnp.sum(du, axis=0, keepdims=True)
        for k in range(kw):
            dw_ref[k:k + 1, :] += jnp.sum(du * xs[pl.ds(HALO - (kw - 1) + k, t), :], axis=0, keepdims=True)

    row = pl.BlockSpec((t, c), lambda i: (i, 0))
    nxt = pl.BlockSpec((HALO, c), lambda i: (jnp.minimum((i + 1) * (t // HALO), s // HALO - 1), 0))
    return pl.pallas_call(
        body, grid=(nt,),
        in_specs=[row, nxt, row, nxt,
                  pl.BlockSpec((t, c), lambda i: (i, 1)),
                  _prev_halo_spec(t, c, lambda i: 1),
                  pl.BlockSpec((kw, c), lambda i: (0, 0))],
        out_specs=[row, pl.BlockSpec((kw, c), lambda i: (0, 0)), pl.BlockSpec((1, c), lambda i: (0, 0))],
        out_shape=[jax.ShapeDtypeStruct((s, c), BF16), jax.ShapeDtypeStruct((kw, c), F32),
                   jax.ShapeDtypeStruct((1, c), F32)],
        scratch_shapes=[pltpu.VMEM((t + HALO, c), F32), pltpu.VMEM((ext, c), F32)],
        compiler_params=_params(("arbitrary",)), name=name,
    )(drec_a, drec_a, drec_b, drec_b, proj, proj, cw)


def _exchange(operands, name):
    nop = len(operands)
    sames = [same for _, same in operands]

    def body(*refs):
        srcs = refs[:nop]
        outs = refs[nop:2 * nop]
        send_sems, recv_sems, local_sems = refs[2 * nop:]
        x, y, c = lax.axis_index("x"), lax.axis_index("y"), lax.axis_index("c")
        me = 4 * x + 2 * y + c
        local = []
        for o in range(nop):
            src = srcs[o] if sames[o] else srcs[o].at[me]
            cp = pltpu.make_async_copy(src, outs[o].at[me], local_sems.at[o])
            cp.start()
            local.append(cp)
        copies = []
        for k in range(1, N_DEV):
            px = (x + ((k >> 2) & 1)) % 2
            py = (y + ((k >> 1) & 1)) % 2
            pc = (c + (k & 1)) % 2
            peer = 4 * px + 2 * py + pc
            for o in range(nop):
                sem = (k - 1) * nop + o
                send = pltpu.make_async_remote_copy(
                    src_ref=srcs[o] if sames[o] else srcs[o].at[peer], dst_ref=outs[o].at[me],
                    send_sem=send_sems.at[sem], recv_sem=recv_sems.at[sem],
                    device_id=(px, py, pc), device_id_type=pl.DeviceIdType.MESH)
                send.start()
                recv = pltpu.make_async_remote_copy(
                    src_ref=srcs[o] if sames[o] else srcs[o].at[peer], dst_ref=outs[o].at[peer],
                    send_sem=send_sems.at[sem], recv_sem=recv_sems.at[sem],
                    device_id=(px, py, pc), device_id_type=pl.DeviceIdType.MESH)
                copies.append((send, recv))
        for send, recv in copies:
            send.wait_send()
            recv.wait_recv()
        for cp in local:
            cp.wait()

    anyspec = pl.BlockSpec(memory_space=pl.ANY)
    shapes = []
    for arr, same in operands:
        n = arr.shape[0] if same else arr.shape[1]
        shapes.append(jax.ShapeDtypeStruct((N_DEV, n, LANES), arr.dtype))
    return pl.pallas_call(
        body, in_specs=[anyspec] * nop, out_specs=[anyspec] * nop, out_shape=shapes,
        scratch_shapes=[pltpu.SemaphoreType.DMA(((N_DEV - 1) * nop,)), pltpu.SemaphoreType.DMA(((N_DEV - 1) * nop,)),
                        pltpu.SemaphoreType.DMA((nop,))],
        compiler_params=pltpu.CompilerParams(has_side_effects=True), name=name,
    )(*[arr for arr, _ in operands])


def _adam(parts, w, m, v, name):
    n = w.shape[0]
    t = _tile(n, 512)
    c1 = 1.0 - ADAM_B1 ** ADAM_STEP
    c2 = 1.0 - ADAM_B2 ** ADAM_STEP

    def body(p_ref, w_ref, m_ref, v_ref, g_ref, d_ref, mo_ref, vo_ref):
        g = p_ref[0]
        for k in range(1, N_DEV):
            g = g + p_ref[k]
        mn = ADAM_B1 * m_ref[...] + (1.0 - ADAM_B1) * g
        vn = ADAM_B2 * v_ref[...] + (1.0 - ADAM_B2) * (g * g)
        m_hat = mn / c1
        v_hat = vn / c2
        g_ref[...] = g
        d_ref[...] = (-ADAM_LR) * (m_hat / (jnp.sqrt(v_hat) + ADAM_EPS) + ADAM_WD * w_ref[...])
        mo_ref[...] = mn
        vo_ref[...] = vn

    row = pl.BlockSpec((t, LANES), lambda i: (i, 0))
    out = jax.ShapeDtypeStruct((n, LANES), F32)
    return pl.pallas_call(
        body, grid=(n // t,),
        in_specs=[pl.BlockSpec((N_DEV, t, LANES), lambda i: (0, i, 0)), row, row, row],
        out_specs=[row, row, row, row], out_shape=[out, out, out, out],
        compiler_params=_params(("parallel",)), name=name,
    )(parts, w, m, v)


def _pack(pieces, row_mult, lead=False):
    if row_mult is None:
        assert all(math.prod(p.shape) % (N_DEV * 16 * LANES if lead else 16 * LANES) == 0 for p in pieces)
        if lead:
            return jnp.concatenate([p.reshape(N_DEV, -1, LANES) for p in pieces], axis=1)
        return jnp.concatenate([p.reshape(-1, LANES) for p in pieces], axis=0)
    if lead:
        flat = jnp.concatenate([p.reshape(N_DEV, -1) for p in pieces], axis=1)
        n = flat.shape[1]
    else:
        flat = jnp.concatenate([p.reshape(-1) for p in pieces])
        n = flat.shape[0]
    group = row_mult * LANES
    total = -(-n // group) * group
    if lead:
        flat = jnp.pad(flat, ((0, 0), (0, total - n)))
        return flat.reshape(N_DEV, total // LANES, LANES)
    flat = jnp.pad(flat, (0, total - n))
    return flat.reshape(total // LANES, LANES)


def _unpack(buf, shapes, lead=False, rows=False):
    out = []
    off = 0
    if rows:
        for shp in shapes:
            n = math.prod(shp) // LANES
            if lead:
                out.append(buf[:, off:off + n].reshape((N_DEV,) + tuple(shp)))
            else:
                out.append(buf[off:off + n].reshape(shp))
            off += n
        return out
    flat = buf.reshape(N_DEV, -1) if lead else buf.reshape(-1)
    for shp in shapes:
        size = math.prod(shp)
        if lead:
            out.append(flat[:, off:off + size].reshape((N_DEV,) + tuple(shp)))
        else:
            out.append(flat[off:off + size].reshape(shp))
        off += size
    return out


BIG = ["attn_w_qkv", "attn_w_o", "rg_w_in", "rg_w_out", "ffn_w_up", "ffn_w_down"]
SMALL = ["rg_conv_w", "rg_conv_b", "rg_b_a", "rg_b_x", "rg_lambda", "ffn_conv_w"]
REPL = ["rg_w_a", "rg_w_x", "ffn_conv_b", "mix_pre_g", "mix_post_g", "ffn_pre_g", "ffn_post_g"]
ORDER = ["attn_w_qkv", "attn_w_o", "rg_w_in", "rg_conv_w", "rg_conv_b", "rg_w_a", "rg_b_a", "rg_w_x", "rg_b_x",
         "rg_lambda", "rg_w_out", "ffn_w_up", "ffn_conv_w", "ffn_conv_b", "ffn_w_down", "mix_pre_g",
         "mix_post_g", "ffn_pre_g", "ffn_post_g"]
SHARD_AXIS = {"attn_w_qkv": 2, "attn_w_o": 1, "rg_w_in": 2, "rg_w_out": 1, "ffn_w_up": 2, "ffn_w_down": 1,
              "rg_conv_w": 2, "rg_conv_b": 1, "rg_b_a": 1, "rg_b_x": 1, "rg_lambda": 1, "ffn_conv_w": 2}
BIG_ROWS = None
SMALL_ROWS = 64
REPL_ROWS = 128


def _assemble(stacked, axis):
    moved = jnp.moveaxis(stacked, 0, axis)
    shp = list(moved.shape)
    shp[axis:axis + 2] = [shp[axis] * shp[axis + 1]]
    return moved.reshape(shp)


def _split(full, axis):
    shp = list(full.shape)
    shp[axis:axis + 1] = [N_DEV, shp[axis] // N_DEV]
    return jnp.moveaxis(full.reshape(shp), axis, 0)


def _block_diag_pad(w):
    out = jnp.zeros((D_RNN_PAD, D_RNN_PAD), w.dtype)
    for n in range(RG_BLOCKS):
        o = n * RG_BLOCK_W
        out = lax.dynamic_update_slice(out, w[n], (o, o))
    return out


def _block_diag_take(dense):
    return jnp.stack([dense[n * RG_BLOCK_W:(n + 1) * RG_BLOCK_W, n * RG_BLOCK_W:(n + 1) * RG_BLOCK_W]
                      for n in range(RG_BLOCKS)])


def _pad_cols(a, width):
    return jnp.pad(a, ((0, 0), (0, width - a.shape[1])))


def _interleave(a):
    lead = a.shape[:-1]
    nb = D_FF // FFN_COL
    return jnp.swapaxes(a.reshape(lead + (2, nb, FFN_COL)), -3, -2).reshape(lead + (2 * D_FF,))


def _deinterleave(a):
    lead = a.shape[:-1]
    nb = D_FF // FFN_COL
    return jnp.swapaxes(a.reshape(lead + (nb, 2, FFN_COL)), -3, -2).reshape(lead + (2 * D_FF,))


def _ffn_fwd(hn, wp, i):
    pre = _matmul(hn, wp["up"][i], name=f"ffn{i}_up")
    act = _ffn_act_fwd(pre, wp["ffn_cw"][i], wp["ffn_cb"][i], name=f"ffn{i}_act")
    f = _matmul(act, wp["down"][i], name=f"ffn{i}_down")
    return pre, act, f


def _ffn_bwd(df, hn, pre, act, wp, i):
    dact = _matmul(df, wp["down"][i], tb=True, name=f"ffn{i}_dact")
    d_down = _matmul(act, df, ta=True, name=f"ffn{i}_ddown")
    dpre, dcw, dcb = _ffn_act_bwd(pre, dact, wp["ffn_cw"][i], wp["ffn_cb"][i], name=f"ffn{i}_act_bwd")
    d_up = _matmul(hn, dpre, ta=True, name=f"ffn{i}_dup")
    dhn = _matmul(dpre, wp["up"][i], tb=True, name=f"ffn{i}_dhn")
    return dhn, d_up, d_down, dcw, dcb


def _local_step(x, tgt, wp):
    g = wp["gains"]
    gr = {}

    hn0 = _rms_fwd(x, g["mix_pre"][0], name="l0_mix_pre")
    qkv = _matmul(hn0, wp["qkv"], out_dtype=BF16, name="attn_qkv")
    o, lsum, first = _attn_fwd(qkv, name="attn_fwd")
    f0 = _matmul(o, wp["wo"], name="attn_out")
    x1, hn1 = _resid_norm(x, f0, g["mix_post"][0], g["ffn_pre"][0], name="l0_mix_post")
    pre0, act0, f1 = _ffn_fwd(hn1, wp, 0)
    x2, hn2 = _resid_norm(x1, f1, g["ffn_post"][0], g["mix_pre"][1], name="l0_ffn_post")
    proj = _matmul(hn2, wp["w_in"], name="rg_in")
    rec, rec_b = _rg_conv_fwd(proj, wp["rg_cw"], wp["rg_cb"], name="rg_conv")
    pre_ax = _matmul(rec_b, wp["w_ax"], name="rg_gates")
    h, gh = _rg_scan_fwd(pre_ax, rec, proj, wp["rg_ba"], wp["rg_bx"], wp["rg_lam"], name="rg_scan")
    f2 = _matmul(gh, wp["w_out"], name="rg_out")
    x3, hn3 = _resid_norm(x2, f2, g["mix_post"][1], g["ffn_pre"][1], name="l1_mix_post")
    pre1, act1, f3 = _ffn_fwd(hn3, wp, 1)
    dy, sq = _resid_loss(x3, f3, g["ffn_post"][1], tgt, name="l1_ffn_post_loss")

    ffn_dup, ffn_ddown, ffn_dcw, ffn_dcb = [None, None], [None, None], [None, None], [None, None]
    d_ffn_post, d_ffn_pre, d_mix_post, d_mix_pre = [None, None], [None, None], [None, None], [None, None]

    df3, d_ffn_post[1] = _rms_bwd(f3, g["ffn_post"][1], dy, None, BF16, name="l1_ffn_post_bwd")
    dhn3, ffn_dup[1], ffn_ddown[1], ffn_dcw[1], ffn_dcb[1] = _ffn_bwd(df3, hn3, pre1, act1, wp, 1)
    dx3, d_ffn_pre[1] = _rms_bwd(x3, g["ffn_pre"][1], dhn3, dy, F32, name="l1_ffn_pre_bwd")

    df2, d_mix_post[1] = _rms_bwd(f2, g["mix_post"][1], dx3, None, BF16, name="l1_mix_post_bwd")
    dgh = _matmul(df2, wp["w_out"], tb=True, name="rg_dgh")
    gr["w_out"] = _matmul(gh, df2, ta=True, name="rg_dwout")
    dgate, dpax, drec_u, gr["rg_ba"], gr["rg_bx"], gr["rg_lam"] = _rg_scan_bwd(
        dgh, proj, h, pre_ax, rec, wp["rg_ba"], wp["rg_bx"], wp["rg_lam"], name="rg_scan_bwd")
    drec_g = _matmul(dpax, wp["w_ax"], tb=True, name="rg_drec")
    gr["w_ax"] = _matmul(rec_b, dpax, ta=True, name="rg_dwax")
    dproj_rec, gr["rg_cw"], gr["rg_cb"] = _rg_conv_bwd(drec_u, drec_g, proj, wp["rg_cw"], name="rg_conv_bwd")
    dproj = jnp.concatenate([dgate, dproj_rec], axis=1)
    gr["w_in"] = _matmul(hn2, dproj, ta=True, name="rg_dwin")
    dhn2 = _matmul(dproj, wp["w_in"], tb=True, name="rg_dhn")
    dx2, d_mix_pre[1] = _rms_bwd(x2, g["mix_pre"][1], dhn2, dx3, F32, name="l1_mix_pre_bwd")

    df1, d_ffn_post[0] = _rms_bwd(f1, g["ffn_post"][0], dx2, None, BF16, name="l0_ffn_post_bwd")
    dhn1, ffn_dup[0], ffn_ddown[0], ffn_dcw[0], ffn_dcb[0] = _ffn_bwd(df1, hn1, pre0, act0, wp, 0)
    dx1, d_ffn_pre[0] = _rms_bwd(x1, g["ffn_pre"][0], dhn1, dx2, F32, name="l0_ffn_pre_bwd")

    df0, d_mix_post[0] = _rms_bwd(f0, g["mix_post"][0], dx1, None, BF16, name="l0_mix_post_bwd")
    do = _matmul(df0, wp["wo"], tb=True, out_dtype=BF16, name="attn_do")
    gr["wo"] = _matmul(o, df0, ta=True, name="attn_dwo")
    dq, dk, dv = _attn_bwd(qkv, do, lsum, first, name="attn_bwd")
    dqkv = jnp.concatenate([dq, dk.astype(BF16), dv.astype(BF16)], axis=1)
    gr["qkv"] = _matmul(hn0, dqkv, ta=True, name="attn_dwqkv")
    dhn0 = _matmul(dqkv, wp["qkv"], tb=True, name="attn_dhn")
    dx0, d_mix_pre[0] = _rms_bwd(x, g["mix_pre"][0], dhn0, dx1, F32, name="l0_mix_pre_bwd")

    gr["up"] = ffn_dup
    gr["down"] = ffn_ddown
    gr["ffn_cw"] = ffn_dcw
    gr["ffn_cb"] = ffn_dcb
    gr["gains"] = {"mix_pre": d_mix_pre, "mix_post": d_mix_post, "ffn_pre": d_ffn_pre, "ffn_post": d_ffn_post}
    return sq[0, 0], dx0, gr


def _prepare(full):
    c, cp = D_RNN, D_RNN_PAD
    w_in = full["rg_w_in"][0]
    w_a = _block_diag_pad(full["rg_w_a"][0].astype(BF16))
    w_x = _block_diag_pad(full["rg_w_x"][0].astype(BF16))
    vec = lambda a: _pad_cols(a.reshape(1, c), cp)
    return {
        "qkv": full["attn_w_qkv"][0],
        "wo": full["attn_w_o"][0],
        "w_in": jnp.concatenate([_pad_cols(w_in[:, :c], cp), _pad_cols(w_in[:, c:], cp)], axis=1),
        "w_ax": jnp.concatenate([w_a, w_x], axis=1),
        "w_out": jnp.pad(full["rg_w_out"][0], ((0, cp - c), (0, 0))),
        "rg_cw": _pad_cols(full["rg_conv_w"][0], cp),
        "rg_cb": vec(full["rg_conv_b"][0]),
        "rg_ba": vec(full["rg_b_a"][0]),
        "rg_bx": vec(full["rg_b_x"][0]),
        "rg_lam": vec(full["rg_lambda"][0]),
        "up": [_interleave(full["ffn_w_up"][i]) for i in range(2)],
        "down": [full["ffn_w_down"][i] for i in range(2)],
        "ffn_cw": [_interleave(full["ffn_conv_w"][i]) for i in range(2)],
        "ffn_cb": [_interleave(full["ffn_conv_b"][i].reshape(1, -1)) for i in range(2)],
        "gains": {k: [full[k + "_g"][i].reshape(1, D_MODEL) for i in range(2)]
                  for k in ("mix_pre", "mix_post", "ffn_pre", "ffn_post")},
    }


def _natural_grads(gr):
    c, cp = D_RNN, D_RNN_PAD
    gg = gr["gains"]
    stack2 = lambda pair: jnp.stack([pair[0].reshape(-1), pair[1].reshape(-1)])
    return {
        "attn_w_qkv": gr["qkv"][None],
        "attn_w_o": gr["wo"][None],
        "rg_w_in": jnp.concatenate([gr["w_in"][:, :c], gr["w_in"][:, cp:cp + c]], axis=1)[None],
        "rg_conv_w": gr["rg_cw"][:, :c][None],
        "rg_conv_b": gr["rg_cb"][:, :c],
        "rg_w_a": _block_diag_take(gr["w_ax"][:, :cp])[None],
        "rg_b_a": gr["rg_ba"][:, :c],
        "rg_w_x": _block_diag_take(gr["w_ax"][:, cp:])[None],
        "rg_b_x": gr["rg_bx"][:, :c],
        "rg_lambda": gr["rg_lam"][:, :c],
        "rg_w_out": gr["w_out"][:c][None],
        "ffn_w_up": jnp.stack([_deinterleave(gr["up"][i]) for i in range(2)]),
        "ffn_conv_w": jnp.stack([_deinterleave(gr["ffn_cw"][i]) for i in range(2)]),
        "ffn_conv_b": jnp.stack([_deinterleave(gr["ffn_cb"][i]).reshape(-1) for i in range(2)]),
        "ffn_w_down": jnp.stack(gr["down"]),
        "mix_pre_g": stack2(gg["mix_pre"]),
        "mix_post_g": stack2(gg["mix_post"]),
        "ffn_pre_g": stack2(gg["ffn_pre"]),
        "ffn_post_g": stack2(gg["ffn_post"]),
    }


def _step(x, loss_target, w, m, v):
    big_shapes = [w[n].shape for n in BIG]
    small_shapes = [w[n].shape for n in SMALL]
    repl_shapes = [w[n].shape for n in REPL]
    wbig = _pack([w[n].astype(BF16) for n in BIG], BIG_ROWS)
    wsmall = _pack([w[n] for n in SMALL], SMALL_ROWS)
    gbig, gsmall = _exchange([(wbig, True), (wsmall, True)], name="gather_weights")
    full = {}
    for n, st in zip(BIG, _unpack(gbig, big_shapes, lead=True, rows=True)):
        full[n] = _assemble(st, SHARD_AXIS[n])
    for n, st in zip(SMALL, _unpack(gsmall, small_shapes, lead=True)):
        full[n] = _assemble(st, SHARD_AXIS[n])
    for n in REPL:
        full[n] = w[n]

    sq, dx, gr = _local_step(x[0], loss_target[0], _prepare(full))
    grads = _natural_grads(gr)

    pbig = _pack([_split(grads[n], SHARD_AXIS[n]) for n in BIG], BIG_ROWS, lead=True)
    psmall = _pack([_split(grads[n], SHARD_AXIS[n]) for n in SMALL], SMALL_ROWS, lead=True)
    prepl = _pack([grads[n] for n in REPL], REPL_ROWS)
    rbig, rsmall, rrepl = _exchange([(pbig, False), (psmall, False), (prepl, True)], name="exchange_grads")

    out = {}
    for names, parts, rows, shapes, tag in ((BIG, rbig, BIG_ROWS, big_shapes, "big"),
                                            (SMALL, rsmall, SMALL_ROWS, small_shapes, "small"),
                                            (REPL, rrepl, REPL_ROWS, repl_shapes, "repl")):
        packs = [_pack([d[n] for n in names], rows) for d in (w, m, v)]
        res = _adam(parts, *packs, name="adamw_" + tag)
        for kind, buf in zip(("grad", "delta", "new_m", "new_v"), res):
            for n, a in zip(names, _unpack(buf, shapes, rows=rows is None)):
                out[(kind, n)] = a

    loss = lax.psum(sq * (0.5 / D_MODEL), ("x", "y", "c"))
    return loss, dx[None], out


def kernel(x, attn_w_qkv, attn_w_o, rg_w_in, rg_conv_w, rg_conv_b, rg_w_a, rg_b_a, rg_w_x, rg_b_x, rg_lambda, rg_w_out, ffn_w_up, ffn_conv_w, ffn_conv_b, ffn_w_down, mix_pre_g, mix_post_g, ffn_pre_g, ffn_post_g, loss_target, m_attn_w_qkv, m_attn_w_o, m_rg_w_in, m_rg_conv_w, m_rg_conv_b, m_rg_w_a, m_rg_b_a, m_rg_w_x, m_rg_b_x, m_rg_lambda, m_rg_w_out, m_ffn_w_up, m_ffn_conv_w, m_ffn_conv_b, m_ffn_w_down, m_mix_pre_g, m_mix_post_g, m_ffn_pre_g, m_ffn_post_g, v_attn_w_qkv, v_attn_w_o, v_rg_w_in, v_rg_conv_w, v_rg_conv_b, v_rg_w_a, v_rg_b_a, v_rg_w_x, v_rg_b_x, v_rg_lambda, v_rg_w_out, v_ffn_w_up, v_ffn_conv_w, v_ffn_conv_b, v_ffn_w_down, v_mix_pre_g, v_mix_post_g, v_ffn_pre_g, v_ffn_post_g):
    w = dict(zip(ORDER, (attn_w_qkv, attn_w_o, rg_w_in, rg_conv_w, rg_conv_b, rg_w_a, rg_b_a, rg_w_x, rg_b_x,
                         rg_lambda, rg_w_out, ffn_w_up, ffn_conv_w, ffn_conv_b, ffn_w_down, mix_pre_g,
                         mix_post_g, ffn_pre_g, ffn_post_g)))
    m = dict(zip(ORDER, (m_attn_w_qkv, m_attn_w_o, m_rg_w_in, m_rg_conv_w, m_rg_conv_b, m_rg_w_a, m_rg_b_a,
                         m_rg_w_x, m_rg_b_x, m_rg_lambda, m_rg_w_out, m_ffn_w_up, m_ffn_conv_w, m_ffn_conv_b,
                         m_ffn_w_down, m_mix_pre_g, m_mix_post_g, m_ffn_pre_g, m_ffn_post_g)))
    v = dict(zip(ORDER, (v_attn_w_qkv, v_attn_w_o, v_rg_w_in, v_rg_conv_w, v_rg_conv_b, v_rg_w_a, v_rg_b_a,
                         v_rg_w_x, v_rg_b_x, v_rg_lambda, v_rg_w_out, v_ffn_w_up, v_ffn_conv_w, v_ffn_conv_b,
                         v_ffn_w_down, v_mix_pre_g, v_mix_post_g, v_ffn_pre_g, v_ffn_post_g)))
    loss, dx, out = _step(x, loss_target, w, m, v)
    return (loss, dx, *[out[("grad", n)] for n in ORDER], *[out[("delta", n)] for n in ORDER],
            *[out[("new_m", n)] for n in ORDER], *[out[("new_v", n)] for n in ORDER])
```

```python
import functools
import math

import jax
import jax.numpy as jnp
from jax import lax
from jax.experimental import pallas as pl
from jax.experimental.pallas import tpu as pltpu

F32 = jnp.float32
BF16 = jnp.bfloat16

D_MODEL = 1024
N_HEADS = 16
HEAD_DIM = 64
D_RNN = 1344
D_RNN_PAD = 1408
RG_BLOCKS = 16
RG_BLOCK_W = 84
RG_CONV_W = 4
RG_C = 8.0
D_FF = 2816
FFN_CONV_W = 3
FFN_COL = 256
NORM_EPS = 1e-6
N_DEV = 8
LANES = 128
HALO = 8
STRIP = 32

ADAM_LR = 0.001
ADAM_B1 = 0.9
ADAM_B2 = 0.999
ADAM_EPS = 1e-08
ADAM_WD = 0.01
ADAM_STEP = 10

ATT_BLK = 256
ATT_HEADS = 4
ATT_STOP_LOG = -110.0
VMEM_LIMIT = 56 * 1024 * 1024

_GELU_C = math.sqrt(2.0 / math.pi)


def _params(sem=None):
    return pltpu.CompilerParams(dimension_semantics=sem, vmem_limit_bytes=VMEM_LIMIT)


def _tile(n, cap, mult=8):
    if n <= cap:
        return n
    best = None
    for t in range(mult, cap + 1, mult):
        if n % t == 0:
            best = t
    assert best is not None, (n, cap, mult)
    return best


def _gelu_and_grad(x):
    x2 = x * x
    th = jnp.tanh(_GELU_C * (x + 0.044715 * (x2 * x)))
    cdf = 0.5 * (1.0 + th)
    gel = x * cdf
    dgel = cdf + 0.5 * x * (1.0 - th * th) * (_GELU_C * (1.0 + 3.0 * 0.044715 * x2))
    return gel, dgel


def _gelu(x):
    th = jnp.tanh(_GELU_C * (x + 0.044715 * (x * x * x)))
    return x * (0.5 * (1.0 + th))


def _matmul(a, b, *, ta=False, tb=False, out_dtype=F32, name):
    if ta:
        kdim, m = a.shape
    else:
        m, kdim = a.shape
    if tb:
        n, kb_ = b.shape
    else:
        kb_, n = b.shape
    assert kdim == kb_, (a.shape, b.shape, ta, tb)
    wide = lambda d: 1408 if d % 1408 == 0 else 1024
    tm = _tile(m, wide(m), 128) if ta else _tile(m, 512)
    tn = _tile(n, wide(n), 128)
    tk = _tile(kdim, 512, 128) if ta else _tile(kdim, wide(kdim), 128)
    nk = kdim // tk
    dims = (((0 if ta else 1,), (1 if tb else 0,)), ((), ()))

    def body(a_ref, b_ref, o_ref, acc_ref):
        part = lax.dot_general(a_ref[...], b_ref[...], dims, preferred_element_type=F32)
        if nk == 1:
            o_ref[...] = part.astype(out_dtype)
        else:
            k = pl.program_id(2)

            @pl.when(k == 0)
            def _():
                acc_ref[...] = part

            @pl.when(k > 0)
            def _():
                acc_ref[...] += part

            @pl.when(k == nk - 1)
            def _():
                o_ref[...] = acc_ref[...].astype(out_dtype)

    a_spec = pl.BlockSpec((tk, tm), lambda i, j, k: (k, i)) if ta else pl.BlockSpec((tm, tk), lambda i, j, k: (i, k))
    b_spec = pl.BlockSpec((tn, tk), lambda i, j, k: (j, k)) if tb else pl.BlockSpec((tk, tn), lambda i, j, k: (k, j))
    return pl.pallas_call(
        body,
        grid=(m // tm, n // tn, nk),
        in_specs=[a_spec, b_spec],
        out_specs=pl.BlockSpec((tm, tn), lambda i, j, k: (i, j)),
        out_shape=jax.ShapeDtypeStruct((m, n), out_dtype),
        scratch_shapes=[pltpu.VMEM((tm, tn), F32)],
        compiler_params=_params(("parallel", "parallel", "arbitrary")),
        name=name,
    )(a, b)


def _rms(xv, g):
    inv = lax.rsqrt(jnp.mean(xv * xv, axis=-1, keepdims=True) + NORM_EPS)
    return xv * inv * g


def _rms_fwd(x, g, name):
    s, d = x.shape
    t = _tile(s, 512)

    def body(x_ref, g_ref, o_ref):
        o_ref[...] = _rms(x_ref[...], g_ref[...]).astype(BF16)

    return pl.pallas_call(
        body, grid=(s // t,),
        in_specs=[pl.BlockSpec((t, d), lambda i: (i, 0)), pl.BlockSpec((1, d), lambda i: (0, 0))],
        out_specs=pl.BlockSpec((t, d), lambda i: (i, 0)),
        out_shape=jax.ShapeDtypeStruct((s, d), BF16),
        compiler_params=_params(("parallel",)), name=name,
    )(x, g)


def _resid_norm(x, f, post_g, pre_g, name):
    s, d = x.shape
    t = _tile(s, 512)

    def body(x_ref, f_ref, pg_ref, ng_ref, xo_ref, hn_ref):
        xn = x_ref[...] + _rms(f_ref[...], pg_ref[...])
        xo_ref[...] = xn
        hn_ref[...] = _rms(xn, ng_ref[...]).astype(BF16)

    row = pl.BlockSpec((t, d), lambda i: (i, 0))
    vec = pl.BlockSpec((1, d), lambda i: (0, 0))
    return pl.pallas_call(
        body, grid=(s // t,), in_specs=[row, row, vec, vec], out_specs=[row, row],
        out_shape=[jax.ShapeDtypeStruct((s, d), F32), jax.ShapeDtypeStruct((s, d), BF16)],
        compiler_params=_params(("parallel",)), name=name,
    )(x, f, post_g, pre_g)


def _resid_loss(x, f, post_g, tgt, name):
    s, d = x.shape
    t = _tile(s, 512)

    def body(x_ref, f_ref, pg_ref, t_ref, dy_ref, loss_ref):
        @pl.when(pl.program_id(0) == 0)
        def _():
            loss_ref[...] = jnp.zeros_like(loss_ref)

        err = x_ref[...] + _rms(f_ref[...], pg_ref[...]) - t_ref[...]
        dy_ref[...] = err * (1.0 / d)
        sq = jnp.sum(jnp.sum(err * err, axis=1, keepdims=True), axis=0, keepdims=True)
        loss_ref[...] += sq

    row = pl.BlockSpec((t, d), lambda i: (i, 0))
    vec = pl.BlockSpec((1, d), lambda i: (0, 0))
    return pl.pallas_call(
        body, grid=(s // t,), in_specs=[row, row, vec, row],
        out_specs=[row, pl.BlockSpec((8, LANES), lambda i: (0, 0))],
        out_shape=[jax.ShapeDtypeStruct((s, d), F32), jax.ShapeDtypeStruct((8, LANES), F32)],
        compiler_params=_params(("arbitrary",)), name=name,
    )(x, f, post_g, tgt)


def _rms_bwd(xin, g, dy, add, out_dtype, name):
    s, d = xin.shape
    t = _tile(s, 512)
    has_add = add is not None

    def body(*refs):
        if has_add:
            x_ref, g_ref, dy_ref, add_ref, dx_ref, dg_ref = refs
        else:
            x_ref, g_ref, dy_ref, dx_ref, dg_ref = refs

        @pl.when(pl.program_id(0) == 0)
        def _():
            dg_ref[...] = jnp.zeros_like(dg_ref)

        xv = x_ref[...]
        dyv = dy_ref[...].astype(F32)
        inv = lax.rsqrt(jnp.mean(xv * xv, axis=-1, keepdims=True) + NORM_EPS)
        xh = xv * inv
        gy = dyv * g_ref[...]
        dx = inv * (gy - xh * jnp.mean(xh * gy, axis=-1, keepdims=True))
        if has_add:
            dx = dx + add_ref[...]
        dx_ref[...] = dx.astype(out_dtype)
        dg_ref[...] += jnp.sum(dyv * xh, axis=0, keepdims=True)

    row = pl.BlockSpec((t, d), lambda i: (i, 0))
    vec = pl.BlockSpec((1, d), lambda i: (0, 0))
    args = [xin, g, dy] + ([add] if has_add else [])
    specs = [row, vec, row] + ([row] if has_add else [])
    return pl.pallas_call(
        body, grid=(s // t,), in_specs=specs, out_specs=[row, vec],
        out_shape=[jax.ShapeDtypeStruct((s, d), out_dtype), jax.ShapeDtypeStruct((1, d), F32)],
        compiler_params=_params(("arbitrary",)), name=name,
    )(*args)


_NT = (((1,), (1,)), ((), ()))
_NN = (((1,), (0,)), ((), ()))
_TN = (((0,), (0,)), ((), ()))


def _tri_dot(x, tri):
    hi = x.astype(BF16)
    lo = (x - hi.astype(F32)).astype(BF16)
    out = lax.dot_general(hi, tri, _NN, preferred_element_type=F32)
    return out + lax.dot_general(lo, tri, _NN, preferred_element_type=F32)


def _log_sigmoids(z):
    lb = jnp.minimum(z, 0.0) - jnp.log(1.0 + jnp.exp(-jnp.abs(z)))
    return lb, lb - z


def _att_blk(s):
    return min(ATT_BLK, s)


def _attn_fwd(qkv, name):
    s = qkv.shape[0]
    blk = _att_blk(s)
    nq = s // blk
    nh = ATT_HEADS
    width = nh * HEAD_DIM
    ngrp = N_HEADS // nh
    hs = range(nh)
    scale = HEAD_DIM ** -0.5

    def body(q_ref, k_ref, v_ref, o_ref, l_ref, first_ref):
        qi = pl.program_id(1)
        row = lax.broadcasted_iota(jnp.int32, (blk, blk), 0)
        col = lax.broadcasted_iota(jnp.int32, (blk, blk), 1)
        causal = col < row
        tri = (row > col).astype(BF16)
        lane_head = lax.broadcasted_iota(jnp.int32, (blk, width), 1) // HEAD_DIM
        masks = [lane_head == hh for hh in hs]
        qs = q_ref[...] * scale
        qh = [jnp.where(m, qs, jnp.zeros_like(qs)) for m in masks]

        def block(kb, carry, diag):
            rsums, acc = carry
            r0 = pl.multiple_of(kb * blk, blk)
            k2 = k_ref[pl.ds(r0, blk), :]
            v2 = v_ref[pl.ds(r0, blk), :]
            z = [lax.dot_general(qh[hh], k2, _NT, preferred_element_type=F32) for hh in hs]
            lbm = [_log_sigmoids(z[hh]) for hh in hs]
            lb = [p[0] for p in lbm]
            lm = [jnp.where(causal, p[1], 0.0) if diag else p[1] for p in lbm]
            cs = [_tri_dot(lm[hh], tri) for hh in hs]
            w = [jnp.exp(lb[hh] + cs[hh] + rsums[hh]) for hh in hs]
            if diag:
                w = [jnp.where(causal, w[hh], 0.0) for hh in hs]
            for hh in hs:
                vh = jnp.where(masks[hh], v2, jnp.zeros_like(v2))
                acc = acc + lax.dot_general(w[hh].astype(BF16), vh, _NN, preferred_element_type=F32)
            return tuple(rsums[hh] + (cs[hh][:, 0:1] + lm[hh][:, 0:1]) for hh in hs), acc

        def largest(rsums):
            m = jnp.max(rsums[0])
            for hh in range(1, nh):
                m = jnp.maximum(m, jnp.max(rsums[hh]))
            return m

        def more(c):
            return jnp.logical_and(c[0] < qi, c[1] > ATT_STOP_LOG)

        def step(c):
            rsums, acc = block(qi - 1 - c[0], (c[2], c[3]), False)
            return c[0] + 1, largest(rsums), rsums, acc

        zero = jnp.zeros((blk, 1), F32)
        rsums, acc = block(qi, ((zero,) * nh, jnp.zeros((blk, width), F32)), True)
        done, _, rsums, acc = lax.while_loop(more, step, (jnp.int32(0), largest(rsums), rsums, acc))
        first_ref[pl.program_id(0) * nq + qi] = (qi - done).astype(F32)
        o_ref[...] = acc.astype(BF16)
        lout = rsums[nh - 1]
        for hh in range(nh - 2, -1, -1):
            lout = jnp.where(masks[hh], rsums[hh], lout)
        l_ref[...] = lout

    return pl.pallas_call(
        body, grid=(ngrp, nq),
        in_specs=[pl.BlockSpec((blk, width), lambda h, i: (i, h)),
                  pl.BlockSpec((s, width), lambda h, i: (0, ngrp + h)),
                  pl.BlockSpec((s, width), lambda h, i: (0, 2 * ngrp + h))],
        out_specs=[pl.BlockSpec((blk, width), lambda h, i: (i, h)), pl.BlockSpec((blk, width), lambda h, i: (i, h)),
                   pl.BlockSpec(memory_space=pltpu.SMEM)],
        out_shape=[jax.ShapeDtypeStruct((s, D_MODEL), BF16), jax.ShapeDtypeStruct((s, D_MODEL), F32),
                   jax.ShapeDtypeStruct((ngrp * nq,), F32)],
        compiler_params=_params(("arbitrary", "arbitrary")), name=name,
    )(qkv, qkv, qkv)


def _attn_bwd(qkv, do, lsum, first, name):
    s = qkv.shape[0]
    blk = _att_blk(s)
    nq = s // blk
    nh = ATT_HEADS
    width = nh * HEAD_DIM
    ngrp = N_HEADS // nh
    hs = range(nh)
    scale = HEAD_DIM ** -0.5

    def body(first_ref, q_ref, k_ref, v_ref, do_ref, l_ref, dq_ref, dk_ref, dv_ref):
        qi = pl.program_id(1)
        start = first_ref[pl.program_id(0) * nq + qi].astype(jnp.int32)

        @pl.when(qi == 0)
        def _():
            dk_ref[...] = jnp.zeros_like(dk_ref)
            dv_ref[...] = jnp.zeros_like(dv_ref)

        row = lax.broadcasted_iota(jnp.int32, (blk, blk), 0)
        col = lax.broadcasted_iota(jnp.int32, (blk, blk), 1)
        causal = col < row
        tri_in = (row <= col).astype(BF16)
        tri_ex = (row < col).astype(BF16)
        lane_head = lax.broadcasted_iota(jnp.int32, (blk, width), 1) // HEAD_DIM
        masks = [lane_head == hh for hh in hs]
        q2 = q_ref[...]
        qs = q2 * scale
        do2 = do_ref[...]
        l2 = l_ref[...]
        qh = [jnp.where(m, q2, jnp.zeros_like(q2)) for m in masks]
        qsh = [jnp.where(m, qs, jnp.zeros_like(qs)) for m in masks]
        doh = [jnp.where(m, do2, jnp.zeros_like(do2)) for m in masks]
        ltot = [l2[:, hh * HEAD_DIM:hh * HEAD_DIM + 1] for hh in hs]
        last = slice(blk - 1, blk)

        def block(kb, carry, diag):
            r0 = pl.multiple_of(kb * blk, blk)
            k2 = k_ref[pl.ds(r0, blk), :]
            v2 = v_ref[pl.ds(r0, blk), :]
            dq, pre_lm, pre_da = carry
            mask = (lambda a: jnp.where(causal, a, 0.0)) if diag else (lambda a: a)
            z = [lax.dot_general(qsh[hh], k2, _NT, preferred_element_type=F32) for hh in hs]
            dw = [lax.dot_general(doh[hh], v2, _NT, preferred_element_type=F32) for hh in hs]
            lbm = [_log_sigmoids(z[hh]) for hh in hs]
            lb = [p[0] for p in lbm]
            lm = [mask(p[1]) for p in lbm]
            pin = [_tri_dot(lm[hh], tri_in) for hh in hs]
            w = [mask(jnp.exp(lb[hh] + (ltot[hh] - pre_lm[hh]) - pin[hh])) for hh in hs]
            da = [w[hh] * dw[hh] for hh in hs]
            cex = [_tri_dot(da[hh], tri_ex) for hh in hs]
            dzb = [(mask(da[hh] - (da[hh] + (cex[hh] + pre_da[hh])) * jnp.exp(lb[hh])) * scale).astype(BF16)
                   for hh in hs]
            dk_blk = jnp.zeros((blk, width), F32)
            dv_blk = jnp.zeros((blk, width), F32)
            for hh in hs:
                kh = jnp.where(masks[hh], k2, jnp.zeros_like(k2))
                dq = dq + lax.dot_general(dzb[hh], kh, _NN, preferred_element_type=F32)
                dk_blk = dk_blk + lax.dot_general(dzb[hh], qh[hh], _TN, preferred_element_type=F32)
                dv_blk = dv_blk + lax.dot_general(w[hh].astype(BF16), doh[hh], _TN, preferred_element_type=F32)
            dk_ref[pl.ds(r0, blk), :] += dk_blk
            dv_ref[pl.ds(r0, blk), :] += dv_blk
            return (dq, tuple(pre_lm[hh] + pin[hh][:, last] for hh in hs),
                    tuple(pre_da[hh] + (cex[hh][:, last] + da[hh][:, last]) for hh in hs))

        zero = jnp.zeros((blk, 1), F32)
        init = (jnp.zeros((blk, width), F32), (zero,) * nh, (zero,) * nh)
        carry = lax.fori_loop(start, qi, lambda i, c: block(i, c, False), init)
        dq, _, _ = block(qi, carry, True)
        dq_ref[...] = dq.astype(BF16)

    qblk = pl.BlockSpec((blk, width), lambda h, i: (i, h))
    once = pl.Buffered(1)
    full = pl.BlockSpec((s, width), lambda h, i: (0, h), pipeline_mode=once)
    return pl.pallas_call(
        body, grid=(ngrp, nq),
        in_specs=[pl.BlockSpec(memory_space=pltpu.SMEM), qblk,
                  pl.BlockSpec((s, width), lambda h, i: (0, ngrp + h), pipeline_mode=once),
                  pl.BlockSpec((s, width), lambda h, i: (0, 2 * ngrp + h), pipeline_mode=once),
                  qblk, qblk],
        out_specs=[qblk, full, full],
        out_shape=[jax.ShapeDtypeStruct((s, D_MODEL), BF16), jax.ShapeDtypeStruct((s, D_MODEL), F32),
                   jax.ShapeDtypeStruct((s, D_MODEL), F32)],
        compiler_params=_params(("arbitrary", "arbitrary")), name=name,
    )(first, qkv, qkv, qkv, do, lsum)


def _prev_halo_spec(t, c, col):
    return pl.BlockSpec((HALO, c), lambda *g: (jnp.maximum(g[0] * (t // HALO) - 1, 0), col(*g)))


def _conv_taps(scr, w_ref, b_ref, kw, rows, lead):
    out = b_ref[...] + w_ref[kw - 1:kw, :] * scr[pl.ds(lead, rows), :]
    for k in range(kw - 1):
        out = out + w_ref[k:k + 1, :] * scr[pl.ds(lead - (kw - 1) + k, rows), :]
    return out


def _ffn_act_fwd(pre, cw, cb, name):
    s, c2 = pre.shape
    t = _tile(s, 512)
    cb2 = 2 * FFN_COL
    ncol = c2 // cb2
    kw = FFN_CONV_W

    def body(x_ref, halo_ref, w_ref, b_ref, o_ref, scr):
        i = pl.program_id(0)
        scr[0:HALO, :] = jnp.where(i > 0, halo_ref[...], 0.0)
        scr[HALO:HALO + t, :] = x_ref[...]
        w = [w_ref[k:k + 1, :] for k in range(kw)]
        b = b_ref[...]
        for r0 in range(0, t, STRIP):
            rows = min(STRIP, t - r0)
            u = b + w[kw - 1] * scr[HALO + r0:HALO + r0 + rows, :]
            for k in range(kw - 1):
                lo = HALO - (kw - 1) + k + r0
                u = u + w[k] * scr[lo:lo + rows, :]
            o_ref[r0:r0 + rows, :] = (_gelu(u[:, :FFN_COL]) * u[:, FFN_COL:]).astype(BF16)

    return pl.pallas_call(
        body, grid=(s // t, ncol),
        in_specs=[pl.BlockSpec((t, cb2), lambda i, j: (i, j)),
                  _prev_halo_spec(t, cb2, lambda i, j: j),
                  pl.BlockSpec((FFN_CONV_W, cb2), lambda i, j: (0, j)),
                  pl.BlockSpec((1, cb2), lambda i, j: (0, j))],
        out_specs=pl.BlockSpec((t, FFN_COL), lambda i, j: (i, j)),
        out_shape=jax.ShapeDtypeStruct((s, c2 // 2), BF16),
        scratch_shapes=[pltpu.VMEM((t + HALO, cb2), F32)],
        compiler_params=_params(("parallel", "parallel")), name=name,
    )(pre, pre, cw, cb)


def _ffn_act_bwd(pre, dact, cw, cb, name):
    s, c2 = pre.shape
    t = _tile(s, 512)
    cb2 = 2 * FFN_COL
    ncol = c2 // cb2
    nt = s // t
    kw = FFN_CONV_W
    ext = t + HALO

    def body(x_ref, prev_ref, next_ref, da_ref, dan_ref, w_ref, b_ref, dx_ref, dw_ref, db_ref, xs, dus):
        i = pl.program_id(1)

        @pl.when(i == 0)
        def _():
            dw_ref[...] = jnp.zeros_like(dw_ref)
            db_ref[...] = jnp.zeros_like(db_ref)

        xs[0:HALO, :] = jnp.where(i > 0, prev_ref[...], 0.0)
        xs[HALO:HALO + t, :] = x_ref[...]
        xs[HALO + t:HALO + ext, :] = next_ref[...]
        w = [w_ref[k:k + 1, :] for k in range(kw)]
        b = b_ref[...]
        for r0 in range(0, ext, STRIP):
            rows = min(STRIP, ext - r0)
            u = b + w[kw - 1] * xs[HALO + r0:HALO + r0 + rows, :]
            for k in range(kw - 1):
                lo = HALO - (kw - 1) + k + r0
                u = u + w[k] * xs[lo:lo + rows, :]
            gel, dgel = _gelu_and_grad(u[:, :FFN_COL])
            if r0 < t:
                dav = da_ref[r0:r0 + rows, :]
            else:
                dav = jnp.where(i < nt - 1, dan_ref[...], 0.0)
            dus[r0:r0 + rows, :FFN_COL] = dav * u[:, FFN_COL:] * dgel
            dus[r0:r0 + rows, FFN_COL:] = dav * gel
        fold = lambda a: a.reshape(a.shape[0] // 8, 8, cb2).sum(axis=0)
        db = jnp.zeros((8, cb2), F32)
        dw = [jnp.zeros((8, cb2), F32) for _ in range(kw)]
        for r0 in range(0, t, STRIP):
            rows = min(STRIP, t - r0)
            du = dus[r0:r0 + rows, :]
            dx = w[kw - 1] * du
            for k in range(kw - 1):
                lo = r0 + kw - 1 - k
                dx = dx + w[k] * dus[lo:lo + rows, :]
            dx_ref[r0:r0 + rows, :] = dx.astype(BF16)
            db = db + fold(du)
            for k in range(kw):
                lo = HALO - (kw - 1) + k + r0
                dw[k] = dw[k] + fold(du * xs[lo:lo + rows, :])
        db_ref[...] += jnp.sum(db, axis=0, keepdims=True)
        for k in range(kw):
            dw_ref[k:k + 1, :] += jnp.sum(dw[k], axis=0, keepdims=True)

    nxt = lambda c: pl.BlockSpec((HALO, c), lambda j, i: (jnp.minimum((i + 1) * (t // HALO), s // HALO - 1), j))
    return pl.pallas_call(
        body, grid=(ncol, nt),
        in_specs=[pl.BlockSpec((t, cb2), lambda j, i: (i, j)),
                  pl.BlockSpec((HALO, cb2), lambda j, i: (jnp.maximum(i * (t // HALO) - 1, 0), j)),
                  nxt(cb2),
                  pl.BlockSpec((t, FFN_COL), lambda j, i: (i, j)),
                  nxt(FFN_COL),
                  pl.BlockSpec((kw, cb2), lambda j, i: (0, j)),
                  pl.BlockSpec((1, cb2), lambda j, i: (0, j))],
        out_specs=[pl.BlockSpec((t, cb2), lambda j, i: (i, j)),
                   pl.BlockSpec((kw, cb2), lambda j, i: (0, j)),
                   pl.BlockSpec((1, cb2), lambda j, i: (0, j))],
        out_shape=[jax.ShapeDtypeStruct((s, c2), BF16), jax.ShapeDtypeStruct((kw, c2), F32),
                   jax.ShapeDtypeStruct((1, c2), F32)],
        scratch_shapes=[pltpu.VMEM((ext + HALO, cb2), F32), pltpu.VMEM((ext, cb2), F32)],
        compiler_params=_params(("parallel", "arbitrary")), name=name,
    )(pre, pre, pre, dact, dact, cw, cb)


def _rg_conv_fwd(proj, cw, cb, name):
    s = proj.shape[0]
    c = D_RNN_PAD
    t = _tile(s, 256)

    def body(x_ref, halo_ref, w_ref, b_ref, o_ref, ob_ref, scr):
        i = pl.program_id(0)
        scr[0:HALO, :] = jnp.where(i > 0, halo_ref[...], 0.0)
        scr[HALO:HALO + t, :] = x_ref[...]
        u = _conv_taps(scr, w_ref, b_ref, RG_CONV_W, t, HALO)
        o_ref[...] = u
        ob_ref[...] = u.astype(BF16)

    row = pl.BlockSpec((t, c), lambda i: (i, 0))
    return pl.pallas_call(
        body, grid=(s // t,),
        in_specs=[pl.BlockSpec((t, c), lambda i: (i, 1)),
                  _prev_halo_spec(t, c, lambda i: 1),
                  pl.BlockSpec((RG_CONV_W, c), lambda i: (0, 0)),
                  pl.BlockSpec((1, c), lambda i: (0, 0))],
        out_specs=[row, row],
        out_shape=[jax.ShapeDtypeStruct((s, c), F32), jax.ShapeDtypeStruct((s, c), BF16)],
        scratch_shapes=[pltpu.VMEM((t + HALO, c), F32)],
        compiler_params=_params(("parallel",)), name=name,
    )(proj, proj, cw, cb)


def _neg_expm1(x):
    y = jnp.exp(x)
    ly = jnp.log(y)
    safe = jnp.where(ly == 0.0, 1.0, ly)
    em = jnp.where(y == 1.0, x, (y - 1.0) * x / safe)
    em = jnp.where(x < -30.0, -1.0, em)
    return -em


def _rg_gates(pre_ax, rec, ba, bx, lam):
    c = D_RNN_PAD
    r = jax.nn.sigmoid(pre_ax[:, :c] + ba)
    ig = jax.nn.sigmoid(pre_ax[:, c:] + bx)
    sp = jnp.maximum(-lam, 0.0) + jnp.log1p(jnp.exp(-jnp.abs(lam)))
    log_a = (-RG_C) * r * sp
    a = jnp.exp(log_a)
    m2 = _neg_expm1(2.0 * log_a)
    mult = jnp.sqrt(m2)
    return r, ig, sp, a, m2, mult


def _rg_scan_fwd(pre_ax, rec, proj, ba, bx, lam, name):
    s = rec.shape[0]
    c = D_RNN_PAD
    t = _tile(s, 256)

    def body(pax_ref, rec_ref, gate_ref, ba_ref, bx_ref, lam_ref, h_ref, gh_ref, a_scr, u_scr, carry):
        @pl.when(pl.program_id(0) == 0)
        def _():
            carry[...] = jnp.zeros_like(carry)

        recv = rec_ref[...]
        _, ig, _, a, _, mult = _rg_gates(pax_ref[...], recv, ba_ref[...], bx_ref[...], lam_ref[...])
        a_scr[...] = a
        u_scr[...] = mult * ig * recv
        rowid = lax.broadcasted_iota(jnp.int32, (8, c), 0)

        def group(gi, h):
            r0 = pl.multiple_of(gi * 8, 8)
            a8 = a_scr[pl.ds(r0, 8), :]
            u8 = u_scr[pl.ds(r0, 8), :]
            out = jnp.zeros((8, c), F32)
            for j in range(8):
                h = jnp.broadcast_to(a8[j:j + 1, :], (8, c)) * h + jnp.broadcast_to(u8[j:j + 1, :], (8, c))
                out = jnp.where(rowid == j, h, out)
            h_ref[pl.ds(r0, 8), :] = out
            return h

        carry[...] = lax.fori_loop(0, t // 8, group, carry[...])
        gh_ref[...] = (_gelu(gate_ref[...]) * h_ref[...]).astype(BF16)

    row = pl.BlockSpec((t, c), lambda i: (i, 0))
    vec = pl.BlockSpec((1, c), lambda i: (0, 0))
    return pl.pallas_call(
        body, grid=(s // t,),
        in_specs=[pl.BlockSpec((t, 2 * c), lambda i: (i, 0)), row, row, vec, vec, vec],
        out_specs=[row, row],
        out_shape=[jax.ShapeDtypeStruct((s, c), F32), jax.ShapeDtypeStruct((s, c), BF16)],
        scratch_shapes=[pltpu.VMEM((t, c), F32), pltpu.VMEM((t, c), F32), pltpu.VMEM((8, c), F32)],
        compiler_params=_params(("arbitrary",)), name=name,
    )(pre_ax, rec, proj, ba, bx, lam)


def _rg_scan_bwd(dgh, proj, h, pre_ax, rec, ba, bx, lam, name):
    s = rec.shape[0]
    c = D_RNN_PAD
    t = _tile(s, 256)
    nt = s // t

    def body(dgh_ref, gate_ref, h_ref, hprev_ref, pax_ref, rec_ref, ba_ref, bx_ref, lam_ref,
             dgate_ref, dpax_ref, drec_ref, dba_ref, dbx_ref, dlam_ref, a_scr, dh_scr, g_scr, hp_scr, carry):
        i = pl.program_id(0)
        ti = nt - 1 - i

        @pl.when(i == 0)
        def _():
            carry[...] = jnp.zeros_like(carry)
            dba_ref[...] = jnp.zeros_like(dba_ref)
            dbx_ref[...] = jnp.zeros_like(dbx_ref)
            dlam_ref[...] = jnp.zeros_like(dlam_ref)

        recv = rec_ref[...]
        lam = lam_ref[...]
        r, ig, sp, a, m2, mult = _rg_gates(pax_ref[...], recv, ba_ref[...], bx_ref[...], lam)
        gel, dgel = _gelu_and_grad(gate_ref[...])
        dghv = dgh_ref[...]
        hv = h_ref[...]
        dgate_ref[...] = (dghv * hv * dgel).astype(BF16)
        a_scr[...] = a
        dh_scr[...] = dghv * gel
        rowid = lax.broadcasted_iota(jnp.int32, (8, c), 0)

        def group(gi, cr):
            r0 = pl.multiple_of((t // 8 - 1 - gi) * 8, 8)
            a8 = a_scr[pl.ds(r0, 8), :]
            d8 = dh_scr[pl.ds(r0, 8), :]
            out = jnp.zeros((8, c), F32)
            for j in range(7, -1, -1):
                g = jnp.broadcast_to(d8[j:j + 1, :], (8, c)) + cr
                out = jnp.where(rowid == j, g, out)
                cr = jnp.broadcast_to(a8[j:j + 1, :], (8, c)) * g
            g_scr[pl.ds(r0, 8), :] = out
            return cr

        carry[...] = lax.fori_loop(0, t // 8, group, carry[...])
        g = g_scr[...]
        hp_scr[0:HALO, :] = jnp.where(ti > 0, hprev_ref[...], 0.0)
        hp_scr[HALO:HALO + t, :] = hv
        da = g * hp_scr[pl.ds(HALO - 1, t), :]
        dmult = g * ig * recv
        dig = g * mult * recv
        drec_ref[...] = g * mult * ig
        dm2 = dmult * 0.5 / mult
        dloga = da * a - 2.0 * (1.0 - m2) * dm2
        dr = dloga * ((-RG_C) * sp)
        dsp = jnp.sum(dloga * ((-RG_C) * r), axis=0, keepdims=True)
        dpa = dr * r * (1.0 - r)
        dpx = dig * ig * (1.0 - ig)
        dpax_ref[:, :c] = dpa.astype(BF16)
        dpax_ref[:, c:] = dpx.astype(BF16)
        dba_ref[...] += jnp.sum(dpa, axis=0, keepdims=True)
        dbx_ref[...] += jnp.sum(dpx, axis=0, keepdims=True)
        dlam_ref[...] += dsp * (-jnp.exp(-(lam + sp)))

    row = pl.BlockSpec((t, c), lambda i: (nt - 1 - i, 0))
    vec = pl.BlockSpec((1, c), lambda i: (0, 0))
    return pl.pallas_call(
        body, grid=(nt,),
        in_specs=[row, row, row,
                  pl.BlockSpec((HALO, c), lambda i: (jnp.maximum((nt - 1 - i) * (t // HALO) - 1, 0), 0)),
                  pl.BlockSpec((t, 2 * c), lambda i: (nt - 1 - i, 0)), row, vec, vec, vec],
        out_specs=[row, pl.BlockSpec((t, 2 * c), lambda i: (nt - 1 - i, 0)), row, vec, vec, vec],
        out_shape=[jax.ShapeDtypeStruct((s, c), BF16), jax.ShapeDtypeStruct((s, 2 * c), BF16),
                   jax.ShapeDtypeStruct((s, c), F32), jax.ShapeDtypeStruct((1, c), F32),
                   jax.ShapeDtypeStruct((1, c), F32), jax.ShapeDtypeStruct((1, c), F32)],
        scratch_shapes=[pltpu.VMEM((t, c), F32), pltpu.VMEM((t, c), F32), pltpu.VMEM((t, c), F32),
                        pltpu.VMEM((t + HALO, c), F32), pltpu.VMEM((8, c), F32)],
        compiler_params=_params(("arbitrary",)), name=name,
    )(dgh, proj, h, h, pre_ax, rec, ba, bx, lam)


def _rg_conv_bwd(drec_a, drec_b, proj, cw, name):
    s = drec_a.shape[0]
    c = D_RNN_PAD
    t = _tile(s, 256)
    nt = s // t
    kw = RG_CONV_W
    ext = t + HALO

    def body(da_ref, dan_ref, db_ref, dbn_ref, x_ref, prev_ref, w_ref, dx_ref, dw_ref, dbias_ref, xs, dus):
        i = pl.program_id(0)

        @pl.when(i == 0)
        def _():
            dw_ref[...] = jnp.zeros_like(dw_ref)
            dbias_ref[...] = jnp.zeros_like(dbias_ref)

        xs[0:HALO, :] = jnp.where(i > 0, prev_ref[...], 0.0)
        xs[HALO:HALO + t, :] = x_ref[...]
        du = da_ref[...] + db_ref[...]
        dus[0:t, :] = du
        dus[t:ext, :] = jnp.where(i < nt - 1, dan_ref[...] + dbn_ref[...], 0.0)
        dx = w_ref[kw - 1:kw, :] * du
        for k in range(kw - 1):
            dx = dx + w_ref[k:k + 1, :] * dus[pl.ds(kw - 1 - k, t), :]
        dx_ref[...] = dx.astype(BF16)
        dbias_ref[...] += jnp.sum(du, axis=0, keepdims=True)
        for k in range(kw):
            dw_ref[k:k + 1, :] += jnp.sum(du * xs[pl.ds(HALO - (kw - 1) + k, t), :], axis=0, keepdims=True)

    row = pl.BlockSpec((t, c), lambda i: (i, 0))
    nxt = pl.BlockSpec((HALO, c), lambda i: (jnp.minimum((i + 1) * (t // HALO), s // HALO - 1), 0))
    return pl.pallas_call(
        body, grid=(nt,),
        in_specs=[row, nxt, row, nxt,
                  pl.BlockSpec((t, c), lambda i: (i, 1)),
                  _prev_halo_spec(t, c, lambda i: 1),
                  pl.BlockSpec((kw, c), lambda i: (0, 0))],
        out_specs=[row, pl.BlockSpec((kw, c), lambda i: (0, 0)), pl.BlockSpec((1, c), lambda i: (0, 0))],
        out_shape=[jax.ShapeDtypeStruct((s, c), BF16), jax.ShapeDtypeStruct((kw, c), F32),
                   jax.ShapeDtypeStruct((1, c), F32)],
        scratch_shapes=[pltpu.VMEM((t + HALO, c), F32), pltpu.VMEM((ext, c), F32)],
        compiler_params=_params(("arbitrary",)), name=name,
    )(drec_a, drec_a, drec_b, drec_b, proj, proj, cw)


def _exchange(operands, name):
    nop = len(operands)
    sames = [same for _, same in operands]

    def body(*refs):
        srcs = refs[:nop]
        outs = refs[nop:2 * nop]
        send_sems, recv_sems, local_sems = refs[2 * nop:]
        x, y, c = lax.axis_index("x"), lax.axis_index("y"), lax.axis_index("c")
        me = 4 * x + 2 * y + c
        local = []
        for o in range(nop):
            src = srcs[o] if sames[o] else srcs[o].at[me]
            cp = pltpu.make_async_copy(src, outs[o].at[me], local_sems.at[o])
            cp.start()
            local.append(cp)
        copies = []
        for k in range(1, N_DEV):
            px = (x + ((k >> 2) & 1)) % 2
            py = (y + ((k >> 1) & 1)) % 2
            pc = (c + (k & 1)) % 2
            peer = 4 * px + 2 * py + pc
            for o in range(nop):
                sem = (k - 1) * nop + o
                send = pltpu.make_async_remote_copy(
                    src_ref=srcs[o] if sames[o] else srcs[o].at[peer], dst_ref=outs[o].at[me],
                    send_sem=send_sems.at[sem], recv_sem=recv_sems.at[sem],
                    device_id=(px, py, pc), device_id_type=pl.DeviceIdType.MESH)
                send.start()
                recv = pltpu.make_async_remote_copy(
                    src_ref=srcs[o] if sames[o] else srcs[o].at[peer], dst_ref=outs[o].at[peer],
                    send_sem=send_sems.at[sem], recv_sem=recv_sems.at[sem],
                    device_id=(px, py, pc), device_id_type=pl.DeviceIdType.MESH)
                copies.append((send, recv))
        for send, recv in copies:
            send.wait_send()
            recv.wait_recv()
        for cp in local:
            cp.wait()

    anyspec = pl.BlockSpec(memory_space=pl.ANY)
    shapes = []
    for arr, same in operands:
        n = arr.shape[0] if same else arr.shape[1]
        shapes.append(jax.ShapeDtypeStruct((N_DEV, n, LANES), arr.dtype))
    return pl.pallas_call(
        body, in_specs=[anyspec] * nop, out_specs=[anyspec] * nop, out_shape=shapes,
        scratch_shapes=[pltpu.SemaphoreType.DMA(((N_DEV - 1) * nop,)), pltpu.SemaphoreType.DMA(((N_DEV - 1) * nop,)),
                        pltpu.SemaphoreType.DMA((nop,))],
        compiler_params=pltpu.CompilerParams(has_side_effects=True), name=name,
    )(*[arr for arr, _ in operands])


def _adam(parts, w, m, v, name):
    n = w.shape[0]
    t = _tile(n, 512, 16)
    c1 = 1.0 - ADAM_B1 ** ADAM_STEP
    c2 = 1.0 - ADAM_B2 ** ADAM_STEP

    def body(p_ref, w_ref, m_ref, v_ref, g_ref, d_ref, mo_ref, vo_ref):
        g = p_ref[0].astype(F32)
        for k in range(1, N_DEV):
            g = g + p_ref[k].astype(F32)
        mn = ADAM_B1 * m_ref[...] + (1.0 - ADAM_B1) * g
        vn = ADAM_B2 * v_ref[...] + (1.0 - ADAM_B2) * (g * g)
        m_hat = mn / c1
        v_hat = vn / c2
        g_ref[...] = g
        d_ref[...] = (-ADAM_LR) * (m_hat / (jnp.sqrt(v_hat) + ADAM_EPS) + ADAM_WD * w_ref[...])
        mo_ref[...] = mn
        vo_ref[...] = vn

    row = pl.BlockSpec((t, LANES), lambda i: (i, 0))
    out = jax.ShapeDtypeStruct((n, LANES), F32)
    return pl.pallas_call(
        body, grid=(n // t,),
        in_specs=[pl.BlockSpec((N_DEV, t, LANES), lambda i: (0, i, 0)), row, row, row],
        out_specs=[row, row, row, row], out_shape=[out, out, out, out],
        compiler_params=_params(("parallel",)), name=name,
    )(parts, w, m, v)


def _pack(pieces, row_mult, lead=False):
    if row_mult is None:
        assert all(math.prod(p.shape) % (N_DEV * 16 * LANES if lead else 16 * LANES) == 0 for p in pieces)
        if lead:
            return jnp.concatenate([p.reshape(N_DEV, -1, LANES) for p in pieces], axis=1)
        return jnp.concatenate([p.reshape(-1, LANES) for p in pieces], axis=0)
    if lead:
        flat = jnp.concatenate([p.reshape(N_DEV, -1) for p in pieces], axis=1)
        n = flat.shape[1]
    else:
        flat = jnp.concatenate([p.reshape(-1) for p in pieces])
        n = flat.shape[0]
    group = row_mult * LANES
    total = -(-n // group) * group
    if lead:
        flat = jnp.pad(flat, ((0, 0), (0, total - n)))
        return flat.reshape(N_DEV, total // LANES, LANES)
    flat = jnp.pad(flat, (0, total - n))
    return flat.reshape(total // LANES, LANES)


def _unpack(buf, shapes, lead=False, rows=False):
    out = []
    off = 0
    if rows:
        for shp in shapes:
            n = math.prod(shp) // LANES
            if lead:
                out.append(buf[:, off:off + n].reshape((N_DEV,) + tuple(shp)))
            else:
                out.append(buf[off:off + n].reshape(shp))
            off += n
        return out
    flat = buf.reshape(N_DEV, -1) if lead else buf.reshape(-1)
    for shp in shapes:
        size = math.prod(shp)
        if lead:
            out.append(flat[:, off:off + size].reshape((N_DEV,) + tuple(shp)))
        else:
            out.append(flat[off:off + size].reshape(shp))
        off += size
    return out


BIG = ["attn_w_qkv", "attn_w_o", "rg_w_in", "rg_w_out", "ffn_w_up", "ffn_w_down"]
SMALL = ["rg_conv_w", "rg_conv_b", "rg_b_a", "rg_b_x", "rg_lambda", "ffn_conv_w"]
REPL = ["rg_w_a", "rg_w_x", "ffn_conv_b", "mix_pre_g", "mix_post_g", "ffn_pre_g", "ffn_post_g"]
ORDER = ["attn_w_qkv", "attn_w_o", "rg_w_in", "rg_conv_w", "rg_conv_b", "rg_w_a", "rg_b_a", "rg_w_x", "rg_b_x",
         "rg_lambda", "rg_w_out", "ffn_w_up", "ffn_conv_w", "ffn_conv_b", "ffn_w_down", "mix_pre_g",
         "mix_post_g", "ffn_pre_g", "ffn_post_g"]
SHARD_AXIS = {"attn_w_qkv": 2, "attn_w_o": 1, "rg_w_in": 2, "rg_w_out": 1, "ffn_w_up": 2, "ffn_w_down": 1,
              "rg_conv_w": 2, "rg_conv_b": 1, "rg_b_a": 1, "rg_b_x": 1, "rg_lambda": 1, "ffn_conv_w": 2}
BIG_ROWS = None
SMALL_ROWS = 64
REPL_ROWS = 128


def _assemble(stacked, axis):
    moved = jnp.moveaxis(stacked, 0, axis)
    shp = list(moved.shape)
    shp[axis:axis + 2] = [shp[axis] * shp[axis + 1]]
    return moved.reshape(shp)


def _split(full, axis):
    shp = list(full.shape)
    shp[axis:axis + 1] = [N_DEV, shp[axis] // N_DEV]
    return jnp.moveaxis(full.reshape(shp), axis, 0)


def _block_diag_pad(w):
    out = jnp.zeros((D_RNN_PAD, D_RNN_PAD), w.dtype)
    for n in range(RG_BLOCKS):
        o = n * RG_BLOCK_W
        out = lax.dynamic_update_slice(out, w[n], (o, o))
    return out


def _block_diag_take(dense):
    return jnp.stack([dense[n * RG_BLOCK_W:(n + 1) * RG_BLOCK_W, n * RG_BLOCK_W:(n + 1) * RG_BLOCK_W]
                      for n in range(RG_BLOCKS)])


def _pad_cols(a, width):
    return jnp.pad(a, ((0, 0), (0, width - a.shape[1])))


def _interleave(a):
    lead = a.shape[:-1]
    nb = D_FF // FFN_COL
    return jnp.swapaxes(a.reshape(lead + (2, nb, FFN_COL)), -3, -2).reshape(lead + (2 * D_FF,))


def _deinterleave(a):
    lead = a.shape[:-1]
    nb = D_FF // FFN_COL
    return jnp.swapaxes(a.reshape(lead + (nb, 2, FFN_COL)), -3, -2).reshape(lead + (2 * D_FF,))


def _ffn_fwd(hn, wp, i):
    pre = _matmul(hn, wp["up"][i], name=f"ffn{i}_up")
    act = _ffn_act_fwd(pre, wp["ffn_cw"][i], wp["ffn_cb"][i], name=f"ffn{i}_act")
    f = _matmul(act, wp["down"][i], name=f"ffn{i}_down")
    return pre, act, f


def _ffn_bwd(df, hn, pre, act, wp, i):
    dact = _matmul(df, wp["down"][i], tb=True, name=f"ffn{i}_dact")
    d_down = _matmul(act, df, ta=True, name=f"ffn{i}_ddown")
    dpre, dcw, dcb = _ffn_act_bwd(pre, dact, wp["ffn_cw"][i], wp["ffn_cb"][i], name=f"ffn{i}_act_bwd")
    d_up = _matmul(hn, dpre, ta=True, name=f"ffn{i}_dup")
    dhn = _matmul(dpre, wp["up"][i], tb=True, name=f"ffn{i}_dhn")
    return dhn, d_up, d_down, dcw, dcb


def _local_step(x, tgt, wp):
    g = wp["gains"]
    gr = {}

    hn0 = _rms_fwd(x, g["mix_pre"][0], name="l0_mix_pre")
    qkv = _matmul(hn0, wp["qkv"], out_dtype=BF16, name="attn_qkv")
    o, lsum, first = _attn_fwd(qkv, name="attn_fwd")
    f0 = _matmul(o, wp["wo"], name="attn_out")
    x1, hn1 = _resid_norm(x, f0, g["mix_post"][0], g["ffn_pre"][0], name="l0_mix_post")
    pre0, act0, f1 = _ffn_fwd(hn1, wp, 0)
    x2, hn2 = _resid_norm(x1, f1, g["ffn_post"][0], g["mix_pre"][1], name="l0_ffn_post")
    proj = _matmul(hn2, wp["w_in"], name="rg_in")
    rec, rec_b = _rg_conv_fwd(proj, wp["rg_cw"], wp["rg_cb"], name="rg_conv")
    pre_ax = _matmul(rec_b, wp["w_ax"], name="rg_gates")
    h, gh = _rg_scan_fwd(pre_ax, rec, proj, wp["rg_ba"], wp["rg_bx"], wp["rg_lam"], name="rg_scan")
    f2 = _matmul(gh, wp["w_out"], name="rg_out")
    x3, hn3 = _resid_norm(x2, f2, g["mix_post"][1], g["ffn_pre"][1], name="l1_mix_post")
    pre1, act1, f3 = _ffn_fwd(hn3, wp, 1)
    dy, sq = _resid_loss(x3, f3, g["ffn_post"][1], tgt, name="l1_ffn_post_loss")

    ffn_dup, ffn_ddown, ffn_dcw, ffn_dcb = [None, None], [None, None], [None, None], [None, None]
    d_ffn_post, d_ffn_pre, d_mix_post, d_mix_pre = [None, None], [None, None], [None, None], [None, None]

    df3, d_ffn_post[1] = _rms_bwd(f3, g["ffn_post"][1], dy, None, BF16, name="l1_ffn_post_bwd")
    dhn3, ffn_dup[1], ffn_ddown[1], ffn_dcw[1], ffn_dcb[1] = _ffn_bwd(df3, hn3, pre1, act1, wp, 1)
    dx3, d_ffn_pre[1] = _rms_bwd(x3, g["ffn_pre"][1], dhn3, dy, F32, name="l1_ffn_pre_bwd")

    df2, d_mix_post[1] = _rms_bwd(f2, g["mix_post"][1], dx3, None, BF16, name="l1_mix_post_bwd")
    dgh = _matmul(df2, wp["w_out"], tb=True, name="rg_dgh")
    gr["w_out"] = _matmul(gh, df2, ta=True, name="rg_dwout")
    dgate, dpax, drec_u, gr["rg_ba"], gr["rg_bx"], gr["rg_lam"] = _rg_scan_bwd(
        dgh, proj, h, pre_ax, rec, wp["rg_ba"], wp["rg_bx"], wp["rg_lam"], name="rg_scan_bwd")
    drec_g = _matmul(dpax, wp["w_ax"], tb=True, name="rg_drec")
    gr["w_ax"] = _matmul(rec_b, dpax, ta=True, name="rg_dwax")
    dproj_rec, gr["rg_cw"], gr["rg_cb"] = _rg_conv_bwd(drec_u, drec_g, proj, wp["rg_cw"], name="rg_conv_bwd")
    dproj = jnp.concatenate([dgate, dproj_rec], axis=1)
    gr["w_in"] = _matmul(hn2, dproj, ta=True, name="rg_dwin")
    dhn2 = _matmul(dproj, wp["w_in"], tb=True, name="rg_dhn")
    dx2, d_mix_pre[1] = _rms_bwd(x2, g["mix_pre"][1], dhn2, dx3, F32, name="l1_mix_pre_bwd")

    df1, d_ffn_post[0] = _rms_bwd(f1, g["ffn_post"][0], dx2, None, BF16, name="l0_ffn_post_bwd")
    dhn1, ffn_dup[0], ffn_ddown[0], ffn_dcw[0], ffn_dcb[0] = _ffn_bwd(df1, hn1, pre0, act0, wp, 0)
    dx1, d_ffn_pre[0] = _rms_bwd(x1, g["ffn_pre"][0], dhn1, dx2, F32, name="l0_ffn_pre_bwd")

    df0, d_mix_post[0] = _rms_bwd(f0, g["mix_post"][0], dx1, None, BF16, name="l0_mix_post_bwd")
    do = _matmul(df0, wp["wo"], tb=True, out_dtype=BF16, name="attn_do")
    gr["wo"] = _matmul(o, df0, ta=True, name="attn_dwo")
    dq, dk, dv = _attn_bwd(qkv, do, lsum, first, name="attn_bwd")
    dqkv = jnp.concatenate([dq, dk.astype(BF16), dv.astype(BF16)], axis=1)
    gr["qkv"] = _matmul(hn0, dqkv, ta=True, name="attn_dwqkv")
    dhn0 = _matmul(dqkv, wp["qkv"], tb=True, name="attn_dhn")
    dx0, d_mix_pre[0] = _rms_bwd(x, g["mix_pre"][0], dhn0, dx1, F32, name="l0_mix_pre_bwd")

    gr["up"] = ffn_dup
    gr["down"] = ffn_ddown
    gr["ffn_cw"] = ffn_dcw
    gr["ffn_cb"] = ffn_dcb
    gr["gains"] = {"mix_pre": d_mix_pre, "mix_post": d_mix_post, "ffn_pre": d_ffn_pre, "ffn_post": d_ffn_post}
    return sq[0, 0], dx0, gr


def _prepare(full):
    c, cp = D_RNN, D_RNN_PAD
    w_in = full["rg_w_in"][0]
    w_a = _block_diag_pad(full["rg_w_a"][0].astype(BF16))
    w_x = _block_diag_pad(full["rg_w_x"][0].astype(BF16))
    vec = lambda a: _pad_cols(a.reshape(1, c), cp)
    return {
        "qkv": full["attn_w_qkv"][0],
        "wo": full["attn_w_o"][0],
        "w_in": jnp.concatenate([_pad_cols(w_in[:, :c], cp), _pad_cols(w_in[:, c:], cp)], axis=1),
        "w_ax": jnp.concatenate([w_a, w_x], axis=1),
        "w_out": jnp.pad(full["rg_w_out"][0], ((0, cp - c), (0, 0))),
        "rg_cw": _pad_cols(full["rg_conv_w"][0], cp),
        "rg_cb": vec(full["rg_conv_b"][0]),
        "rg_ba": vec(full["rg_b_a"][0]),
        "rg_bx": vec(full["rg_b_x"][0]),
        "rg_lam": vec(full["rg_lambda"][0]),
        "up": [_interleave(full["ffn_w_up"][i]) for i in range(2)],
        "down": [full["ffn_w_down"][i] for i in range(2)],
        "ffn_cw": [_interleave(full["ffn_conv_w"][i]) for i in range(2)],
        "ffn_cb": [_interleave(full["ffn_conv_b"][i].reshape(1, -1)) for i in range(2)],
        "gains": {k: [full[k + "_g"][i].reshape(1, D_MODEL) for i in range(2)]
                  for k in ("mix_pre", "mix_post", "ffn_pre", "ffn_post")},
    }


def _natural_grads(gr):
    c, cp = D_RNN, D_RNN_PAD
    gg = gr["gains"]
    stack2 = lambda pair: jnp.stack([pair[0].reshape(-1), pair[1].reshape(-1)])
    return {
        "attn_w_qkv": gr["qkv"][None],
        "attn_w_o": gr["wo"][None],
        "rg_w_in": jnp.concatenate([gr["w_in"][:, :c], gr["w_in"][:, cp:cp + c]], axis=1)[None],
        "rg_conv_w": gr["rg_cw"][:, :c][None],
        "rg_conv_b": gr["rg_cb"][:, :c],
        "rg_w_a": _block_diag_take(gr["w_ax"][:, :cp])[None],
        "rg_b_a": gr["rg_ba"][:, :c],
        "rg_w_x": _block_diag_take(gr["w_ax"][:, cp:])[None],
        "rg_b_x": gr["rg_bx"][:, :c],
        "rg_lambda": gr["rg_lam"][:, :c],
        "rg_w_out": gr["w_out"][:c][None],
        "ffn_w_up": jnp.stack([_deinterleave(gr["up"][i]) for i in range(2)]),
        "ffn_conv_w": jnp.stack([_deinterleave(gr["ffn_cw"][i]) for i in range(2)]),
        "ffn_conv_b": jnp.stack([_deinterleave(gr["ffn_cb"][i]).reshape(-1) for i in range(2)]),
        "ffn_w_down": jnp.stack(gr["down"]),
        "mix_pre_g": stack2(gg["mix_pre"]),
        "mix_post_g": stack2(gg["mix_post"]),
        "ffn_pre_g": stack2(gg["ffn_pre"]),
        "ffn_post_g": stack2(gg["ffn_post"]),
    }


def _step(x, loss_target, w, m, v):
    big_shapes = [w[n].shape for n in BIG]
    small_shapes = [w[n].shape for n in SMALL]
    repl_shapes = [w[n].shape for n in REPL]
    wbig = _pack([w[n].astype(BF16) for n in BIG], BIG_ROWS)
    wsmall = _pack([w[n] for n in SMALL], SMALL_ROWS)
    gbig, gsmall = _exchange([(wbig, True), (wsmall, True)], name="gather_weights")
    full = {}
    for n, st in zip(BIG, _unpack(gbig, big_shapes, lead=True, rows=True)):
        full[n] = _assemble(st, SHARD_AXIS[n])
    for n, st in zip(SMALL, _unpack(gsmall, small_shapes, lead=True)):
        full[n] = _assemble(st, SHARD_AXIS[n])
    for n in REPL:
        full[n] = w[n]

    sq, dx, gr = _local_step(x[0], loss_target[0], _prepare(full))
    grads = _natural_grads(gr)

    pbig = _pack([_split(grads[n], SHARD_AXIS[n]).astype(BF16) for n in BIG], BIG_ROWS, lead=True)
    psmall = _pack([_split(grads[n], SHARD_AXIS[n]) for n in SMALL], SMALL_ROWS, lead=True)
    prepl = _pack([grads[n] for n in REPL], REPL_ROWS)
    rbig, rsmall, rrepl = _exchange([(pbig, False), (psmall, False), (prepl, True)], name="exchange_grads")

    out = {}
    for names, parts, rows, shapes, tag in ((BIG, rbig, BIG_ROWS, big_shapes, "big"),
                                            (SMALL, rsmall, SMALL_ROWS, small_shapes, "small"),
                                            (REPL, rrepl, REPL_ROWS, repl_shapes, "repl")):
        packs = [_pack([d[n] for n in names], rows) for d in (w, m, v)]
        res = _adam(parts, *packs, name="adamw_" + tag)
        for kind, buf in zip(("grad", "delta", "new_m", "new_v"), res):
            for n, a in zip(names, _unpack(buf, shapes, rows=rows is None)):
                out[(kind, n)] = a

    loss = lax.psum(sq * (0.5 / D_MODEL), ("x", "y", "c"))
    return loss, dx[None], out


def kernel(x, attn_w_qkv, attn_w_o, rg_w_in, rg_conv_w, rg_conv_b, rg_w_a, rg_b_a, rg_w_x, rg_b_x, rg_lambda, rg_w_out, ffn_w_up, ffn_conv_w, ffn_conv_b, ffn_w_down, mix_pre_g, mix_post_g, ffn_pre_g, ffn_post_g, loss_target, m_attn_w_qkv, m_attn_w_o, m_rg_w_in, m_rg_conv_w, m_rg_conv_b, m_rg_w_a, m_rg_b_a, m_rg_w_x, m_rg_b_x, m_rg_lambda, m_rg_w_out, m_ffn_w_up, m_ffn_conv_w, m_ffn_conv_b, m_ffn_w_down, m_mix_pre_g, m_mix_post_g, m_ffn_pre_g, m_ffn_post_g, v_attn_w_qkv, v_attn_w_o, v_rg_w_in, v_rg_conv_w, v_rg_conv_b, v_rg_w_a, v_rg_b_a, v_rg_w_x, v_rg_b_x, v_rg_lambda, v_rg_w_out, v_ffn_w_up, v_ffn_conv_w, v_ffn_conv_b, v_ffn_w_down, v_mix_pre_g, v_mix_post_g, v_ffn_pre_g, v_ffn_post_g):
    w = dict(zip(ORDER, (attn_w_qkv, attn_w_o, rg_w_in, rg_conv_w, rg_conv_b, rg_w_a, rg_b_a, rg_w_x, rg_b_x,
                         rg_lambda, rg_w_out, ffn_w_up, ffn_conv_w, ffn_conv_b, ffn_w_down, mix_pre_g,
                         mix_post_g, ffn_pre_g, ffn_post_g)))
    m = dict(zip(ORDER, (m_attn_w_qkv, m_attn_w_o, m_rg_w_in, m_rg_conv_w, m_rg_conv_b, m_rg_w_a, m_rg_b_a,
                         m_rg_w_x, m_rg_b_x, m_rg_lambda, m_rg_w_out, m_ffn_w_up, m_ffn_conv_w, m_ffn_conv_b,
                         m_ffn_w_down, m_mix_pre_g, m_mix_post_g, m_ffn_pre_g, m_ffn_post_g)))
    v = dict(zip(ORDER, (v_attn_w_qkv, v_attn_w_o, v_rg_w_in, v_rg_conv_w, v_rg_conv_b, v_rg_w_a, v_rg_b_a,
                         v_rg_w_x, v_rg_b_x, v_rg_lambda, v_rg_w_out, v_ffn_w_up, v_ffn_conv_w, v_ffn_conv_b,
                         v_ffn_w_down, v_mix_pre_g, v_mix_post_g, v_ffn_pre_g, v_ffn_post_g)))
    loss, dx, out = _step(x, loss_target, w, m, v)
    return (loss, dx, *[out[("grad", n)] for n in ORDER], *[out[("delta", n)] for n in ORDER],
            *[out[("new_m", n)] for n in ORDER], *[out[("new_v", n)] for n in ORDER])
```

```python
import functools
import math

import jax
import jax.numpy as jnp
from jax import lax
from jax.experimental import pallas as pl
from jax.experimental.pallas import tpu as pltpu

F32 = jnp.float32
BF16 = jnp.bfloat16

D_MODEL = 1024
N_HEADS = 16
HEAD_DIM = 64
D_RNN = 1344
D_RNN_PAD = 1408
RG_BLOCKS = 16
RG_BLOCK_W = 84
RG_CONV_W = 4
RG_C = 8.0
D_FF = 2816
FFN_CONV_W = 3
FFN_COL = 256
NORM_EPS = 1e-6
N_DEV = 8
LANES = 128
HALO = 8
STRIP = 32

ADAM_LR = 0.001
ADAM_B1 = 0.9
ADAM_B2 = 0.999
ADAM_EPS = 1e-08
ADAM_WD = 0.01
ADAM_STEP = 10

ATT_BLK = 256
ATT_HEADS = 4
ATT_STOP_LOG = -110.0
VMEM_LIMIT = 56 * 1024 * 1024

_GELU_C = math.sqrt(2.0 / math.pi)


def _params(sem=None):
    return pltpu.CompilerParams(dimension_semantics=sem, vmem_limit_bytes=VMEM_LIMIT)


def _tile(n, cap, mult=8):
    if n <= cap:
        return n
    best = None
    for t in range(mult, cap + 1, mult):
        if n % t == 0:
            best = t
    assert best is not None, (n, cap, mult)
    return best


def _gelu_and_grad(x):
    x2 = x * x
    th = jnp.tanh(_GELU_C * (x + 0.044715 * (x2 * x)))
    cdf = 0.5 * (1.0 + th)
    gel = x * cdf
    dgel = cdf + 0.5 * x * (1.0 - th * th) * (_GELU_C * (1.0 + 3.0 * 0.044715 * x2))
    return gel, dgel


def _gelu(x):
    th = jnp.tanh(_GELU_C * (x + 0.044715 * (x * x * x)))
    return x * (0.5 * (1.0 + th))


def _matmul(a, b, *, ta=False, tb=False, out_dtype=F32, name):
    if ta:
        kdim, m = a.shape
    else:
        m, kdim = a.shape
    if tb:
        n, kb_ = b.shape
    else:
        kb_, n = b.shape
    assert kdim == kb_, (a.shape, b.shape, ta, tb)
    wide = lambda d: 1408 if d % 1408 == 0 else 1024
    tm = _tile(m, wide(m), 128) if ta else _tile(m, 512)
    tn = _tile(n, wide(n), 128)
    tk = _tile(kdim, 512, 128) if ta else _tile(kdim, wide(kdim), 128)
    nk = kdim // tk
    dims = (((0 if ta else 1,), (1 if tb else 0,)), ((), ()))

    def body(a_ref, b_ref, o_ref, acc_ref):
        part = lax.dot_general(a_ref[...], b_ref[...], dims, preferred_element_type=F32)
        if nk == 1:
            o_ref[...] = part.astype(out_dtype)
        else:
            k = pl.program_id(2)

            @pl.when(k == 0)
            def _():
                acc_ref[...] = part

            @pl.when(k > 0)
            def _():
                acc_ref[...] += part

            @pl.when(k == nk - 1)
            def _():
                o_ref[...] = acc_ref[...].astype(out_dtype)

    a_spec = pl.BlockSpec((tk, tm), lambda i, j, k: (k, i)) if ta else pl.BlockSpec((tm, tk), lambda i, j, k: (i, k))
    b_spec = pl.BlockSpec((tn, tk), lambda i, j, k: (j, k)) if tb else pl.BlockSpec((tk, tn), lambda i, j, k: (k, j))
    return pl.pallas_call(
        body,
        grid=(m // tm, n // tn, nk),
        in_specs=[a_spec, b_spec],
        out_specs=pl.BlockSpec((tm, tn), lambda i, j, k: (i, j)),
        out_shape=jax.ShapeDtypeStruct((m, n), out_dtype),
        scratch_shapes=[pltpu.VMEM((tm, tn), F32)],
        compiler_params=_params(("parallel", "parallel", "arbitrary")),
        name=name,
    )(a, b)


def _rms(xv, g):
    inv = lax.rsqrt(jnp.mean(xv * xv, axis=-1, keepdims=True) + NORM_EPS)
    return xv * inv * g


def _rms_fwd(x, g, name):
    s, d = x.shape
    t = _tile(s, 512)

    def body(x_ref, g_ref, o_ref):
        o_ref[...] = _rms(x_ref[...], g_ref[...]).astype(BF16)

    return pl.pallas_call(
        body, grid=(s // t,),
        in_specs=[pl.BlockSpec((t, d), lambda i: (i, 0)), pl.BlockSpec((1, d), lambda i: (0, 0))],
        out_specs=pl.BlockSpec((t, d), lambda i: (i, 0)),
        out_shape=jax.ShapeDtypeStruct((s, d), BF16),
        compiler_params=_params(("parallel",)), name=name,
    )(x, g)


def _resid_norm(x, f, post_g, pre_g, name):
    s, d = x.shape
    t = _tile(s, 512)

    def body(x_ref, f_ref, pg_ref, ng_ref, xo_ref, hn_ref):
        xn = x_ref[...] + _rms(f_ref[...], pg_ref[...])
        xo_ref[...] = xn
        hn_ref[...] = _rms(xn, ng_ref[...]).astype(BF16)

    row = pl.BlockSpec((t, d), lambda i: (i, 0))
    vec = pl.BlockSpec((1, d), lambda i: (0, 0))
    return pl.pallas_call(
        body, grid=(s // t,), in_specs=[row, row, vec, vec], out_specs=[row, row],
        out_shape=[jax.ShapeDtypeStruct((s, d), F32), jax.ShapeDtypeStruct((s, d), BF16)],
        compiler_params=_params(("parallel",)), name=name,
    )(x, f, post_g, pre_g)


def _resid_loss(x, f, post_g, tgt, name):
    s, d = x.shape
    t = _tile(s, 512)

    def body(x_ref, f_ref, pg_ref, t_ref, dy_ref, loss_ref):
        @pl.when(pl.program_id(0) == 0)
        def _():
            loss_ref[...] = jnp.zeros_like(loss_ref)

        err = x_ref[...] + _rms(f_ref[...], pg_ref[...]) - t_ref[...]
        dy_ref[...] = err * (1.0 / d)
        sq = jnp.sum(jnp.sum(err * err, axis=1, keepdims=True), axis=0, keepdims=True)
        loss_ref[...] += sq

    row = pl.BlockSpec((t, d), lambda i: (i, 0))
    vec = pl.BlockSpec((1, d), lambda i: (0, 0))
    return pl.pallas_call(
        body, grid=(s // t,), in_specs=[row, row, vec, row],
        out_specs=[row, pl.BlockSpec((8, LANES), lambda i: (0, 0))],
        out_shape=[jax.ShapeDtypeStruct((s, d), F32), jax.ShapeDtypeStruct((8, LANES), F32)],
        compiler_params=_params(("arbitrary",)), name=name,
    )(x, f, post_g, tgt)


def _rms_bwd(xin, g, dy, add, out_dtype, name):
    s, d = xin.shape
    t = _tile(s, 512)
    has_add = add is not None

    def body(*refs):
        if has_add:
            x_ref, g_ref, dy_ref, add_ref, dx_ref, dg_ref = refs
        else:
            x_ref, g_ref, dy_ref, dx_ref, dg_ref = refs

        @pl.when(pl.program_id(0) == 0)
        def _():
            dg_ref[...] = jnp.zeros_like(dg_ref)

        xv = x_ref[...]
        dyv = dy_ref[...].astype(F32)
        inv = lax.rsqrt(jnp.mean(xv * xv, axis=-1, keepdims=True) + NORM_EPS)
        xh = xv * inv
        gy = dyv * g_ref[...]
        dx = inv * (gy - xh * jnp.mean(xh * gy, axis=-1, keepdims=True))
        if has_add:
            dx = dx + add_ref[...]
        dx_ref[...] = dx.astype(out_dtype)
        dg_ref[...] += jnp.sum(dyv * xh, axis=0, keepdims=True)

    row = pl.BlockSpec((t, d), lambda i: (i, 0))
    vec = pl.BlockSpec((1, d), lambda i: (0, 0))
    args = [xin, g, dy] + ([add] if has_add else [])
    specs = [row, vec, row] + ([row] if has_add else [])
    return pl.pallas_call(
        body, grid=(s // t,), in_specs=specs, out_specs=[row, vec],
        out_shape=[jax.ShapeDtypeStruct((s, d), out_dtype), jax.ShapeDtypeStruct((1, d), F32)],
        compiler_params=_params(("arbitrary",)), name=name,
    )(*args)


_NT = (((1,), (1,)), ((), ()))
_NN = (((1,), (0,)), ((), ()))
_TN = (((0,), (0,)), ((), ()))


def _tri_dot(x, tri):
    hi = x.astype(BF16)
    lo = (x - hi.astype(F32)).astype(BF16)
    out = lax.dot_general(hi, tri, _NN, preferred_element_type=F32)
    return out + lax.dot_general(lo, tri, _NN, preferred_element_type=F32)


def _log_sigmoids(z):
    lb = jnp.minimum(z, 0.0) - jnp.log(1.0 + jnp.exp(-jnp.abs(z)))
    return lb, lb - z


def _att_blk(s):
    return min(ATT_BLK, s)


def _attn_fwd(qkv, name):
    s = qkv.shape[0]
    blk = _att_blk(s)
    nq = s // blk
    nh = ATT_HEADS
    width = nh * HEAD_DIM
    ngrp = N_HEADS // nh
    hs = range(nh)
    scale = HEAD_DIM ** -0.5

    def body(q_ref, k_ref, v_ref, o_ref, l_ref, first_ref):
        qi = pl.program_id(1)
        row = lax.broadcasted_iota(jnp.int32, (blk, blk), 0)
        col = lax.broadcasted_iota(jnp.int32, (blk, blk), 1)
        causal = col < row
        tri = (row > col).astype(BF16)
        lane_head = lax.broadcasted_iota(jnp.int32, (blk, width), 1) // HEAD_DIM
        masks = [lane_head == hh for hh in hs]
        qs = q_ref[...] * scale
        qh = [jnp.where(m, qs, jnp.zeros_like(qs)) for m in masks]

        def block(kb, carry, diag):
            rsums, acc = carry
            r0 = pl.multiple_of(kb * blk, blk)
            k2 = k_ref[pl.ds(r0, blk), :]
            v2 = v_ref[pl.ds(r0, blk), :]
            z = [lax.dot_general(qh[hh], k2, _NT, preferred_element_type=F32) for hh in hs]
            lbm = [_log_sigmoids(z[hh]) for hh in hs]
            lb = [p[0] for p in lbm]
            lm = [jnp.where(causal, p[1], 0.0) if diag else p[1] for p in lbm]
            cs = [_tri_dot(lm[hh], tri) for hh in hs]
            w = [jnp.exp(lb[hh] + cs[hh] + rsums[hh]) for hh in hs]
            if diag:
                w = [jnp.where(causal, w[hh], 0.0) for hh in hs]
            for hh in hs:
                vh = jnp.where(masks[hh], v2, jnp.zeros_like(v2))
                acc = acc + lax.dot_general(w[hh].astype(BF16), vh, _NN, preferred_element_type=F32)
            return tuple(rsums[hh] + (cs[hh][:, 0:1] + lm[hh][:, 0:1]) for hh in hs), acc

        def largest(rsums):
            m = jnp.max(rsums[0])
            for hh in range(1, nh):
                m = jnp.maximum(m, jnp.max(rsums[hh]))
            return m

        def more(c):
            return jnp.logical_and(c[0] < qi, c[1] > ATT_STOP_LOG)

        def step(c):
            rsums, acc = block(qi - 1 - c[0], (c[2], c[3]), False)
            return c[0] + 1, largest(rsums), rsums, acc

        zero = jnp.zeros((blk, 1), F32)
        rsums, acc = block(qi, ((zero,) * nh, jnp.zeros((blk, width), F32)), True)
        done, _, rsums, acc = lax.while_loop(more, step, (jnp.int32(0), largest(rsums), rsums, acc))
        first_ref[pl.program_id(0) * nq + qi] = (qi - done).astype(F32)
        o_ref[...] = acc.astype(BF16)
        lout = rsums[nh - 1]
        for hh in range(nh - 2, -1, -1):
            lout = jnp.where(masks[hh], rsums[hh], lout)
        l_ref[...] = lout

    return pl.pallas_call(
        body, grid=(ngrp, nq),
        in_specs=[pl.BlockSpec((blk, width), lambda h, i: (i, h)),
                  pl.BlockSpec((s, width), lambda h, i: (0, ngrp + h)),
                  pl.BlockSpec((s, width), lambda h, i: (0, 2 * ngrp + h))],
        out_specs=[pl.BlockSpec((blk, width), lambda h, i: (i, h)), pl.BlockSpec((blk, width), lambda h, i: (i, h)),
                   pl.BlockSpec(memory_space=pltpu.SMEM)],
        out_shape=[jax.ShapeDtypeStruct((s, D_MODEL), BF16), jax.ShapeDtypeStruct((s, D_MODEL), F32),
                   jax.ShapeDtypeStruct((ngrp * nq,), F32)],
        compiler_params=_params(("arbitrary", "arbitrary")), name=name,
    )(qkv, qkv, qkv)


def _attn_bwd(qkv, do, lsum, first, name):
    s = qkv.shape[0]
    blk = _att_blk(s)
    nq = s // blk
    nh = ATT_HEADS
    width = nh * HEAD_DIM
    ngrp = N_HEADS // nh
    hs = range(nh)
    scale = HEAD_DIM ** -0.5

    def body(first_ref, q_ref, k_ref, v_ref, do_ref, l_ref, dq_ref, dk_ref, dv_ref):
        qi = pl.program_id(1)
        start = first_ref[pl.program_id(0) * nq + qi].astype(jnp.int32)

        @pl.when(qi == 0)
        def _():
            dk_ref[...] = jnp.zeros_like(dk_ref)
            dv_ref[...] = jnp.zeros_like(dv_ref)

        row = lax.broadcasted_iota(jnp.int32, (blk, blk), 0)
        col = lax.broadcasted_iota(jnp.int32, (blk, blk), 1)
        causal = col < row
        tri_in = (row <= col).astype(BF16)
        tri_ex = (row < col).astype(BF16)
        lane_head = lax.broadcasted_iota(jnp.int32, (blk, width), 1) // HEAD_DIM
        masks = [lane_head == hh for hh in hs]
        q2 = q_ref[...]
        qs = q2 * scale
        do2 = do_ref[...]
        l2 = l_ref[...]
        qh = [jnp.where(m, q2, jnp.zeros_like(q2)) for m in masks]
        qsh = [jnp.where(m, qs, jnp.zeros_like(qs)) for m in masks]
        doh = [jnp.where(m, do2, jnp.zeros_like(do2)) for m in masks]
        ltot = [l2[:, hh * HEAD_DIM:hh * HEAD_DIM + 1] for hh in hs]
        last = slice(blk - 1, blk)

        def block(kb, carry, diag):
            r0 = pl.multiple_of(kb * blk, blk)
            k2 = k_ref[pl.ds(r0, blk), :]
            v2 = v_ref[pl.ds(r0, blk), :]
            dq, pre_lm, pre_da = carry
            mask = (lambda a: jnp.where(causal, a, 0.0)) if diag else (lambda a: a)
            z = [lax.dot_general(qsh[hh], k2, _NT, preferred_element_type=F32) for hh in hs]
            dw = [lax.dot_general(doh[hh], v2, _NT, preferred_element_type=F32) for hh in hs]
            lbm = [_log_sigmoids(z[hh]) for hh in hs]
            lb = [p[0] for p in lbm]
            lm = [mask(p[1]) for p in lbm]
            pin = [_tri_dot(lm[hh], tri_in) for hh in hs]
            w = [mask(jnp.exp(lb[hh] + (ltot[hh] - pre_lm[hh]) - pin[hh])) for hh in hs]
            da = [w[hh] * dw[hh] for hh in hs]
            cex = [_tri_dot(da[hh], tri_ex) for hh in hs]
            dzb = [(mask(da[hh] - (da[hh] + (cex[hh] + pre_da[hh])) * jnp.exp(lb[hh])) * scale).astype(BF16)
                   for hh in hs]
            dk_blk = jnp.zeros((blk, width), F32)
            dv_blk = jnp.zeros((blk, width), F32)
            for hh in hs:
                kh = jnp.where(masks[hh], k2, jnp.zeros_like(k2))
                dq = dq + lax.dot_general(dzb[hh], kh, _NN, preferred_element_type=F32)
                dk_blk = dk_blk + lax.dot_general(dzb[hh], qh[hh], _TN, preferred_element_type=F32)
                dv_blk = dv_blk + lax.dot_general(w[hh].astype(BF16), doh[hh], _TN, preferred_element_type=F32)
            dk_ref[pl.ds(r0, blk), :] += dk_blk
            dv_ref[pl.ds(r0, blk), :] += dv_blk
            return (dq, tuple(pre_lm[hh] + pin[hh][:, last] for hh in hs),
                    tuple(pre_da[hh] + (cex[hh][:, last] + da[hh][:, last]) for hh in hs))

        zero = jnp.zeros((blk, 1), F32)
        init = (jnp.zeros((blk, width), F32), (zero,) * nh, (zero,) * nh)
        carry = lax.fori_loop(start, qi, lambda i, c: block(i, c, False), init)
        dq, _, _ = block(qi, carry, True)
        dq_ref[...] = dq.astype(BF16)

    qblk = pl.BlockSpec((blk, width), lambda h, i: (i, h))
    once = pl.Buffered(1)
    full = pl.BlockSpec((s, width), lambda h, i: (0, h), pipeline_mode=once)
    return pl.pallas_call(
        body, grid=(ngrp, nq),
        in_specs=[pl.BlockSpec(memory_space=pltpu.SMEM), qblk,
                  pl.BlockSpec((s, width), lambda h, i: (0, ngrp + h), pipeline_mode=once),
                  pl.BlockSpec((s, width), lambda h, i: (0, 2 * ngrp + h), pipeline_mode=once),
                  qblk, qblk],
        out_specs=[qblk, full, full],
        out_shape=[jax.ShapeDtypeStruct((s, D_MODEL), BF16), jax.ShapeDtypeStruct((s, D_MODEL), F32),
                   jax.ShapeDtypeStruct((s, D_MODEL), F32)],
        compiler_params=_params(("arbitrary", "arbitrary")), name=name,
    )(first, qkv, qkv, qkv, do, lsum)


def _prev_halo_spec(t, c, col):
    return pl.BlockSpec((HALO, c), lambda *g: (jnp.maximum(g[0] * (t // HALO) - 1, 0), col(*g)))


def _conv_taps(scr, w_ref, b_ref, kw, rows, lead):
    out = b_ref[...] + w_ref[kw - 1:kw, :] * scr[pl.ds(lead, rows), :]
    for k in range(kw - 1):
        out = out + w_ref[k:k + 1, :] * scr[pl.ds(lead - (kw - 1) + k, rows), :]
    return out


def _ffn_act_fwd(pre, cw, cb, name):
    s, c2 = pre.shape
    t = _tile(s, 512)
    cb2 = 2 * FFN_COL
    ncol = c2 // cb2
    kw = FFN_CONV_W

    def body(x_ref, halo_ref, w_ref, b_ref, o_ref, scr):
        i = pl.program_id(0)
        scr[0:HALO, :] = jnp.where(i > 0, halo_ref[...], 0.0)
        scr[HALO:HALO + t, :] = x_ref[...]
        w = [w_ref[k:k + 1, :] for k in range(kw)]
        b = b_ref[...]
        for r0 in range(0, t, STRIP):
            rows = min(STRIP, t - r0)
            u = b + w[kw - 1] * scr[HALO + r0:HALO + r0 + rows, :]
            for k in range(kw - 1):
                lo = HALO - (kw - 1) + k + r0
                u = u + w[k] * scr[lo:lo + rows, :]
            o_ref[r0:r0 + rows, :] = (_gelu(u[:, :FFN_COL]) * u[:, FFN_COL:]).astype(BF16)

    return pl.pallas_call(
        body, grid=(s // t, ncol),
        in_specs=[pl.BlockSpec((t, cb2), lambda i, j: (i, j)),
                  _prev_halo_spec(t, cb2, lambda i, j: j),
                  pl.BlockSpec((FFN_CONV_W, cb2), lambda i, j: (0, j)),
                  pl.BlockSpec((1, cb2), lambda i, j: (0, j))],
        out_specs=pl.BlockSpec((t, FFN_COL), lambda i, j: (i, j)),
        out_shape=jax.ShapeDtypeStruct((s, c2 // 2), BF16),
        scratch_shapes=[pltpu.VMEM((t + HALO, cb2), F32)],
        compiler_params=_params(("parallel", "parallel")), name=name,
    )(pre, pre, cw, cb)


def _ffn_act_bwd(pre, dact, cw, cb, name):
    s, c2 = pre.shape
    t = _tile(s, 512)
    cb2 = 2 * FFN_COL
    ncol = c2 // cb2
    nt = s // t
    kw = FFN_CONV_W
    ext = t + HALO

    def body(x_ref, prev_ref, next_ref, da_ref, dan_ref, w_ref, b_ref, dx_ref, dw_ref, db_ref, xs, dus):
        i = pl.program_id(1)

        @pl.when(i == 0)
        def _():
            dw_ref[...] = jnp.zeros_like(dw_ref)
            db_ref[...] = jnp.zeros_like(db_ref)

        xs[0:HALO, :] = jnp.where(i > 0, prev_ref[...], 0.0)
        xs[HALO:HALO + t, :] = x_ref[...]
        xs[HALO + t:HALO + ext, :] = next_ref[...]
        w = [w_ref[k:k + 1, :] for k in range(kw)]
        b = b_ref[...]
        for r0 in range(0, ext, STRIP):
            rows = min(STRIP, ext - r0)
            u = b + w[kw - 1] * xs[HALO + r0:HALO + r0 + rows, :]
            for k in range(kw - 1):
                lo = HALO - (kw - 1) + k + r0
                u = u + w[k] * xs[lo:lo + rows, :]
            gel, dgel = _gelu_and_grad(u[:, :FFN_COL])
            if r0 < t:
                dav = da_ref[r0:r0 + rows, :]
            else:
                dav = jnp.where(i < nt - 1, dan_ref[...], 0.0)
            dus[r0:r0 + rows, :FFN_COL] = dav * u[:, FFN_COL:] * dgel
            dus[r0:r0 + rows, FFN_COL:] = dav * gel
        fold = lambda a: a.reshape(a.shape[0] // 8, 8, cb2).sum(axis=0)
        db = jnp.zeros((8, cb2), F32)
        dw = [jnp.zeros((8, cb2), F32) for _ in range(kw)]
        for r0 in range(0, t, STRIP):
            rows = min(STRIP, t - r0)
            du = dus[r0:r0 + rows, :]
            dx = w[kw - 1] * du
            for k in range(kw - 1):
                lo = r0 + kw - 1 - k
                dx = dx + w[k] * dus[lo:lo + rows, :]
            dx_ref[r0:r0 + rows, :] = dx.astype(BF16)
            db = db + fold(du)
            for k in range(kw):
                lo = HALO - (kw - 1) + k + r0
                dw[k] = dw[k] + fold(du * xs[lo:lo + rows, :])
        db_ref[...] += jnp.sum(db, axis=0, keepdims=True)
        for k in range(kw):
            dw_ref[k:k + 1, :] += jnp.sum(dw[k], axis=0, keepdims=True)

    nxt = lambda c: pl.BlockSpec((HALO, c), lambda j, i: (jnp.minimum((i + 1) * (t // HALO), s // HALO - 1), j))
    return pl.pallas_call(
        body, grid=(ncol, nt),
        in_specs=[pl.BlockSpec((t, cb2), lambda j, i: (i, j)),
                  pl.BlockSpec((HALO, cb2), lambda j, i: (jnp.maximum(i * (t // HALO) - 1, 0), j)),
                  nxt(cb2),
                  pl.BlockSpec((t, FFN_COL), lambda j, i: (i, j)),
                  nxt(FFN_COL),
                  pl.BlockSpec((kw, cb2), lambda j, i: (0, j)),
                  pl.BlockSpec((1, cb2), lambda j, i: (0, j))],
        out_specs=[pl.BlockSpec((t, cb2), lambda j, i: (i, j)),
                   pl.BlockSpec((kw, cb2), lambda j, i: (0, j)),
                   pl.BlockSpec((1, cb2), lambda j, i: (0, j))],
        out_shape=[jax.ShapeDtypeStruct((s, c2), BF16), jax.ShapeDtypeStruct((kw, c2), F32),
                   jax.ShapeDtypeStruct((1, c2), F32)],
        scratch_shapes=[pltpu.VMEM((ext + HALO, cb2), F32), pltpu.VMEM((ext, cb2), F32)],
        compiler_params=_params(("parallel", "arbitrary")), name=name,
    )(pre, pre, pre, dact, dact, cw, cb)


def _rg_conv_fwd(proj, cw, cb, name):
    s = proj.shape[0]
    c = D_RNN_PAD
    t = _tile(s, 256)

    def body(x_ref, halo_ref, w_ref, b_ref, o_ref, ob_ref, scr):
        i = pl.program_id(0)
        scr[0:HALO, :] = jnp.where(i > 0, halo_ref[...], 0.0)
        scr[HALO:HALO + t, :] = x_ref[...]
        u = _conv_taps(scr, w_ref, b_ref, RG_CONV_W, t, HALO)
        o_ref[...] = u
        ob_ref[...] = u.astype(BF16)

    row = pl.BlockSpec((t, c), lambda i: (i, 0))
    return pl.pallas_call(
        body, grid=(s // t,),
        in_specs=[pl.BlockSpec((t, c), lambda i: (i, 1)),
                  _prev_halo_spec(t, c, lambda i: 1),
                  pl.BlockSpec((RG_CONV_W, c), lambda i: (0, 0)),
                  pl.BlockSpec((1, c), lambda i: (0, 0))],
        out_specs=[row, row],
        out_shape=[jax.ShapeDtypeStruct((s, c), F32), jax.ShapeDtypeStruct((s, c), BF16)],
        scratch_shapes=[pltpu.VMEM((t + HALO, c), F32)],
        compiler_params=_params(("parallel",)), name=name,
    )(proj, proj, cw, cb)


def _neg_expm1(x):
    y = jnp.exp(x)
    ly = jnp.log(y)
    safe = jnp.where(ly == 0.0, 1.0, ly)
    em = jnp.where(y == 1.0, x, (y - 1.0) * x / safe)
    em = jnp.where(x < -30.0, -1.0, em)
    return -em


def _rg_gates(pre_ax, rec, ba, bx, lam):
    c = D_RNN_PAD
    r = jax.nn.sigmoid(pre_ax[:, :c] + ba)
    ig = jax.nn.sigmoid(pre_ax[:, c:] + bx)
    sp = jnp.maximum(-lam, 0.0) + jnp.log1p(jnp.exp(-jnp.abs(lam)))
    log_a = (-RG_C) * r * sp
    a = jnp.exp(log_a)
    m2 = _neg_expm1(2.0 * log_a)
    mult = jnp.sqrt(m2)
    return r, ig, sp, a, m2, mult


def _rg_scan_fwd(pre_ax, rec, proj, ba, bx, lam, name):
    s = rec.shape[0]
    c = D_RNN_PAD
    t = _tile(s, 256)

    def body(pax_ref, rec_ref, gate_ref, ba_ref, bx_ref, lam_ref, h_ref, gh_ref, a_scr, u_scr, carry):
        @pl.when(pl.program_id(0) == 0)
        def _():
            carry[...] = jnp.zeros_like(carry)

        recv = rec_ref[...]
        _, ig, _, a, _, mult = _rg_gates(pax_ref[...], recv, ba_ref[...], bx_ref[...], lam_ref[...])
        a_scr[...] = a
        u_scr[...] = mult * ig * recv
        rowid = lax.broadcasted_iota(jnp.int32, (8, c), 0)

        def group(gi, h):
            r0 = pl.multiple_of(gi * 8, 8)
            a8 = a_scr[pl.ds(r0, 8), :]
            u8 = u_scr[pl.ds(r0, 8), :]
            out = jnp.zeros((8, c), F32)
            for j in range(8):
                h = jnp.broadcast_to(a8[j:j + 1, :], (8, c)) * h + jnp.broadcast_to(u8[j:j + 1, :], (8, c))
                out = jnp.where(rowid == j, h, out)
            h_ref[pl.ds(r0, 8), :] = out
            return h

        carry[...] = lax.fori_loop(0, t // 8, group, carry[...])
        gh_ref[...] = (_gelu(gate_ref[...]) * h_ref[...]).astype(BF16)

    row = pl.BlockSpec((t, c), lambda i: (i, 0))
    vec = pl.BlockSpec((1, c), lambda i: (0, 0))
    return pl.pallas_call(
        body, grid=(s // t,),
        in_specs=[pl.BlockSpec((t, 2 * c), lambda i: (i, 0)), row, row, vec, vec, vec],
        out_specs=[row, row],
        out_shape=[jax.ShapeDtypeStruct((s, c), F32), jax.ShapeDtypeStruct((s, c), BF16)],
        scratch_shapes=[pltpu.VMEM((t, c), F32), pltpu.VMEM((t, c), F32), pltpu.VMEM((8, c), F32)],
        compiler_params=_params(("arbitrary",)), name=name,
    )(pre_ax, rec, proj, ba, bx, lam)


def _rg_scan_bwd(dgh, proj, h, pre_ax, rec, ba, bx, lam, name):
    s = rec.shape[0]
    c = D_RNN_PAD
    t = _tile(s, 256)
    nt = s // t

    def body(dgh_ref, gate_ref, h_ref, hprev_ref, pax_ref, rec_ref, ba_ref, bx_ref, lam_ref,
             dgate_ref, dpax_ref, drec_ref, dba_ref, dbx_ref, dlam_ref, a_scr, dh_scr, g_scr, hp_scr, carry):
        i = pl.program_id(0)
        ti = nt - 1 - i

        @pl.when(i == 0)
        def _():
            carry[...] = jnp.zeros_like(carry)
            dba_ref[...] = jnp.zeros_like(dba_ref)
            dbx_ref[...] = jnp.zeros_like(dbx_ref)
            dlam_ref[...] = jnp.zeros_like(dlam_ref)

        recv = rec_ref[...]
        lam = lam_ref[...]
        r, ig, sp, a, m2, mult = _rg_gates(pax_ref[...], recv, ba_ref[...], bx_ref[...], lam)
        gel, dgel = _gelu_and_grad(gate_ref[...])
        dghv = dgh_ref[...]
        hv = h_ref[...]
        dgate_ref[...] = (dghv * hv * dgel).astype(BF16)
        a_scr[...] = a
        dh_scr[...] = dghv * gel
        rowid = lax.broadcasted_iota(jnp.int32, (8, c), 0)

        def group(gi, cr):
            r0 = pl.multiple_of((t // 8 - 1 - gi) * 8, 8)
            a8 = a_scr[pl.ds(r0, 8), :]
            d8 = dh_scr[pl.ds(r0, 8), :]
            out = jnp.zeros((8, c), F32)
            for j in range(7, -1, -1):
                g = jnp.broadcast_to(d8[j:j + 1, :], (8, c)) + cr
                out = jnp.where(rowid == j, g, out)
                cr = jnp.broadcast_to(a8[j:j + 1, :], (8, c)) * g
            g_scr[pl.ds(r0, 8), :] = out
            return cr

        carry[...] = lax.fori_loop(0, t // 8, group, carry[...])
        g = g_scr[...]
        hp_scr[0:HALO, :] = jnp.where(ti > 0, hprev_ref[...], 0.0)
        hp_scr[HALO:HALO + t, :] = hv
        da = g * hp_scr[pl.ds(HALO - 1, t), :]
        dmult = g * ig * recv
        dig = g * mult * recv
        drec_ref[...] = g * mult * ig
        dm2 = dmult * 0.5 / mult
        dloga = da * a - 2.0 * (1.0 - m2) * dm2
        dr = dloga * ((-RG_C) * sp)
        dsp = jnp.sum(dloga * ((-RG_C) * r), axis=0, keepdims=True)
        dpa = dr * r * (1.0 - r)
        dpx = dig * ig * (1.0 - ig)
        dpax_ref[:, :c] = dpa.astype(BF16)
        dpax_ref[:, c:] = dpx.astype(BF16)
        dba_ref[...] += jnp.sum(dpa, axis=0, keepdims=True)
        dbx_ref[...] += jnp.sum(dpx, axis=0, keepdims=True)
        dlam_ref[...] += dsp * (-jnp.exp(-(lam + sp)))

    row = pl.BlockSpec((t, c), lambda i: (nt - 1 - i, 0))
    vec = pl.BlockSpec((1, c), lambda i: (0, 0))
    return pl.pallas_call(
        body, grid=(nt,),
        in_specs=[row, row, row,
                  pl.BlockSpec((HALO, c), lambda i: (jnp.maximum((nt - 1 - i) * (t // HALO) - 1, 0), 0)),
                  pl.BlockSpec((t, 2 * c), lambda i: (nt - 1 - i, 0)), row, vec, vec, vec],
        out_specs=[row, pl.BlockSpec((t, 2 * c), lambda i: (nt - 1 - i, 0)), row, vec, vec, vec],
        out_shape=[jax.ShapeDtypeStruct((s, c), BF16), jax.ShapeDtypeStruct((s, 2 * c), BF16),
                   jax.ShapeDtypeStruct((s, c), F32), jax.ShapeDtypeStruct((1, c), F32),
                   jax.ShapeDtypeStruct((1, c), F32), jax.ShapeDtypeStruct((1, c), F32)],
        scratch_shapes=[pltpu.VMEM((t, c), F32), pltpu.VMEM((t, c), F32), pltpu.VMEM((t, c), F32),
                        pltpu.VMEM((t + HALO, c), F32), pltpu.VMEM((8, c), F32)],
        compiler_params=_params(("arbitrary",)), name=name,
    )(dgh, proj, h, h, pre_ax, rec, ba, bx, lam)


def _rg_conv_bwd(drec_a, drec_b, proj, cw, name):
    s = drec_a.shape[0]
    c = D_RNN_PAD
    t = _tile(s, 256)
    nt = s // t
    kw = RG_CONV_W
    ext = t + HALO

    def body(da_ref, dan_ref, db_ref, dbn_ref, x_ref, prev_ref, w_ref, dx_ref, dw_ref, dbias_ref, xs, dus):
        i = pl.program_id(0)

        @pl.when(i == 0)
        def _():
            dw_ref[...] = jnp.zeros_like(dw_ref)
            dbias_ref[...] = jnp.zeros_like(dbias_ref)

        xs[0:HALO, :] = jnp.where(i > 0, prev_ref[...], 0.0)
        xs[HALO:HALO + t, :] = x_ref[...]
        du = da_ref[...] + db_ref[...]
        dus[0:t, :] = du
        dus[t:ext, :] = jnp.where(i < nt - 1, dan_ref[...] + dbn_ref[...], 0.0)
        dx = w_ref[kw - 1:kw, :] * du
        for k in range(kw - 1):
            dx = dx + w_ref[k:k + 1, :] * dus[pl.ds(kw - 1 - k, t), :]
        dx_ref[...] = dx.astype(BF16)
        dbias_ref[...] += jnp.sum(du, axis=0, keepdims=True)
        for k in range(kw):
            dw_ref[k:k + 1, :] += jnp.sum(du * xs[pl.ds(HALO - (kw - 1) + k, t), :], axis=0, keepdims=True)

    row = pl.BlockSpec((t, c), lambda i: (i, 0))
    nxt = pl.BlockSpec((HALO, c), lambda i: (jnp.minimum((i + 1) * (t // HALO), s // HALO - 1), 0))
    return pl.pallas_call(
        body, grid=(nt,),
        in_specs=[row, nxt, row, nxt,
                  pl.BlockSpec((t, c), lambda i: (i, 1)),
                  _prev_halo_spec(t, c, lambda i: 1),
                  pl.BlockSpec((kw, c), lambda i: (0, 0))],
        out_specs=[row, pl.BlockSpec((kw, c), lambda i: (0, 0)), pl.BlockSpec((1, c), lambda i: (0, 0))],
        out_shape=[jax.ShapeDtypeStruct((s, c), BF16), jax.ShapeDtypeStruct((kw, c), F32),
                   jax.ShapeDtypeStruct((1, c), F32)],
        scratch_shapes=[pltpu.VMEM((t + HALO, c), F32), pltpu.VMEM((ext, c), F32)],
        compiler_params=_params(("arbitrary",)), name=name,
    )(drec_a, drec_a, drec_b, drec_b, proj, proj, cw)


N_CHIP = 4


def _chip_exchange(operands, name):
    nop = len(operands)
    sames = [same for _, same in operands]

    def body(*refs):
        srcs = refs[:nop]
        outs = refs[nop:2 * nop]
        send_sems, recv_sems, local_sems = refs[2 * nop:]
        x, y, c = lax.axis_index("x"), lax.axis_index("y"), lax.axis_index("c")
        mine = 2 * x + y
        local = []
        for o in range(nop):
            cp = pltpu.make_async_copy(srcs[o] if sames[o] else srcs[o].at[mine], outs[o].at[mine], local_sems.at[o])
            cp.start()
            local.append(cp)
        copies = []
        for k in range(1, N_CHIP):
            px = (x + (k >> 1)) % 2
            py = (y + (k & 1)) % 2
            chip = 2 * px + py
            for o in range(nop):
                sem = (k - 1) * nop + o
                src = srcs[o] if sames[o] else srcs[o].at[chip]
                send = pltpu.make_async_remote_copy(
                    src_ref=src, dst_ref=outs[o].at[mine], send_sem=send_sems.at[sem], recv_sem=recv_sems.at[sem],
                    device_id=(px, py, c), device_id_type=pl.DeviceIdType.MESH)
                send.start()
                recv = pltpu.make_async_remote_copy(
                    src_ref=src, dst_ref=outs[o].at[chip], send_sem=send_sems.at[sem], recv_sem=recv_sems.at[sem],
                    device_id=(px, py, c), device_id_type=pl.DeviceIdType.MESH)
                copies.append((send, recv))
        for send, recv in copies:
            send.wait_send()
            recv.wait_recv()
        for cp in local:
            cp.wait()

    anyspec = pl.BlockSpec(memory_space=pl.ANY)
    shapes = [jax.ShapeDtypeStruct((N_CHIP,) + (arr.shape if same else arr.shape[1:]), arr.dtype)
              for arr, same in operands]
    return pl.pallas_call(
        body, in_specs=[anyspec] * nop, out_specs=[anyspec] * nop, out_shape=shapes,
        scratch_shapes=[pltpu.SemaphoreType.DMA(((N_CHIP - 1) * nop,)), pltpu.SemaphoreType.DMA(((N_CHIP - 1) * nop,)),
                        pltpu.SemaphoreType.DMA((nop,))],
        name=name,
    )(*[arr for arr, _ in operands])


def _core_swap(operands, name):
    nop = len(operands)
    boths = [both for _, both in operands]

    def body(*refs):
        srcs = refs[:nop]
        outs = refs[nop:2 * nop]
        send_sems, recv_sems, local_sems = refs[2 * nop:]
        x, y, c = lax.axis_index("x"), lax.axis_index("y"), lax.axis_index("c")
        other = 1 - c
        local = []
        copies = []
        for o in range(nop):
            if boths[o]:
                cp = pltpu.make_async_copy(srcs[o], outs[o].at[c], local_sems.at[o])
                cp.start()
                local.append(cp)
                src, dst, landed = srcs[o], outs[o].at[c], outs[o].at[other]
            else:
                src, dst, landed = srcs[o].at[other], outs[o], outs[o]
            send = pltpu.make_async_remote_copy(
                src_ref=src, dst_ref=dst, send_sem=send_sems.at[o], recv_sem=recv_sems.at[o],
                device_id=(x, y, other), device_id_type=pl.DeviceIdType.MESH)
            send.start()
            recv = pltpu.make_async_remote_copy(
                src_ref=src, dst_ref=landed, send_sem=send_sems.at[o], recv_sem=recv_sems.at[o],
                device_id=(x, y, other), device_id_type=pl.DeviceIdType.MESH)
            copies.append((send, recv))
        for send, recv in copies:
            send.wait_send()
            recv.wait_recv()
        for cp in local:
            cp.wait()

    anyspec = pl.BlockSpec(memory_space=pl.ANY)
    shapes = [jax.ShapeDtypeStruct(((2,) + arr.shape) if both else arr.shape[1:], arr.dtype)
              for arr, both in operands]
    return pl.pallas_call(
        body, in_specs=[anyspec] * nop, out_specs=[anyspec] * nop, out_shape=shapes,
        scratch_shapes=[pltpu.SemaphoreType.DMA((nop,)), pltpu.SemaphoreType.DMA((nop,)),
                        pltpu.SemaphoreType.DMA((nop,))],
        name=name,
    )(*[arr for arr, _ in operands])


def _pair_sum(mine2, theirs, name):
    _, nq, n, cols = mine2.shape
    t = _tile(n, 512, 16)

    def body(a_ref, b_ref, o_ref):
        c = lax.axis_index("c")
        o_ref[...] = (a_ref[c].astype(F32) + b_ref[...].astype(F32)).astype(o_ref.dtype)

    return pl.pallas_call(
        body, grid=(nq, n // t),
        in_specs=[pl.BlockSpec((2, None, t, cols), lambda q, i: (0, q, i, 0)),
                  pl.BlockSpec((None, t, cols), lambda q, i: (q, i, 0))],
        out_specs=pl.BlockSpec((None, t, cols), lambda q, i: (q, i, 0)),
        out_shape=jax.ShapeDtypeStruct(theirs.shape, theirs.dtype),
        compiler_params=_params(("parallel", "parallel")), name=name,
    )(mine2, theirs)


def _core_sum(both, name):
    _, n, _ = both.shape
    t = _tile(n, 1024)

    def body(a_ref, o_ref):
        o_ref[...] = a_ref[0] + a_ref[1]

    return pl.pallas_call(
        body, grid=(n // t,),
        in_specs=[pl.BlockSpec((2, t, LANES), lambda i: (0, i, 0))],
        out_specs=pl.BlockSpec((t, LANES), lambda i: (i, 0)),
        out_shape=jax.ShapeDtypeStruct(both.shape[1:], both.dtype),
        compiler_params=_params(("parallel",)), name=name,
    )(both)


def _adam(parts, w, m, v, name):
    n, cols = w.shape
    nparts = parts.shape[0]
    t = _tile(n, 512, 16)
    c1 = 1.0 - ADAM_B1 ** ADAM_STEP
    c2 = 1.0 - ADAM_B2 ** ADAM_STEP

    def body(p_ref, w_ref, m_ref, v_ref, g_ref, d_ref, mo_ref, vo_ref):
        g = p_ref[0].astype(F32)
        for k in range(1, nparts):
            g = g + p_ref[k].astype(F32)
        mn = ADAM_B1 * m_ref[...] + (1.0 - ADAM_B1) * g
        vn = ADAM_B2 * v_ref[...] + (1.0 - ADAM_B2) * (g * g)
        m_hat = mn / c1
        v_hat = vn / c2
        g_ref[...] = g
        d_ref[...] = (-ADAM_LR) * (m_hat / (jnp.sqrt(v_hat) + ADAM_EPS) + ADAM_WD * w_ref[...])
        mo_ref[...] = mn
        vo_ref[...] = vn

    row = pl.BlockSpec((t, cols), lambda i: (i, 0))
    out = jax.ShapeDtypeStruct((n, cols), F32)
    return pl.pallas_call(
        body, grid=(n // t,),
        in_specs=[pl.BlockSpec((nparts, t, cols), lambda i: (0, i, 0)), row, row, row],
        out_specs=[row, row, row, row], out_shape=[out, out, out, out],
        compiler_params=_params(("parallel",)), name=name,
    )(parts, w, m, v)


def _pack(pieces, row_mult, lead=False):
    if lead:
        flat = jnp.concatenate([p.reshape(N_DEV, -1) for p in pieces], axis=1)
        n = flat.shape[1]
    else:
        flat = jnp.concatenate([p.reshape(-1) for p in pieces])
        n = flat.shape[0]
    group = row_mult * LANES
    total = -(-n // group) * group
    if lead:
        flat = jnp.pad(flat, ((0, 0), (0, total - n)))
        return flat.reshape(N_DEV, total // LANES, LANES)
    flat = jnp.pad(flat, (0, total - n))
    return flat.reshape(total // LANES, LANES)


def _unpack(buf, shapes, lead=False):
    out = []
    off = 0
    flat = buf.reshape(N_DEV, -1) if lead else buf.reshape(-1)
    for shp in shapes:
        size = math.prod(shp)
        if lead:
            out.append(flat[:, off:off + size].reshape((N_DEV,) + tuple(shp)))
        else:
            out.append(flat[off:off + size].reshape(shp))
        off += size
    return out


BIG = ["attn_w_qkv", "attn_w_o", "rg_w_in", "rg_w_out", "ffn_w_up", "ffn_w_down"]
SMALL = ["rg_conv_w", "rg_conv_b", "rg_b_a", "rg_b_x", "rg_lambda", "ffn_conv_w"]
REPL = ["rg_w_a", "rg_w_x", "ffn_conv_b", "mix_pre_g", "mix_post_g", "ffn_pre_g", "ffn_post_g"]
ORDER = ["attn_w_qkv", "attn_w_o", "rg_w_in", "rg_conv_w", "rg_conv_b", "rg_w_a", "rg_b_a", "rg_w_x", "rg_b_x",
         "rg_lambda", "rg_w_out", "ffn_w_up", "ffn_conv_w", "ffn_conv_b", "ffn_w_down", "mix_pre_g",
         "mix_post_g", "ffn_pre_g", "ffn_post_g"]
SHARD_AXIS = {"attn_w_qkv": 2, "attn_w_o": 1, "rg_w_in": 2, "rg_w_out": 1, "ffn_w_up": 2, "ffn_w_down": 1,
              "rg_conv_w": 2, "rg_conv_b": 1, "rg_b_a": 1, "rg_b_x": 1, "rg_lambda": 1, "ffn_conv_w": 2}
SMALL_ROWS = 64
REPL_ROWS = 128


def _assemble(stacked, axis):
    moved = jnp.moveaxis(stacked, 0, axis)
    shp = list(moved.shape)
    shp[axis:axis + 2] = [shp[axis] * shp[axis + 1]]
    return moved.reshape(shp)


def _split(full, axis):
    shp = list(full.shape)
    shp[axis:axis + 1] = [N_DEV, shp[axis] // N_DEV]
    return jnp.moveaxis(full.reshape(shp), axis, 0)


def _block_diag_pad(w):
    out = jnp.zeros((D_RNN_PAD, D_RNN_PAD), w.dtype)
    for n in range(RG_BLOCKS):
        o = n * RG_BLOCK_W
        out = lax.dynamic_update_slice(out, w[n], (o, o))
    return out


def _block_diag_take(dense):
    return jnp.stack([dense[n * RG_BLOCK_W:(n + 1) * RG_BLOCK_W, n * RG_BLOCK_W:(n + 1) * RG_BLOCK_W]
                      for n in range(RG_BLOCKS)])


def _pad_cols(a, width):
    return jnp.pad(a, ((0, 0), (0, width - a.shape[1])))


def _interleave(a):
    lead = a.shape[:-1]
    nb = D_FF // FFN_COL
    return jnp.swapaxes(a.reshape(lead + (2, nb, FFN_COL)), -3, -2).reshape(lead + (2 * D_FF,))


def _deinterleave(a):
    lead = a.shape[:-1]
    nb = D_FF // FFN_COL
    return jnp.swapaxes(a.reshape(lead + (nb, 2, FFN_COL)), -3, -2).reshape(lead + (2 * D_FF,))


def _ffn_fwd(hn, wp, i):
    pre = _matmul(hn, wp["up"][i], name=f"ffn{i}_up")
    act = _ffn_act_fwd(pre, wp["ffn_cw"][i], wp["ffn_cb"][i], name=f"ffn{i}_act")
    f = _matmul(act, wp["down"][i], name=f"ffn{i}_down")
    return pre, act, f


def _ffn_bwd(df, hn, pre, act, wp, i):
    dact = _matmul(df, wp["down"][i], tb=True, name=f"ffn{i}_dact")
    d_down = _matmul(act, df, ta=True, name=f"ffn{i}_ddown")
    dpre, dcw, dcb = _ffn_act_bwd(pre, dact, wp["ffn_cw"][i], wp["ffn_cb"][i], name=f"ffn{i}_act_bwd")
    d_up = _matmul(hn, dpre, ta=True, name=f"ffn{i}_dup")
    dhn = _matmul(dpre, wp["up"][i], tb=True, name=f"ffn{i}_dhn")
    return dhn, d_up, d_down, dcw, dcb


def _local_step(x, tgt, wp):
    g = wp["gains"]
    gr = {}

    hn0 = _rms_fwd(x, g["mix_pre"][0], name="l0_mix_pre")
    qkv = _matmul(hn0, wp["qkv"], out_dtype=BF16, name="attn_qkv")
    o, lsum, first = _attn_fwd(qkv, name="attn_fwd")
    f0 = _matmul(o, wp["wo"], name="attn_out")
    x1, hn1 = _resid_norm(x, f0, g["mix_post"][0], g["ffn_pre"][0], name="l0_mix_post")
    pre0, act0, f1 = _ffn_fwd(hn1, wp, 0)
    x2, hn2 = _resid_norm(x1, f1, g["ffn_post"][0], g["mix_pre"][1], name="l0_ffn_post")
    proj = _matmul(hn2, wp["w_in"], name="rg_in")
    rec, rec_b = _rg_conv_fwd(proj, wp["rg_cw"], wp["rg_cb"], name="rg_conv")
    pre_ax = _matmul(rec_b, wp["w_ax"], name="rg_gates")
    h, gh = _rg_scan_fwd(pre_ax, rec, proj, wp["rg_ba"], wp["rg_bx"], wp["rg_lam"], name="rg_scan")
    f2 = _matmul(gh, wp["w_out"], name="rg_out")
    x3, hn3 = _resid_norm(x2, f2, g["mix_post"][1], g["ffn_pre"][1], name="l1_mix_post")
    pre1, act1, f3 = _ffn_fwd(hn3, wp, 1)
    dy, sq = _resid_loss(x3, f3, g["ffn_post"][1], tgt, name="l1_ffn_post_loss")

    ffn_dup, ffn_ddown, ffn_dcw, ffn_dcb = [None, None], [None, None], [None, None], [None, None]
    d_ffn_post, d_ffn_pre, d_mix_post, d_mix_pre = [None, None], [None, None], [None, None], [None, None]

    df3, d_ffn_post[1] = _rms_bwd(f3, g["ffn_post"][1], dy, None, BF16, name="l1_ffn_post_bwd")
    dhn3, ffn_dup[1], ffn_ddown[1], ffn_dcw[1], ffn_dcb[1] = _ffn_bwd(df3, hn3, pre1, act1, wp, 1)
    dx3, d_ffn_pre[1] = _rms_bwd(x3, g["ffn_pre"][1], dhn3, dy, F32, name="l1_ffn_pre_bwd")

    df2, d_mix_post[1] = _rms_bwd(f2, g["mix_post"][1], dx3, None, BF16, name="l1_mix_post_bwd")
    dgh = _matmul(df2, wp["w_out"], tb=True, name="rg_dgh")
    gr["w_out"] = _matmul(gh, df2, ta=True, name="rg_dwout")
    dgate, dpax, drec_u, gr["rg_ba"], gr["rg_bx"], gr["rg_lam"] = _rg_scan_bwd(
        dgh, proj, h, pre_ax, rec, wp["rg_ba"], wp["rg_bx"], wp["rg_lam"], name="rg_scan_bwd")
    drec_g = _matmul(dpax, wp["w_ax"], tb=True, name="rg_drec")
    gr["w_ax"] = _matmul(rec_b, dpax, ta=True, name="rg_dwax")
    dproj_rec, gr["rg_cw"], gr["rg_cb"] = _rg_conv_bwd(drec_u, drec_g, proj, wp["rg_cw"], name="rg_conv_bwd")
    dproj = jnp.concatenate([dgate, dproj_rec], axis=1)
    gr["w_in"] = _matmul(hn2, dproj, ta=True, name="rg_dwin")
    dhn2 = _matmul(dproj, wp["w_in"], tb=True, name="rg_dhn")
    dx2, d_mix_pre[1] = _rms_bwd(x2, g["mix_pre"][1], dhn2, dx3, F32, name="l1_mix_pre_bwd")

    df1, d_ffn_post[0] = _rms_bwd(f1, g["ffn_post"][0], dx2, None, BF16, name="l0_ffn_post_bwd")
    dhn1, ffn_dup[0], ffn_ddown[0], ffn_dcw[0], ffn_dcb[0] = _ffn_bwd(df1, hn1, pre0, act0, wp, 0)
    dx1, d_ffn_pre[0] = _rms_bwd(x1, g["ffn_pre"][0], dhn1, dx2, F32, name="l0_ffn_pre_bwd")

    df0, d_mix_post[0] = _rms_bwd(f0, g["mix_post"][0], dx1, None, BF16, name="l0_mix_post_bwd")
    do = _matmul(df0, wp["wo"], tb=True, out_dtype=BF16, name="attn_do")
    gr["wo"] = _matmul(o, df0, ta=True, name="attn_dwo")
    dq, dk, dv = _attn_bwd(qkv, do, lsum, first, name="attn_bwd")
    dqkv = jnp.concatenate([dq, dk.astype(BF16), dv.astype(BF16)], axis=1)
    gr["qkv"] = _matmul(hn0, dqkv, ta=True, name="attn_dwqkv")
    dhn0 = _matmul(dqkv, wp["qkv"], tb=True, name="attn_dhn")
    dx0, d_mix_pre[0] = _rms_bwd(x, g["mix_pre"][0], dhn0, dx1, F32, name="l0_mix_pre_bwd")

    gr["up"] = ffn_dup
    gr["down"] = ffn_ddown
    gr["ffn_cw"] = ffn_dcw
    gr["ffn_cb"] = ffn_dcb
    gr["gains"] = {"mix_pre": d_mix_pre, "mix_post": d_mix_post, "ffn_pre": d_ffn_pre, "ffn_post": d_ffn_post}
    return sq[0, 0], dx0, gr


def _prepare(full):
    c, cp = D_RNN, D_RNN_PAD
    w_in = full["rg_w_in"][0]
    w_a = _block_diag_pad(full["rg_w_a"][0].astype(BF16))
    w_x = _block_diag_pad(full["rg_w_x"][0].astype(BF16))
    vec = lambda a: _pad_cols(a.reshape(1, c), cp)
    return {
        "qkv": full["attn_w_qkv"][0],
        "wo": full["attn_w_o"][0],
        "w_in": jnp.concatenate([_pad_cols(w_in[:, :c], cp), _pad_cols(w_in[:, c:], cp)], axis=1),
        "w_ax": jnp.concatenate([w_a, w_x], axis=1),
        "w_out": jnp.pad(full["rg_w_out"][0], ((0, cp - c), (0, 0))),
        "rg_cw": _pad_cols(full["rg_conv_w"][0], cp),
        "rg_cb": vec(full["rg_conv_b"][0]),
        "rg_ba": vec(full["rg_b_a"][0]),
        "rg_bx": vec(full["rg_b_x"][0]),
        "rg_lam": vec(full["rg_lambda"][0]),
        "up": [_interleave(full["ffn_w_up"][i]) for i in range(2)],
        "down": [full["ffn_w_down"][i] for i in range(2)],
        "ffn_cw": [_interleave(full["ffn_conv_w"][i]) for i in range(2)],
        "ffn_cb": [_interleave(full["ffn_conv_b"][i].reshape(1, -1)) for i in range(2)],
        "gains": {k: [full[k + "_g"][i].reshape(1, D_MODEL) for i in range(2)]
                  for k in ("mix_pre", "mix_post", "ffn_pre", "ffn_post")},
    }


def _natural_grads(gr):
    c, cp = D_RNN, D_RNN_PAD
    gg = gr["gains"]
    stack2 = lambda pair: jnp.stack([pair[0].reshape(-1), pair[1].reshape(-1)])
    return {
        "attn_w_qkv": gr["qkv"][None],
        "attn_w_o": gr["wo"][None],
        "rg_w_in": jnp.concatenate([gr["w_in"][:, :c], gr["w_in"][:, cp:cp + c]], axis=1)[None],
        "rg_conv_w": gr["rg_cw"][:, :c][None],
        "rg_conv_b": gr["rg_cb"][:, :c],
        "rg_w_a": _block_diag_take(gr["w_ax"][:, :cp])[None],
        "rg_b_a": gr["rg_ba"][:, :c],
        "rg_w_x": _block_diag_take(gr["w_ax"][:, cp:])[None],
        "rg_b_x": gr["rg_bx"][:, :c],
        "rg_lambda": gr["rg_lam"][:, :c],
        "rg_w_out": gr["w_out"][:c][None],
        "ffn_w_up": jnp.stack([_deinterleave(gr["up"][i]) for i in range(2)]),
        "ffn_conv_w": jnp.stack([_deinterleave(gr["ffn_cw"][i]) for i in range(2)]),
        "ffn_conv_b": jnp.stack([_deinterleave(gr["ffn_cb"][i]).reshape(-1) for i in range(2)]),
        "ffn_w_down": jnp.stack(gr["down"]),
        "mix_pre_g": stack2(gg["mix_pre"]),
        "mix_post_g": stack2(gg["mix_post"]),
        "ffn_pre_g": stack2(gg["ffn_pre"]),
        "ffn_post_g": stack2(gg["ffn_post"]),
    }


def _rows(a):
    return a.reshape(-1, a.shape[-1])


def _step(x, loss_target, w, m, v):
    small_shapes = [w[n].shape for n in SMALL]
    repl_shapes = [w[n].shape for n in REPL]
    wsmall = _pack([w[n] for n in SMALL], SMALL_ROWS)
    on_chips = _chip_exchange([(_rows(w[n].astype(BF16)), True) for n in BIG] + [(wsmall, True)],
                              name="gather_weights_chips")
    on_cores = _core_swap([(a, True) for a in on_chips], name="gather_weights_cores")
    by_device = lambda a: jnp.swapaxes(a, 0, 1).reshape((N_DEV,) + a.shape[2:])
    full = {}
    for n, a in zip(BIG, on_cores):
        full[n] = _assemble(by_device(a).reshape((N_DEV,) + w[n].shape), SHARD_AXIS[n])
    for n, st in zip(SMALL, _unpack(by_device(on_cores[-1]), small_shapes, lead=True)):
        full[n] = _assemble(st, SHARD_AXIS[n])
    for n in REPL:
        full[n] = w[n]

    sq, dx, gr = _local_step(x[0], loss_target[0], _prepare(full))
    grads = _natural_grads(gr)

    by_core = lambda a: jnp.swapaxes(a.reshape((N_CHIP, 2) + a.shape[1:]), 0, 1)
    pbig = [by_core(_split(grads[n], SHARD_AXIS[n]).astype(BF16).reshape((N_DEV,) + _rows(w[n]).shape)) for n in BIG]
    psmall = by_core(_pack([_split(grads[n], SHARD_AXIS[n]) for n in SMALL], SMALL_ROWS, lead=True))
    prepl = _pack([grads[n] for n in REPL], REPL_ROWS)
    swapped = _core_swap([(p, False) for p in pbig] + [(psmall, False), (prepl, True)], name="reduce_grads_cores")
    sums = [_pair_sum(p, o, name="chip_sum_" + n) for n, p, o in zip(BIG, pbig, swapped)]
    sums.append(_pair_sum(psmall, swapped[-2], name="chip_sum_small"))
    trepl = _core_sum(swapped[-1], name="chip_sum_repl")
    reduced = _chip_exchange([(t, False) for t in sums] + [(trepl, True)], name="reduce_grads_chips")

    out = {}
    kinds = ("grad", "delta", "new_m", "new_v")
    for n, parts in zip(BIG, reduced):
        res = _adam(parts, _rows(w[n]), _rows(m[n]), _rows(v[n]), name="adamw_" + n)
        for kind, buf in zip(kinds, res):
            out[(kind, n)] = buf.reshape(w[n].shape)
    for names, parts, rows, shapes, tag in ((SMALL, reduced[-2], SMALL_ROWS, small_shapes, "small"),
                                            (REPL, reduced[-1], REPL_ROWS, repl_shapes, "repl")):
        packs = [_pack([d[n] for n in names], rows) for d in (w, m, v)]
        res = _adam(parts, *packs, name="adamw_" + tag)
        for kind, buf in zip(kinds, res):
            for n, a in zip(names, _unpack(buf, shapes)):
                out[(kind, n)] = a

    loss = lax.psum(sq * (0.5 / D_MODEL), ("x", "y", "c"))
    return loss, dx[None], out


def kernel(x, attn_w_qkv, attn_w_o, rg_w_in, rg_conv_w, rg_conv_b, rg_w_a, rg_b_a, rg_w_x, rg_b_x, rg_lambda, rg_w_out, ffn_w_up, ffn_conv_w, ffn_conv_b, ffn_w_down, mix_pre_g, mix_post_g, ffn_pre_g, ffn_post_g, loss_target, m_attn_w_qkv, m_attn_w_o, m_rg_w_in, m_rg_conv_w, m_rg_conv_b, m_rg_w_a, m_rg_b_a, m_rg_w_x, m_rg_b_x, m_rg_lambda, m_rg_w_out, m_ffn_w_up, m_ffn_conv_w, m_ffn_conv_b, m_ffn_w_down, m_mix_pre_g, m_mix_post_g, m_ffn_pre_g, m_ffn_post_g, v_attn_w_qkv, v_attn_w_o, v_rg_w_in, v_rg_conv_w, v_rg_conv_b, v_rg_w_a, v_rg_b_a, v_rg_w_x, v_rg_b_x, v_rg_lambda, v_rg_w_out, v_ffn_w_up, v_ffn_conv_w, v_ffn_conv_b, v_ffn_w_down, v_mix_pre_g, v_mix_post_g, v_ffn_pre_g, v_ffn_post_g):
    w = dict(zip(ORDER, (attn_w_qkv, attn_w_o, rg_w_in, rg_conv_w, rg_conv_b, rg_w_a, rg_b_a, rg_w_x, rg_b_x,
                         rg_lambda, rg_w_out, ffn_w_up, ffn_conv_w, ffn_conv_b, ffn_w_down, mix_pre_g,
                         mix_post_g, ffn_pre_g, ffn_post_g)))
    m = dict(zip(ORDER, (m_attn_w_qkv, m_attn_w_o, m_rg_w_in, m_rg_conv_w, m_rg_conv_b, m_rg_w_a, m_rg_b_a,
                         m_rg_w_x, m_rg_b_x, m_rg_lambda, m_rg_w_out, m_ffn_w_up, m_ffn_conv_w, m_ffn_conv_b,
                         m_ffn_w_down, m_mix_pre_g, m_mix_post_g, m_ffn_pre_g, m_ffn_post_g)))
    v = dict(zip(ORDER, (v_attn_w_qkv, v_attn_w_o, v_rg_w_in, v_rg_conv_w, v_rg_conv_b, v_rg_w_a, v_rg_b_a,
                         v_rg_w_x, v_rg_b_x, v_rg_lambda, v_rg_w_out, v_ffn_w_up, v_ffn_conv_w, v_ffn_conv_b,
                         v_ffn_w_down, v_mix_pre_g, v_mix_post_g, v_ffn_pre_g, v_ffn_post_g)))
    loss, dx, out = _step(x, loss_target, w, m, v)
    return (loss, dx, *[out[("grad", n)] for n in ORDER], *[out[("delta", n)] for n in ORDER],
            *[out[("new_m", n)] for n in ORDER], *[out[("new_v", n)] for n in ORDER])
```

```python
import functools
import math

import jax
import jax.numpy as jnp
from jax import lax
from jax.experimental import pallas as pl
from jax.experimental.pallas import tpu as pltpu

F32 = jnp.float32
BF16 = jnp.bfloat16

D_MODEL = 1024
N_HEADS = 16
HEAD_DIM = 64
D_RNN = 1344
D_RNN_PAD = 1408
RG_BLOCKS = 16
RG_BLOCK_W = 84
RG_CONV_W = 4
RG_C = 8.0
D_FF = 2816
FFN_CONV_W = 3
FFN_COL = 256
NORM_EPS = 1e-6
N_DEV = 8
LANES = 128
HALO = 8
STRIP = 32

ADAM_LR = 0.001
ADAM_B1 = 0.9
ADAM_B2 = 0.999
ADAM_EPS = 1e-08
ADAM_WD = 0.01
ADAM_STEP = 10

ATT_BLK = 256
ATT_HEADS = 4
ATT_STOP_LOG = -110.0
VMEM_LIMIT = 56 * 1024 * 1024

_GELU_C = math.sqrt(2.0 / math.pi)


def _params(sem=None):
    return pltpu.CompilerParams(dimension_semantics=sem, vmem_limit_bytes=VMEM_LIMIT)


def _tile(n, cap, mult=8):
    if n <= cap:
        return n
    best = None
    for t in range(mult, cap + 1, mult):
        if n % t == 0:
            best = t
    assert best is not None, (n, cap, mult)
    return best


def _gelu_and_grad(x):
    x2 = x * x
    th = jnp.tanh(_GELU_C * (x + 0.044715 * (x2 * x)))
    cdf = 0.5 * (1.0 + th)
    gel = x * cdf
    dgel = cdf + 0.5 * x * (1.0 - th * th) * (_GELU_C * (1.0 + 3.0 * 0.044715 * x2))
    return gel, dgel


def _gelu(x):
    th = jnp.tanh(_GELU_C * (x + 0.044715 * (x * x * x)))
    return x * (0.5 * (1.0 + th))


def _matmul(a, b, *, ta=False, tb=False, out_dtype=F32, name):
    if ta:
        kdim, m = a.shape
    else:
        m, kdim = a.shape
    if tb:
        n, kb_ = b.shape
    else:
        kb_, n = b.shape
    assert kdim == kb_, (a.shape, b.shape, ta, tb)
    wide = lambda d: 1408 if d % 1408 == 0 else 1024
    tm = _tile(m, wide(m), 128) if ta else _tile(m, 512)
    tn = _tile(n, wide(n), 128)
    tk = _tile(kdim, 512, 128) if ta else _tile(kdim, wide(kdim), 128)
    nk = kdim // tk
    dims = (((0 if ta else 1,), (1 if tb else 0,)), ((), ()))

    def body(a_ref, b_ref, o_ref, acc_ref):
        part = lax.dot_general(a_ref[...], b_ref[...], dims, preferred_element_type=F32)
        if nk == 1:
            o_ref[...] = part.astype(out_dtype)
        else:
            k = pl.program_id(2)

            @pl.when(k == 0)
            def _():
                acc_ref[...] = part

            @pl.when(k > 0)
            def _():
                acc_ref[...] += part

            @pl.when(k == nk - 1)
            def _():
                o_ref[...] = acc_ref[...].astype(out_dtype)

    a_spec = pl.BlockSpec((tk, tm), lambda i, j, k: (k, i)) if ta else pl.BlockSpec((tm, tk), lambda i, j, k: (i, k))
    b_spec = pl.BlockSpec((tn, tk), lambda i, j, k: (j, k)) if tb else pl.BlockSpec((tk, tn), lambda i, j, k: (k, j))
    return pl.pallas_call(
        body,
        grid=(m // tm, n // tn, nk),
        in_specs=[a_spec, b_spec],
        out_specs=pl.BlockSpec((tm, tn), lambda i, j, k: (i, j)),
        out_shape=jax.ShapeDtypeStruct((m, n), out_dtype),
        scratch_shapes=[pltpu.VMEM((tm, tn), F32)],
        compiler_params=_params(("parallel", "parallel", "arbitrary")),
        name=name,
    )(a, b)


def _rms(xv, g):
    inv = lax.rsqrt(jnp.mean(xv * xv, axis=-1, keepdims=True) + NORM_EPS)
    return xv * inv * g


def _rms_fwd(x, g, name):
    s, d = x.shape
    t = _tile(s, 512)

    def body(x_ref, g_ref, o_ref):
        o_ref[...] = _rms(x_ref[...], g_ref[...]).astype(BF16)

    return pl.pallas_call(
        body, grid=(s // t,),
        in_specs=[pl.BlockSpec((t, d), lambda i: (i, 0)), pl.BlockSpec((1, d), lambda i: (0, 0))],
        out_specs=pl.BlockSpec((t, d), lambda i: (i, 0)),
        out_shape=jax.ShapeDtypeStruct((s, d), BF16),
        compiler_params=_params(("parallel",)), name=name,
    )(x, g)


def _resid_norm(x, f, post_g, pre_g, name):
    s, d = x.shape
    t = _tile(s, 512)

    def body(x_ref, f_ref, pg_ref, ng_ref, xo_ref, hn_ref):
        xn = x_ref[...] + _rms(f_ref[...], pg_ref[...])
        xo_ref[...] = xn
        hn_ref[...] = _rms(xn, ng_ref[...]).astype(BF16)

    row = pl.BlockSpec((t, d), lambda i: (i, 0))
    vec = pl.BlockSpec((1, d), lambda i: (0, 0))
    return pl.pallas_call(
        body, grid=(s // t,), in_specs=[row, row, vec, vec], out_specs=[row, row],
        out_shape=[jax.ShapeDtypeStruct((s, d), F32), jax.ShapeDtypeStruct((s, d), BF16)],
        compiler_params=_params(("parallel",)), name=name,
    )(x, f, post_g, pre_g)


def _resid_loss(x, f, post_g, tgt, name):
    s, d = x.shape
    t = _tile(s, 512)

    def body(x_ref, f_ref, pg_ref, t_ref, dy_ref, loss_ref):
        @pl.when(pl.program_id(0) == 0)
        def _():
            loss_ref[...] = jnp.zeros_like(loss_ref)

        err = x_ref[...] + _rms(f_ref[...], pg_ref[...]) - t_ref[...]
        dy_ref[...] = err * (1.0 / d)
        sq = jnp.sum(jnp.sum(err * err, axis=1, keepdims=True), axis=0, keepdims=True)
        loss_ref[...] += sq

    row = pl.BlockSpec((t, d), lambda i: (i, 0))
    vec = pl.BlockSpec((1, d), lambda i: (0, 0))
    return pl.pallas_call(
        body, grid=(s // t,), in_specs=[row, row, vec, row],
        out_specs=[row, pl.BlockSpec((8, LANES), lambda i: (0, 0))],
        out_shape=[jax.ShapeDtypeStruct((s, d), F32), jax.ShapeDtypeStruct((8, LANES), F32)],
        compiler_params=_params(("arbitrary",)), name=name,
    )(x, f, post_g, tgt)


def _rms_bwd(xin, g, dy, add, out_dtype, name):
    s, d = xin.shape
    t = _tile(s, 512)
    has_add = add is not None

    def body(*refs):
        if has_add:
            x_ref, g_ref, dy_ref, add_ref, dx_ref, dg_ref = refs
        else:
            x_ref, g_ref, dy_ref, dx_ref, dg_ref = refs

        @pl.when(pl.program_id(0) == 0)
        def _():
            dg_ref[...] = jnp.zeros_like(dg_ref)

        xv = x_ref[...]
        dyv = dy_ref[...].astype(F32)
        inv = lax.rsqrt(jnp.mean(xv * xv, axis=-1, keepdims=True) + NORM_EPS)
        xh = xv * inv
        gy = dyv * g_ref[...]
        dx = inv * (gy - xh * jnp.mean(xh * gy, axis=-1, keepdims=True))
        if has_add:
            dx = dx + add_ref[...]
        dx_ref[...] = dx.astype(out_dtype)
        dg_ref[...] += jnp.sum(dyv * xh, axis=0, keepdims=True)

    row = pl.BlockSpec((t, d), lambda i: (i, 0))
    vec = pl.BlockSpec((1, d), lambda i: (0, 0))
    args = [xin, g, dy] + ([add] if has_add else [])
    specs = [row, vec, row] + ([row] if has_add else [])
    return pl.pallas_call(
        body, grid=(s // t,), in_specs=specs, out_specs=[row, vec],
        out_shape=[jax.ShapeDtypeStruct((s, d), out_dtype), jax.ShapeDtypeStruct((1, d), F32)],
        compiler_params=_params(("arbitrary",)), name=name,
    )(*args)


_NT = (((1,), (1,)), ((), ()))
_NN = (((1,), (0,)), ((), ()))
_TN = (((0,), (0,)), ((), ()))


def _tri_dot(x, tri):
    hi = x.astype(BF16)
    lo = (x - hi.astype(F32)).astype(BF16)
    out = lax.dot_general(hi, tri, _NN, preferred_element_type=F32)
    return out + lax.dot_general(lo, tri, _NN, preferred_element_type=F32)


def _log_sigmoids(z):
    lb = jnp.minimum(z, 0.0) - jnp.log(1.0 + jnp.exp(-jnp.abs(z)))
    return lb, lb - z


def _att_blk(s):
    return min(ATT_BLK, s)


def _attn_fwd(qkv, name):
    s = qkv.shape[0]
    blk = _att_blk(s)
    nq = s // blk
    nh = ATT_HEADS
    width = nh * HEAD_DIM
    ngrp = N_HEADS // nh
    hs = range(nh)
    scale = HEAD_DIM ** -0.5

    def body(q_ref, k_ref, v_ref, o_ref, l_ref, first_ref):
        qi = pl.program_id(1)
        row = lax.broadcasted_iota(jnp.int32, (blk, blk), 0)
        col = lax.broadcasted_iota(jnp.int32, (blk, blk), 1)
        causal = col < row
        tri = (row > col).astype(BF16)
        lane_head = lax.broadcasted_iota(jnp.int32, (blk, width), 1) // HEAD_DIM
        masks = [lane_head == hh for hh in hs]
        qs = q_ref[...] * scale
        qh = [jnp.where(m, qs, jnp.zeros_like(qs)) for m in masks]

        def block(kb, carry, diag):
            rsums, acc = carry
            r0 = pl.multiple_of(kb * blk, blk)
            k2 = k_ref[pl.ds(r0, blk), :]
            v2 = v_ref[pl.ds(r0, blk), :]
            z = [lax.dot_general(qh[hh], k2, _NT, preferred_element_type=F32) for hh in hs]
            lbm = [_log_sigmoids(z[hh]) for hh in hs]
            lb = [p[0] for p in lbm]
            lm = [jnp.where(causal, p[1], 0.0) if diag else p[1] for p in lbm]
            cs = [_tri_dot(lm[hh], tri) for hh in hs]
            w = [jnp.exp(lb[hh] + cs[hh] + rsums[hh]) for hh in hs]
            if diag:
                w = [jnp.where(causal, w[hh], 0.0) for hh in hs]
            for hh in hs:
                vh = jnp.where(masks[hh], v2, jnp.zeros_like(v2))
                acc = acc + lax.dot_general(w[hh].astype(BF16), vh, _NN, preferred_element_type=F32)
            return tuple(rsums[hh] + (cs[hh][:, 0:1] + lm[hh][:, 0:1]) for hh in hs), acc

        def largest(rsums):
            m = jnp.max(rsums[0])
            for hh in range(1, nh):
                m = jnp.maximum(m, jnp.max(rsums[hh]))
            return m

        def more(c):
            return jnp.logical_and(c[0] < qi, c[1] > ATT_STOP_LOG)

        def step(c):
            rsums, acc = block(qi - 1 - c[0], (c[2], c[3]), False)
            return c[0] + 1, largest(rsums), rsums, acc

        zero = jnp.zeros((blk, 1), F32)
        rsums, acc = block(qi, ((zero,) * nh, jnp.zeros((blk, width), F32)), True)
        done, _, rsums, acc = lax.while_loop(more, step, (jnp.int32(0), largest(rsums), rsums, acc))
        first_ref[pl.program_id(0) * nq + qi] = (qi - done).astype(F32)
        o_ref[...] = acc.astype(BF16)
        lout = rsums[nh - 1]
        for hh in range(nh - 2, -1, -1):
            lout = jnp.where(masks[hh], rsums[hh], lout)
        l_ref[...] = lout

    return pl.pallas_call(
        body, grid=(ngrp, nq),
        in_specs=[pl.BlockSpec((blk, width), lambda h, i: (i, h)),
                  pl.BlockSpec((s, width), lambda h, i: (0, ngrp + h)),
                  pl.BlockSpec((s, width), lambda h, i: (0, 2 * ngrp + h))],
        out_specs=[pl.BlockSpec((blk, width), lambda h, i: (i, h)), pl.BlockSpec((blk, width), lambda h, i: (i, h)),
                   pl.BlockSpec(memory_space=pltpu.SMEM)],
        out_shape=[jax.ShapeDtypeStruct((s, D_MODEL), BF16), jax.ShapeDtypeStruct((s, D_MODEL), F32),
                   jax.ShapeDtypeStruct((ngrp * nq,), F32)],
        compiler_params=_params(("arbitrary", "arbitrary")), name=name,
    )(qkv, qkv, qkv)


def _attn_bwd(qkv, do, lsum, first, name):
    s = qkv.shape[0]
    blk = _att_blk(s)
    nq = s // blk
    nh = ATT_HEADS
    width = nh * HEAD_DIM
    ngrp = N_HEADS // nh
    hs = range(nh)
    scale = HEAD_DIM ** -0.5

    def body(first_ref, q_ref, k_ref, v_ref, do_ref, l_ref, dq_ref, dk_ref, dv_ref):
        qi = pl.program_id(1)
        start = first_ref[pl.program_id(0) * nq + qi].astype(jnp.int32)

        @pl.when(qi == 0)
        def _():
            dk_ref[...] = jnp.zeros_like(dk_ref)
            dv_ref[...] = jnp.zeros_like(dv_ref)

        row = lax.broadcasted_iota(jnp.int32, (blk, blk), 0)
        col = lax.broadcasted_iota(jnp.int32, (blk, blk), 1)
        causal = col < row
        tri_in = (row <= col).astype(BF16)
        tri_ex = (row < col).astype(BF16)
        lane_head = lax.broadcasted_iota(jnp.int32, (blk, width), 1) // HEAD_DIM
        masks = [lane_head == hh for hh in hs]
        q2 = q_ref[...]
        qs = q2 * scale
        do2 = do_ref[...]
        l2 = l_ref[...]
        qh = [jnp.where(m, q2, jnp.zeros_like(q2)) for m in masks]
        qsh = [jnp.where(m, qs, jnp.zeros_like(qs)) for m in masks]
        doh = [jnp.where(m, do2, jnp.zeros_like(do2)) for m in masks]
        ltot = [l2[:, hh * HEAD_DIM:hh * HEAD_DIM + 1] for hh in hs]
        last = slice(blk - 1, blk)

        def block(kb, carry, diag):
            r0 = pl.multiple_of(kb * blk, blk)
            k2 = k_ref[pl.ds(r0, blk), :]
            v2 = v_ref[pl.ds(r0, blk), :]
            dq, pre_lm, pre_da = carry
            mask = (lambda a: jnp.where(causal, a, 0.0)) if diag else (lambda a: a)
            z = [lax.dot_general(qsh[hh], k2, _NT, preferred_element_type=F32) for hh in hs]
            dw = [lax.dot_general(doh[hh], v2, _NT, preferred_element_type=F32) for hh in hs]
            lbm = [_log_sigmoids(z[hh]) for hh in hs]
            lb = [p[0] for p in lbm]
            lm = [mask(p[1]) for p in lbm]
            pin = [_tri_dot(lm[hh], tri_in) for hh in hs]
            w = [mask(jnp.exp(lb[hh] + (ltot[hh] - pre_lm[hh]) - pin[hh])) for hh in hs]
            da = [w[hh] * dw[hh] for hh in hs]
            cex = [_tri_dot(da[hh], tri_ex) for hh in hs]
            dzb = [(mask(da[hh] - (da[hh] + (cex[hh] + pre_da[hh])) * jnp.exp(lb[hh])) * scale).astype(BF16)
                   for hh in hs]
            dk_blk = jnp.zeros((blk, width), F32)
            dv_blk = jnp.zeros((blk, width), F32)
            for hh in hs:
                kh = jnp.where(masks[hh], k2, jnp.zeros_like(k2))
                dq = dq + lax.dot_general(dzb[hh], kh, _NN, preferred_element_type=F32)
                dk_blk = dk_blk + lax.dot_general(dzb[hh], qh[hh], _TN, preferred_element_type=F32)
                dv_blk = dv_blk + lax.dot_general(w[hh].astype(BF16), doh[hh], _TN, preferred_element_type=F32)
            dk_ref[pl.ds(r0, blk), :] += dk_blk
            dv_ref[pl.ds(r0, blk), :] += dv_blk
            return (dq, tuple(pre_lm[hh] + pin[hh][:, last] for hh in hs),
                    tuple(pre_da[hh] + (cex[hh][:, last] + da[hh][:, last]) for hh in hs))

        zero = jnp.zeros((blk, 1), F32)
        init = (jnp.zeros((blk, width), F32), (zero,) * nh, (zero,) * nh)
        carry = lax.fori_loop(start, qi, lambda i, c: block(i, c, False), init)
        dq, _, _ = block(qi, carry, True)
        dq_ref[...] = dq.astype(BF16)

    qblk = pl.BlockSpec((blk, width), lambda h, i: (i, h))
    once = pl.Buffered(1)
    full = pl.BlockSpec((s, width), lambda h, i: (0, h), pipeline_mode=once)
    return pl.pallas_call(
        body, grid=(ngrp, nq),
        in_specs=[pl.BlockSpec(memory_space=pltpu.SMEM), qblk,
                  pl.BlockSpec((s, width), lambda h, i: (0, ngrp + h), pipeline_mode=once),
                  pl.BlockSpec((s, width), lambda h, i: (0, 2 * ngrp + h), pipeline_mode=once),
                  qblk, qblk],
        out_specs=[qblk, full, full],
        out_shape=[jax.ShapeDtypeStruct((s, D_MODEL), BF16), jax.ShapeDtypeStruct((s, D_MODEL), F32),
                   jax.ShapeDtypeStruct((s, D_MODEL), F32)],
        compiler_params=_params(("arbitrary", "arbitrary")), name=name,
    )(first, qkv, qkv, qkv, do, lsum)


def _prev_halo_spec(t, c, col):
    return pl.BlockSpec((HALO, c), lambda *g: (jnp.maximum(g[0] * (t // HALO) - 1, 0), col(*g)))


def _conv_taps(scr, w_ref, b_ref, kw, rows, lead):
    out = b_ref[...] + w_ref[kw - 1:kw, :] * scr[pl.ds(lead, rows), :]
    for k in range(kw - 1):
        out = out + w_ref[k:k + 1, :] * scr[pl.ds(lead - (kw - 1) + k, rows), :]
    return out


def _ffn_act_fwd(pre, cw, cb, name):
    s, c2 = pre.shape
    t = _tile(s, 512)
    cb2 = 2 * FFN_COL
    ncol = c2 // cb2
    kw = FFN_CONV_W

    def body(x_ref, halo_ref, w_ref, b_ref, o_ref, scr):
        i = pl.program_id(0)
        scr[0:HALO, :] = jnp.where(i > 0, halo_ref[...], 0.0)
        scr[HALO:HALO + t, :] = x_ref[...]
        w = [w_ref[k:k + 1, :] for k in range(kw)]
        b = b_ref[...]
        for r0 in range(0, t, STRIP):
            rows = min(STRIP, t - r0)
            u = b + w[kw - 1] * scr[HALO + r0:HALO + r0 + rows, :]
            for k in range(kw - 1):
                lo = HALO - (kw - 1) + k + r0
                u = u + w[k] * scr[lo:lo + rows, :]
            o_ref[r0:r0 + rows, :] = (_gelu(u[:, :FFN_COL]) * u[:, FFN_COL:]).astype(BF16)

    return pl.pallas_call(
        body, grid=(s // t, ncol),
        in_specs=[pl.BlockSpec((t, cb2), lambda i, j: (i, j)),
                  _prev_halo_spec(t, cb2, lambda i, j: j),
                  pl.BlockSpec((FFN_CONV_W, cb2), lambda i, j: (0, j)),
                  pl.BlockSpec((1, cb2), lambda i, j: (0, j))],
        out_specs=pl.BlockSpec((t, FFN_COL), lambda i, j: (i, j)),
        out_shape=jax.ShapeDtypeStruct((s, c2 // 2), BF16),
        scratch_shapes=[pltpu.VMEM((t + HALO, cb2), F32)],
        compiler_params=_params(("parallel", "parallel")), name=name,
    )(pre, pre, cw, cb)


def _ffn_act_bwd(pre, dact, cw, cb, name):
    s, c2 = pre.shape
    t = _tile(s, 512)
    cb2 = 2 * FFN_COL
    ncol = c2 // cb2
    nt = s // t
    kw = FFN_CONV_W
    ext = t + HALO

    def body(x_ref, prev_ref, next_ref, da_ref, dan_ref, w_ref, b_ref, dx_ref, dw_ref, db_ref, xs, dus):
        i = pl.program_id(1)

        @pl.when(i == 0)
        def _():
            dw_ref[...] = jnp.zeros_like(dw_ref)
            db_ref[...] = jnp.zeros_like(db_ref)

        xs[0:HALO, :] = jnp.where(i > 0, prev_ref[...], 0.0)
        xs[HALO:HALO + t, :] = x_ref[...]
        xs[HALO + t:HALO + ext, :] = next_ref[...]
        w = [w_ref[k:k + 1, :] for k in range(kw)]
        b = b_ref[...]
        for r0 in range(0, ext, STRIP):
            rows = min(STRIP, ext - r0)
            u = b + w[kw - 1] * xs[HALO + r0:HALO + r0 + rows, :]
            for k in range(kw - 1):
                lo = HALO - (kw - 1) + k + r0
                u = u + w[k] * xs[lo:lo + rows, :]
            gel, dgel = _gelu_and_grad(u[:, :FFN_COL])
            if r0 < t:
                dav = da_ref[r0:r0 + rows, :]
            else:
                dav = jnp.where(i < nt - 1, dan_ref[...], 0.0)
            dus[r0:r0 + rows, :FFN_COL] = dav * u[:, FFN_COL:] * dgel
            dus[r0:r0 + rows, FFN_COL:] = dav * gel
        fold = lambda a: a.reshape(a.shape[0] // 8, 8, cb2).sum(axis=0)
        db = jnp.zeros((8, cb2), F32)
        dw = [jnp.zeros((8, cb2), F32) for _ in range(kw)]
        for r0 in range(0, t, STRIP):
            rows = min(STRIP, t - r0)
            du = dus[r0:r0 + rows, :]
            dx = w[kw - 1] * du
            for k in range(kw - 1):
                lo = r0 + kw - 1 - k
                dx = dx + w[k] * dus[lo:lo + rows, :]
            dx_ref[r0:r0 + rows, :] = dx.astype(BF16)
            db = db + fold(du)
            for k in range(kw):
                lo = HALO - (kw - 1) + k + r0
                dw[k] = dw[k] + fold(du * xs[lo:lo + rows, :])
        db_ref[...] += jnp.sum(db, axis=0, keepdims=True)
        for k in range(kw):
            dw_ref[k:k + 1, :] += jnp.sum(dw[k], axis=0, keepdims=True)

    nxt = lambda c: pl.BlockSpec((HALO, c), lambda j, i: (jnp.minimum((i + 1) * (t // HALO), s // HALO - 1), j))
    return pl.pallas_call(
        body, grid=(ncol, nt),
        in_specs=[pl.BlockSpec((t, cb2), lambda j, i: (i, j)),
                  pl.BlockSpec((HALO, cb2), lambda j, i: (jnp.maximum(i * (t // HALO) - 1, 0), j)),
                  nxt(cb2),
                  pl.BlockSpec((t, FFN_COL), lambda j, i: (i, j)),
                  nxt(FFN_COL),
                  pl.BlockSpec((kw, cb2), lambda j, i: (0, j)),
                  pl.BlockSpec((1, cb2), lambda j, i: (0, j))],
        out_specs=[pl.BlockSpec((t, cb2), lambda j, i: (i, j)),
                   pl.BlockSpec((kw, cb2), lambda j, i: (0, j)),
                   pl.BlockSpec((1, cb2), lambda j, i: (0, j))],
        out_shape=[jax.ShapeDtypeStruct((s, c2), BF16), jax.ShapeDtypeStruct((kw, c2), F32),
                   jax.ShapeDtypeStruct((1, c2), F32)],
        scratch_shapes=[pltpu.VMEM((ext + HALO, cb2), F32), pltpu.VMEM((ext, cb2), F32)],
        compiler_params=_params(("parallel", "arbitrary")), name=name,
    )(pre, pre, pre, dact, dact, cw, cb)


def _rg_conv_fwd(proj, cw, cb, name):
    s = proj.shape[0]
    c = D_RNN_PAD
    t = _tile(s, 256)

    def body(x_ref, halo_ref, w_ref, b_ref, o_ref, ob_ref, scr):
        i = pl.program_id(0)
        scr[0:HALO, :] = jnp.where(i > 0, halo_ref[...], 0.0)
        scr[HALO:HALO + t, :] = x_ref[...]
        u = _conv_taps(scr, w_ref, b_ref, RG_CONV_W, t, HALO)
        o_ref[...] = u
        ob_ref[...] = u.astype(BF16)

    row = pl.BlockSpec((t, c), lambda i: (i, 0))
    return pl.pallas_call(
        body, grid=(s // t,),
        in_specs=[pl.BlockSpec((t, c), lambda i: (i, 1)),
                  _prev_halo_spec(t, c, lambda i: 1),
                  pl.BlockSpec((RG_CONV_W, c), lambda i: (0, 0)),
                  pl.BlockSpec((1, c), lambda i: (0, 0))],
        out_specs=[row, row],
        out_shape=[jax.ShapeDtypeStruct((s, c), F32), jax.ShapeDtypeStruct((s, c), BF16)],
        scratch_shapes=[pltpu.VMEM((t + HALO, c), F32)],
        compiler_params=_params(("parallel",)), name=name,
    )(proj, proj, cw, cb)


def _neg_expm1(x):
    y = jnp.exp(x)
    ly = jnp.log(y)
    safe = jnp.where(ly == 0.0, 1.0, ly)
    em = jnp.where(y == 1.0, x, (y - 1.0) * x / safe)
    em = jnp.where(x < -30.0, -1.0, em)
    return -em


def _rg_gates(pre_ax, rec, ba, bx, lam):
    c = D_RNN_PAD
    r = jax.nn.sigmoid(pre_ax[:, :c] + ba)
    ig = jax.nn.sigmoid(pre_ax[:, c:] + bx)
    sp = jnp.maximum(-lam, 0.0) + jnp.log1p(jnp.exp(-jnp.abs(lam)))
    log_a = (-RG_C) * r * sp
    a = jnp.exp(log_a)
    m2 = _neg_expm1(2.0 * log_a)
    mult = jnp.sqrt(m2)
    return r, ig, sp, a, m2, mult


def _rg_scan_fwd(pre_ax, rec, proj, ba, bx, lam, name):
    s = rec.shape[0]
    c = D_RNN_PAD
    t = _tile(s, 256)

    def body(pax_ref, rec_ref, gate_ref, ba_ref, bx_ref, lam_ref, h_ref, gh_ref, a_scr, u_scr, carry):
        @pl.when(pl.program_id(0) == 0)
        def _():
            carry[...] = jnp.zeros_like(carry)

        recv = rec_ref[...]
        _, ig, _, a, _, mult = _rg_gates(pax_ref[...], recv, ba_ref[...], bx_ref[...], lam_ref[...])
        a_scr[...] = a
        u_scr[...] = mult * ig * recv
        rowid = lax.broadcasted_iota(jnp.int32, (8, c), 0)

        def group(gi, h):
            r0 = pl.multiple_of(gi * 8, 8)
            a8 = a_scr[pl.ds(r0, 8), :]
            u8 = u_scr[pl.ds(r0, 8), :]
            out = jnp.zeros((8, c), F32)
            for j in range(8):
                h = jnp.broadcast_to(a8[j:j + 1, :], (8, c)) * h + jnp.broadcast_to(u8[j:j + 1, :], (8, c))
                out = jnp.where(rowid == j, h, out)
            h_ref[pl.ds(r0, 8), :] = out
            return h

        carry[...] = lax.fori_loop(0, t // 8, group, carry[...])
        gh_ref[...] = (_gelu(gate_ref[...]) * h_ref[...]).astype(BF16)

    row = pl.BlockSpec((t, c), lambda i: (i, 0))
    vec = pl.BlockSpec((1, c), lambda i: (0, 0))
    return pl.pallas_call(
        body, grid=(s // t,),
        in_specs=[pl.BlockSpec((t, 2 * c), lambda i: (i, 0)), row, row, vec, vec, vec],
        out_specs=[row, row],
        out_shape=[jax.ShapeDtypeStruct((s, c), F32), jax.ShapeDtypeStruct((s, c), BF16)],
        scratch_shapes=[pltpu.VMEM((t, c), F32), pltpu.VMEM((t, c), F32), pltpu.VMEM((8, c), F32)],
        compiler_params=_params(("arbitrary",)), name=name,
    )(pre_ax, rec, proj, ba, bx, lam)


def _rg_scan_bwd(dgh, proj, h, pre_ax, rec, ba, bx, lam, name):
    s = rec.shape[0]
    c = D_RNN_PAD
    t = _tile(s, 256)
    nt = s // t

    def body(dgh_ref, gate_ref, h_ref, hprev_ref, pax_ref, rec_ref, ba_ref, bx_ref, lam_ref,
             dgate_ref, dpax_ref, drec_ref, dba_ref, dbx_ref, dlam_ref, a_scr, dh_scr, g_scr, hp_scr, carry):
        i = pl.program_id(0)
        ti = nt - 1 - i

        @pl.when(i == 0)
        def _():
            carry[...] = jnp.zeros_like(carry)
            dba_ref[...] = jnp.zeros_like(dba_ref)
            dbx_ref[...] = jnp.zeros_like(dbx_ref)
            dlam_ref[...] = jnp.zeros_like(dlam_ref)

        recv = rec_ref[...]
        lam = lam_ref[...]
        r, ig, sp, a, m2, mult = _rg_gates(pax_ref[...], recv, ba_ref[...], bx_ref[...], lam)
        gel, dgel = _gelu_and_grad(gate_ref[...])
        dghv = dgh_ref[...]
        hv = h_ref[...]
        dgate_ref[...] = (dghv * hv * dgel).astype(BF16)
        a_scr[...] = a
        dh_scr[...] = dghv * gel
        rowid = lax.broadcasted_iota(jnp.int32, (8, c), 0)

        def group(gi, cr):
            r0 = pl.multiple_of((t // 8 - 1 - gi) * 8, 8)
            a8 = a_scr[pl.ds(r0, 8), :]
            d8 = dh_scr[pl.ds(r0, 8), :]
            out = jnp.zeros((8, c), F32)
            for j in range(7, -1, -1):
                g = jnp.broadcast_to(d8[j:j + 1, :], (8, c)) + cr
                out = jnp.where(rowid == j, g, out)
                cr = jnp.broadcast_to(a8[j:j + 1, :], (8, c)) * g
            g_scr[pl.ds(r0, 8), :] = out
            return cr

        carry[...] = lax.fori_loop(0, t // 8, group, carry[...])
        g = g_scr[...]
        hp_scr[0:HALO, :] = jnp.where(ti > 0, hprev_ref[...], 0.0)
        hp_scr[HALO:HALO + t, :] = hv
        da = g * hp_scr[pl.ds(HALO - 1, t), :]
        dmult = g * ig * recv
        dig = g * mult * recv
        drec_ref[...] = g * mult * ig
        dm2 = dmult * 0.5 / mult
        dloga = da * a - 2.0 * (1.0 - m2) * dm2
        dr = dloga * ((-RG_C) * sp)
        dsp = jnp.sum(dloga * ((-RG_C) * r), axis=0, keepdims=True)
        dpa = dr * r * (1.0 - r)
        dpx = dig * ig * (1.0 - ig)
        dpax_ref[:, :c] = dpa.astype(BF16)
        dpax_ref[:, c:] = dpx.astype(BF16)
        dba_ref[...] += jnp.sum(dpa, axis=0, keepdims=True)
        dbx_ref[...] += jnp.sum(dpx, axis=0, keepdims=True)
        dlam_ref[...] += dsp * (-jnp.exp(-(lam + sp)))

    row = pl.BlockSpec((t, c), lambda i: (nt - 1 - i, 0))
    vec = pl.BlockSpec((1, c), lambda i: (0, 0))
    return pl.pallas_call(
        body, grid=(nt,),
        in_specs=[row, row, row,
                  pl.BlockSpec((HALO, c), lambda i: (jnp.maximum((nt - 1 - i) * (t // HALO) - 1, 0), 0)),
                  pl.BlockSpec((t, 2 * c), lambda i: (nt - 1 - i, 0)), row, vec, vec, vec],
        out_specs=[row, pl.BlockSpec((t, 2 * c), lambda i: (nt - 1 - i, 0)), row, vec, vec, vec],
        out_shape=[jax.ShapeDtypeStruct((s, c), BF16), jax.ShapeDtypeStruct((s, 2 * c), BF16),
                   jax.ShapeDtypeStruct((s, c), F32), jax.ShapeDtypeStruct((1, c), F32),
                   jax.ShapeDtypeStruct((1, c), F32), jax.ShapeDtypeStruct((1, c), F32)],
        scratch_shapes=[pltpu.VMEM((t, c), F32), pltpu.VMEM((t, c), F32), pltpu.VMEM((t, c), F32),
                        pltpu.VMEM((t + HALO, c), F32), pltpu.VMEM((8, c), F32)],
        compiler_params=_params(("arbitrary",)), name=name,
    )(dgh, proj, h, h, pre_ax, rec, ba, bx, lam)


def _rg_conv_bwd(drec_a, drec_b, proj, cw, name):
    s = drec_a.shape[0]
    c = D_RNN_PAD
    t = _tile(s, 256)
    nt = s // t
    kw = RG_CONV_W
    ext = t + HALO

    def body(da_ref, dan_ref, db_ref, dbn_ref, x_ref, prev_ref, w_ref, dx_ref, dw_ref, dbias_ref, xs, dus):
        i = pl.program_id(0)

        @pl.when(i == 0)
        def _():
            dw_ref[...] = jnp.zeros_like(dw_ref)
            dbias_ref[...] = jnp.zeros_like(dbias_ref)

        xs[0:HALO, :] = jnp.where(i > 0, prev_ref[...], 0.0)
        xs[HALO:HALO + t, :] = x_ref[...]
        du = da_ref[...] + db_ref[...]
        dus[0:t, :] = du
        dus[t:ext, :] = jnp.where(i < nt - 1, dan_ref[...] + dbn_ref[...], 0.0)
        dx = w_ref[kw - 1:kw, :] * du
        for k in range(kw - 1):
            dx = dx + w_ref[k:k + 1, :] * dus[pl.ds(kw - 1 - k, t), :]
        dx_ref[...] = dx.astype(BF16)
        dbias_ref[...] += jnp.sum(du, axis=0, keepdims=True)
        for k in range(kw):
            dw_ref[k:k + 1, :] += jnp.sum(du * xs[pl.ds(HALO - (kw - 1) + k, t), :], axis=0, keepdims=True)

    row = pl.BlockSpec((t, c), lambda i: (i, 0))
    nxt = pl.BlockSpec((HALO, c), lambda i: (jnp.minimum((i + 1) * (t // HALO), s // HALO - 1), 0))
    return pl.pallas_call(
        body, grid=(nt,),
        in_specs=[row, nxt, row, nxt,
                  pl.BlockSpec((t, c), lambda i: (i, 1)),
                  _prev_halo_spec(t, c, lambda i: 1),
                  pl.BlockSpec((kw, c), lambda i: (0, 0))],
        out_specs=[row, pl.BlockSpec((kw, c), lambda i: (0, 0)), pl.BlockSpec((1, c), lambda i: (0, 0))],
        out_shape=[jax.ShapeDtypeStruct((s, c), BF16), jax.ShapeDtypeStruct((kw, c), F32),
                   jax.ShapeDtypeStruct((1, c), F32)],
        scratch_shapes=[pltpu.VMEM((t + HALO, c), F32), pltpu.VMEM((ext, c), F32)],
        compiler_params=_params(("arbitrary",)), name=name,
    )(drec_a, drec_a, drec_b, drec_b, proj, proj, cw)


N_CHIP = 4


def _chip_exchange(operands, name):
    nop = len(operands)
    sames = [same for _, same in operands]

    def body(*refs):
        srcs = refs[:nop]
        outs = refs[nop:2 * nop]
        send_sems, recv_sems, local_sems = refs[2 * nop:]
        x, y, c = lax.axis_index("x"), lax.axis_index("y"), lax.axis_index("c")
        mine = 2 * x + y
        local = []
        for o in range(nop):
            cp = pltpu.make_async_copy(srcs[o] if sames[o] else srcs[o].at[mine], outs[o].at[mine], local_sems.at[o])
            cp.start()
            local.append(cp)
        copies = []
        for k in range(1, N_CHIP):
            px = (x + (k >> 1)) % 2
            py = (y + (k & 1)) % 2
            chip = 2 * px + py
            for o in range(nop):
                sem = (k - 1) * nop + o
                src = srcs[o] if sames[o] else srcs[o].at[chip]
                send = pltpu.make_async_remote_copy(
                    src_ref=src, dst_ref=outs[o].at[mine], send_sem=send_sems.at[sem], recv_sem=recv_sems.at[sem],
                    device_id=(px, py, c), device_id_type=pl.DeviceIdType.MESH)
                send.start()
                recv = pltpu.make_async_remote_copy(
                    src_ref=src, dst_ref=outs[o].at[chip], send_sem=send_sems.at[sem], recv_sem=recv_sems.at[sem],
                    device_id=(px, py, c), device_id_type=pl.DeviceIdType.MESH)
                copies.append((send, recv))
        for send, recv in copies:
            send.wait_send()
            recv.wait_recv()
        for cp in local:
            cp.wait()

    anyspec = pl.BlockSpec(memory_space=pl.ANY)
    shapes = [jax.ShapeDtypeStruct((N_CHIP,) + (arr.shape if same else arr.shape[1:]), arr.dtype)
              for arr, same in operands]
    return pl.pallas_call(
        body, in_specs=[anyspec] * nop, out_specs=[anyspec] * nop, out_shape=shapes,
        scratch_shapes=[pltpu.SemaphoreType.DMA(((N_CHIP - 1) * nop,)), pltpu.SemaphoreType.DMA(((N_CHIP - 1) * nop,)),
                        pltpu.SemaphoreType.DMA((nop,))],
        name=name,
    )(*[arr for arr, _ in operands])


def _core_swap(operands, name):
    nop = len(operands)
    wholes = [whole for _, whole in operands]

    def body(*refs):
        srcs = refs[:nop]
        outs = refs[nop:2 * nop]
        send_sems, recv_sems = refs[2 * nop:]
        x, y, c = lax.axis_index("x"), lax.axis_index("y"), lax.axis_index("c")
        other = 1 - c
        copies = []
        for o in range(nop):
            cp = pltpu.make_async_remote_copy(
                src_ref=srcs[o] if wholes[o] else srcs[o].at[other], dst_ref=outs[o],
                send_sem=send_sems.at[o], recv_sem=recv_sems.at[o],
                device_id=(x, y, other), device_id_type=pl.DeviceIdType.MESH)
            cp.start()
            copies.append(cp)
        for cp in copies:
            cp.wait()

    anyspec = pl.BlockSpec(memory_space=pl.ANY)
    shapes = [jax.ShapeDtypeStruct(arr.shape if whole else arr.shape[1:], arr.dtype) for arr, whole in operands]
    return pl.pallas_call(
        body, in_specs=[anyspec] * nop, out_specs=[anyspec] * nop, out_shape=shapes,
        scratch_shapes=[pltpu.SemaphoreType.DMA((nop,)), pltpu.SemaphoreType.DMA((nop,))],
        name=name,
    )(*[arr for arr, _ in operands])


def _pair_sum(mine2, theirs, name):
    _, nq, n, cols = mine2.shape
    t = _tile(n, 512, 16)

    def body(a_ref, b_ref, o_ref):
        c = lax.axis_index("c")
        o_ref[...] = (a_ref[c].astype(F32) + b_ref[...].astype(F32)).astype(o_ref.dtype)

    return pl.pallas_call(
        body, grid=(nq, n // t),
        in_specs=[pl.BlockSpec((2, None, t, cols), lambda q, i: (0, q, i, 0)),
                  pl.BlockSpec((None, t, cols), lambda q, i: (q, i, 0))],
        out_specs=pl.BlockSpec((None, t, cols), lambda q, i: (q, i, 0)),
        out_shape=jax.ShapeDtypeStruct(theirs.shape, theirs.dtype),
        compiler_params=_params(("parallel", "parallel")), name=name,
    )(mine2, theirs)


def _core_sum(mine, theirs, name):
    n, _ = mine.shape
    t = _tile(n, 1024)

    def body(a_ref, b_ref, o_ref):
        o_ref[...] = a_ref[...] + b_ref[...]

    row = pl.BlockSpec((t, LANES), lambda i: (i, 0))
    return pl.pallas_call(
        body, grid=(n // t,), in_specs=[row, row], out_specs=row,
        out_shape=jax.ShapeDtypeStruct(mine.shape, mine.dtype),
        compiler_params=_params(("parallel",)), name=name,
    )(mine, theirs)


def _adam(parts, w, m, v, name):
    n, cols = w.shape
    nparts = parts.shape[0]
    t = _tile(n, 512, 16)
    c1 = 1.0 - ADAM_B1 ** ADAM_STEP
    c2 = 1.0 - ADAM_B2 ** ADAM_STEP

    def body(p_ref, w_ref, m_ref, v_ref, g_ref, d_ref, mo_ref, vo_ref):
        g = p_ref[0].astype(F32)
        for k in range(1, nparts):
            g = g + p_ref[k].astype(F32)
        mn = ADAM_B1 * m_ref[...] + (1.0 - ADAM_B1) * g
        vn = ADAM_B2 * v_ref[...] + (1.0 - ADAM_B2) * (g * g)
        m_hat = mn / c1
        v_hat = vn / c2
        g_ref[...] = g
        d_ref[...] = (-ADAM_LR) * (m_hat / (jnp.sqrt(v_hat) + ADAM_EPS) + ADAM_WD * w_ref[...])
        mo_ref[...] = mn
        vo_ref[...] = vn

    row = pl.BlockSpec((t, cols), lambda i: (i, 0))
    out = jax.ShapeDtypeStruct((n, cols), F32)
    return pl.pallas_call(
        body, grid=(n // t,),
        in_specs=[pl.BlockSpec((nparts, t, cols), lambda i: (0, i, 0)), row, row, row],
        out_specs=[row, row, row, row], out_shape=[out, out, out, out],
        compiler_params=_params(("parallel",)), name=name,
    )(parts, w, m, v)


def _pack(pieces, row_mult, lead=False):
    if lead:
        flat = jnp.concatenate([p.reshape(N_DEV, -1) for p in pieces], axis=1)
        n = flat.shape[1]
    else:
        flat = jnp.concatenate([p.reshape(-1) for p in pieces])
        n = flat.shape[0]
    group = row_mult * LANES
    total = -(-n // group) * group
    if lead:
        flat = jnp.pad(flat, ((0, 0), (0, total - n)))
        return flat.reshape(N_DEV, total // LANES, LANES)
    flat = jnp.pad(flat, (0, total - n))
    return flat.reshape(total // LANES, LANES)


def _unpack(buf, shapes, lead=False):
    out = []
    off = 0
    flat = buf.reshape(N_DEV, -1) if lead else buf.reshape(-1)
    for shp in shapes:
        size = math.prod(shp)
        if lead:
            out.append(flat[:, off:off + size].reshape((N_DEV,) + tuple(shp)))
        else:
            out.append(flat[off:off + size].reshape(shp))
        off += size
    return out


BIG = ["attn_w_qkv", "attn_w_o", "rg_w_in", "rg_w_out", "ffn_w_up", "ffn_w_down"]
SMALL = ["rg_conv_w", "rg_conv_b", "rg_b_a", "rg_b_x", "rg_lambda", "ffn_conv_w"]
REPL = ["rg_w_a", "rg_w_x", "ffn_conv_b", "mix_pre_g", "mix_post_g", "ffn_pre_g", "ffn_post_g"]
ORDER = ["attn_w_qkv", "attn_w_o", "rg_w_in", "rg_conv_w", "rg_conv_b", "rg_w_a", "rg_b_a", "rg_w_x", "rg_b_x",
         "rg_lambda", "rg_w_out", "ffn_w_up", "ffn_conv_w", "ffn_conv_b", "ffn_w_down", "mix_pre_g",
         "mix_post_g", "ffn_pre_g", "ffn_post_g"]
SHARD_AXIS = {"attn_w_qkv": 2, "attn_w_o": 1, "rg_w_in": 2, "rg_w_out": 1, "ffn_w_up": 2, "ffn_w_down": 1,
              "rg_conv_w": 2, "rg_conv_b": 1, "rg_b_a": 1, "rg_b_x": 1, "rg_lambda": 1, "ffn_conv_w": 2}
SMALL_ROWS = 64
REPL_ROWS = 128


def _assemble(stacked, axis):
    moved = jnp.moveaxis(stacked, 0, axis)
    shp = list(moved.shape)
    shp[axis:axis + 2] = [shp[axis] * shp[axis + 1]]
    return moved.reshape(shp)


def _split(full, axis):
    shp = list(full.shape)
    shp[axis:axis + 1] = [N_DEV, shp[axis] // N_DEV]
    return jnp.moveaxis(full.reshape(shp), axis, 0)


def _block_diag_pad(w):
    out = jnp.zeros((D_RNN_PAD, D_RNN_PAD), w.dtype)
    for n in range(RG_BLOCKS):
        o = n * RG_BLOCK_W
        out = lax.dynamic_update_slice(out, w[n], (o, o))
    return out


def _block_diag_take(dense):
    return jnp.stack([dense[n * RG_BLOCK_W:(n + 1) * RG_BLOCK_W, n * RG_BLOCK_W:(n + 1) * RG_BLOCK_W]
                      for n in range(RG_BLOCKS)])


def _pad_cols(a, width):
    return jnp.pad(a, ((0, 0), (0, width - a.shape[1])))


def _interleave(a):
    lead = a.shape[:-1]
    nb = D_FF // FFN_COL
    return jnp.swapaxes(a.reshape(lead + (2, nb, FFN_COL)), -3, -2).reshape(lead + (2 * D_FF,))


def _deinterleave(a):
    lead = a.shape[:-1]
    nb = D_FF // FFN_COL
    return jnp.swapaxes(a.reshape(lead + (nb, 2, FFN_COL)), -3, -2).reshape(lead + (2 * D_FF,))


def _ffn_fwd(hn, wp, i):
    pre = _matmul(hn, wp["up"][i], name=f"ffn{i}_up")
    act = _ffn_act_fwd(pre, wp["ffn_cw"][i], wp["ffn_cb"][i], name=f"ffn{i}_act")
    f = _matmul(act, wp["down"][i], name=f"ffn{i}_down")
    return pre, act, f


def _ffn_bwd(df, hn, pre, act, wp, i):
    dact = _matmul(df, wp["down"][i], tb=True, name=f"ffn{i}_dact")
    d_down = _matmul(act, df, ta=True, name=f"ffn{i}_ddown")
    dpre, dcw, dcb = _ffn_act_bwd(pre, dact, wp["ffn_cw"][i], wp["ffn_cb"][i], name=f"ffn{i}_act_bwd")
    d_up = _matmul(hn, dpre, ta=True, name=f"ffn{i}_dup")
    dhn = _matmul(dpre, wp["up"][i], tb=True, name=f"ffn{i}_dhn")
    return dhn, d_up, d_down, dcw, dcb


def _local_step(x, tgt, wp):
    g = wp["gains"]
    gr = {}

    hn0 = _rms_fwd(x, g["mix_pre"][0], name="l0_mix_pre")
    qkv = _matmul(hn0, wp["qkv"], out_dtype=BF16, name="attn_qkv")
    o, lsum, first = _attn_fwd(qkv, name="attn_fwd")
    f0 = _matmul(o, wp["wo"], name="attn_out")
    x1, hn1 = _resid_norm(x, f0, g["mix_post"][0], g["ffn_pre"][0], name="l0_mix_post")
    pre0, act0, f1 = _ffn_fwd(hn1, wp, 0)
    x2, hn2 = _resid_norm(x1, f1, g["ffn_post"][0], g["mix_pre"][1], name="l0_ffn_post")
    proj = _matmul(hn2, wp["w_in"], name="rg_in")
    rec, rec_b = _rg_conv_fwd(proj, wp["rg_cw"], wp["rg_cb"], name="rg_conv")
    pre_ax = _matmul(rec_b, wp["w_ax"], name="rg_gates")
    h, gh = _rg_scan_fwd(pre_ax, rec, proj, wp["rg_ba"], wp["rg_bx"], wp["rg_lam"], name="rg_scan")
    f2 = _matmul(gh, wp["w_out"], name="rg_out")
    x3, hn3 = _resid_norm(x2, f2, g["mix_post"][1], g["ffn_pre"][1], name="l1_mix_post")
    pre1, act1, f3 = _ffn_fwd(hn3, wp, 1)
    dy, sq = _resid_loss(x3, f3, g["ffn_post"][1], tgt, name="l1_ffn_post_loss")

    ffn_dup, ffn_ddown, ffn_dcw, ffn_dcb = [None, None], [None, None], [None, None], [None, None]
    d_ffn_post, d_ffn_pre, d_mix_post, d_mix_pre = [None, None], [None, None], [None, None], [None, None]

    df3, d_ffn_post[1] = _rms_bwd(f3, g["ffn_post"][1], dy, None, BF16, name="l1_ffn_post_bwd")
    dhn3, ffn_dup[1], ffn_ddown[1], ffn_dcw[1], ffn_dcb[1] = _ffn_bwd(df3, hn3, pre1, act1, wp, 1)
    dx3, d_ffn_pre[1] = _rms_bwd(x3, g["ffn_pre"][1], dhn3, dy, F32, name="l1_ffn_pre_bwd")

    df2, d_mix_post[1] = _rms_bwd(f2, g["mix_post"][1], dx3, None, BF16, name="l1_mix_post_bwd")
    dgh = _matmul(df2, wp["w_out"], tb=True, name="rg_dgh")
    gr["w_out"] = _matmul(gh, df2, ta=True, name="rg_dwout")
    dgate, dpax, drec_u, gr["rg_ba"], gr["rg_bx"], gr["rg_lam"] = _rg_scan_bwd(
        dgh, proj, h, pre_ax, rec, wp["rg_ba"], wp["rg_bx"], wp["rg_lam"], name="rg_scan_bwd")
    drec_g = _matmul(dpax, wp["w_ax"], tb=True, name="rg_drec")
    gr["w_ax"] = _matmul(rec_b, dpax, ta=True, name="rg_dwax")
    dproj_rec, gr["rg_cw"], gr["rg_cb"] = _rg_conv_bwd(drec_u, drec_g, proj, wp["rg_cw"], name="rg_conv_bwd")
    dproj = jnp.concatenate([dgate, dproj_rec], axis=1)
    gr["w_in"] = _matmul(hn2, dproj, ta=True, name="rg_dwin")
    dhn2 = _matmul(dproj, wp["w_in"], tb=True, name="rg_dhn")
    dx2, d_mix_pre[1] = _rms_bwd(x2, g["mix_pre"][1], dhn2, dx3, F32, name="l1_mix_pre_bwd")

    df1, d_ffn_post[0] = _rms_bwd(f1, g["ffn_post"][0], dx2, None, BF16, name="l0_ffn_post_bwd")
    dhn1, ffn_dup[0], ffn_ddown[0], ffn_dcw[0], ffn_dcb[0] = _ffn_bwd(df1, hn1, pre0, act0, wp, 0)
    dx1, d_ffn_pre[0] = _rms_bwd(x1, g["ffn_pre"][0], dhn1, dx2, F32, name="l0_ffn_pre_bwd")

    df0, d_mix_post[0] = _rms_bwd(f0, g["mix_post"][0], dx1, None, BF16, name="l0_mix_post_bwd")
    do = _matmul(df0, wp["wo"], tb=True, out_dtype=BF16, name="attn_do")
    gr["wo"] = _matmul(o, df0, ta=True, name="attn_dwo")
    dq, dk, dv = _attn_bwd(qkv, do, lsum, first, name="attn_bwd")
    dqkv = jnp.concatenate([dq, dk.astype(BF16), dv.astype(BF16)], axis=1)
    gr["qkv"] = _matmul(hn0, dqkv, ta=True, name="attn_dwqkv")
    dhn0 = _matmul(dqkv, wp["qkv"], tb=True, name="attn_dhn")
    dx0, d_mix_pre[0] = _rms_bwd(x, g["mix_pre"][0], dhn0, dx1, F32, name="l0_mix_pre_bwd")

    gr["up"] = ffn_dup
    gr["down"] = ffn_ddown
    gr["ffn_cw"] = ffn_dcw
    gr["ffn_cb"] = ffn_dcb
    gr["gains"] = {"mix_pre": d_mix_pre, "mix_post": d_mix_post, "ffn_pre": d_ffn_pre, "ffn_post": d_ffn_post}
    return sq[0, 0], dx0, gr


def _prepare(full):
    c, cp = D_RNN, D_RNN_PAD
    w_in = full["rg_w_in"][0]
    w_a = _block_diag_pad(full["rg_w_a"][0].astype(BF16))
    w_x = _block_diag_pad(full["rg_w_x"][0].astype(BF16))
    vec = lambda a: _pad_cols(a.reshape(1, c), cp)
    return {
        "qkv": full["attn_w_qkv"][0],
        "wo": full["attn_w_o"][0],
        "w_in": jnp.concatenate([_pad_cols(w_in[:, :c], cp), _pad_cols(w_in[:, c:], cp)], axis=1),
        "w_ax": jnp.concatenate([w_a, w_x], axis=1),
        "w_out": jnp.pad(full["rg_w_out"][0], ((0, cp - c), (0, 0))),
        "rg_cw": _pad_cols(full["rg_conv_w"][0], cp),
        "rg_cb": vec(full["rg_conv_b"][0]),
        "rg_ba": vec(full["rg_b_a"][0]),
        "rg_bx": vec(full["rg_b_x"][0]),
        "rg_lam": vec(full["rg_lambda"][0]),
        "up": [_interleave(full["ffn_w_up"][i]) for i in range(2)],
        "down": [full["ffn_w_down"][i] for i in range(2)],
        "ffn_cw": [_interleave(full["ffn_conv_w"][i]) for i in range(2)],
        "ffn_cb": [_interleave(full["ffn_conv_b"][i].reshape(1, -1)) for i in range(2)],
        "gains": {k: [full[k + "_g"][i].reshape(1, D_MODEL) for i in range(2)]
                  for k in ("mix_pre", "mix_post", "ffn_pre", "ffn_post")},
    }


def _natural_grads(gr):
    c, cp = D_RNN, D_RNN_PAD
    gg = gr["gains"]
    stack2 = lambda pair: jnp.stack([pair[0].reshape(-1), pair[1].reshape(-1)])
    return {
        "attn_w_qkv": gr["qkv"][None],
        "attn_w_o": gr["wo"][None],
        "rg_w_in": jnp.concatenate([gr["w_in"][:, :c], gr["w_in"][:, cp:cp + c]], axis=1)[None],
        "rg_conv_w": gr["rg_cw"][:, :c][None],
        "rg_conv_b": gr["rg_cb"][:, :c],
        "rg_w_a": _block_diag_take(gr["w_ax"][:, :cp])[None],
        "rg_b_a": gr["rg_ba"][:, :c],
        "rg_w_x": _block_diag_take(gr["w_ax"][:, cp:])[None],
        "rg_b_x": gr["rg_bx"][:, :c],
        "rg_lambda": gr["rg_lam"][:, :c],
        "rg_w_out": gr["w_out"][:c][None],
        "ffn_w_up": jnp.stack([_deinterleave(gr["up"][i]) for i in range(2)]),
        "ffn_conv_w": jnp.stack([_deinterleave(gr["ffn_cw"][i]) for i in range(2)]),
        "ffn_conv_b": jnp.stack([_deinterleave(gr["ffn_cb"][i]).reshape(-1) for i in range(2)]),
        "ffn_w_down": jnp.stack(gr["down"]),
        "mix_pre_g": stack2(gg["mix_pre"]),
        "mix_post_g": stack2(gg["mix_post"]),
        "ffn_pre_g": stack2(gg["ffn_pre"]),
        "ffn_post_g": stack2(gg["ffn_post"]),
    }


def _rows(a):
    return a.reshape(-1, a.shape[-1])


def _step(x, loss_target, w, m, v):
    small_shapes = [w[n].shape for n in SMALL]
    repl_shapes = [w[n].shape for n in REPL]
    wsmall = _pack([w[n] for n in SMALL], SMALL_ROWS)
    on_chips = _chip_exchange([(_rows(w[n].astype(BF16)), True) for n in BIG] + [(wsmall, True)],
                              name="gather_weights_chips")
    from_sibling = _core_swap([(a, True) for a in on_chips], name="gather_weights_cores")
    south = lax.axis_index("c") == 0

    def by_device(mine, theirs):
        pair = jnp.stack([jnp.where(south, mine, theirs), jnp.where(south, theirs, mine)], axis=1)
        return pair.reshape((N_DEV,) + mine.shape[1:])

    full = {}
    for n, a, b in zip(BIG, on_chips, from_sibling):
        full[n] = _assemble(by_device(a, b).reshape((N_DEV,) + w[n].shape), SHARD_AXIS[n])
    for n, st in zip(SMALL, _unpack(by_device(on_chips[-1], from_sibling[-1]), small_shapes, lead=True)):
        full[n] = _assemble(st, SHARD_AXIS[n])
    for n in REPL:
        full[n] = w[n]

    sq, dx, gr = _local_step(x[0], loss_target[0], _prepare(full))
    grads = _natural_grads(gr)

    by_core = lambda a: jnp.swapaxes(a.reshape((N_CHIP, 2) + a.shape[1:]), 0, 1)
    pbig = [by_core(_split(grads[n], SHARD_AXIS[n]).astype(BF16).reshape((N_DEV,) + _rows(w[n]).shape)) for n in BIG]
    psmall = by_core(_pack([_split(grads[n], SHARD_AXIS[n]) for n in SMALL], SMALL_ROWS, lead=True))
    prepl = _pack([grads[n] for n in REPL], REPL_ROWS)
    swapped = _core_swap([(p, False) for p in pbig] + [(psmall, False), (prepl, True)], name="reduce_grads_cores")
    sums = [_pair_sum(p, o, name="chip_sum_" + n) for n, p, o in zip(BIG, pbig, swapped)]
    sums.append(_pair_sum(psmall, swapped[-2], name="chip_sum_small"))
    trepl = _core_sum(prepl, swapped[-1], name="chip_sum_repl")
    reduced = _chip_exchange([(t, False) for t in sums] + [(trepl, True)], name="reduce_grads_chips")

    out = {}
    kinds = ("grad", "delta", "new_m", "new_v")
    for n, parts in zip(BIG, reduced):
        res = _adam(parts, _rows(w[n]), _rows(m[n]), _rows(v[n]), name="adamw_" + n)
        for kind, buf in zip(kinds, res):
            out[(kind, n)] = buf.reshape(w[n].shape)
    for names, parts, rows, shapes, tag in ((SMALL, reduced[-2], SMALL_ROWS, small_shapes, "small"),
                                            (REPL, reduced[-1], REPL_ROWS, repl_shapes, "repl")):
        packs = [_pack([d[n] for n in names], rows) for d in (w, m, v)]
        res = _adam(parts, *packs, name="adamw_" + tag)
        for kind, buf in zip(kinds, res):
            for n, a in zip(names, _unpack(buf, shapes)):
                out[(kind, n)] = a

    loss = lax.psum(sq * (0.5 / D_MODEL), ("x", "y", "c"))
    return loss, dx[None], out


def kernel(x, attn_w_qkv, attn_w_o, rg_w_in, rg_conv_w, rg_conv_b, rg_w_a, rg_b_a, rg_w_x, rg_b_x, rg_lambda, rg_w_out, ffn_w_up, ffn_conv_w, ffn_conv_b, ffn_w_down, mix_pre_g, mix_post_g, ffn_pre_g, ffn_post_g, loss_target, m_attn_w_qkv, m_attn_w_o, m_rg_w_in, m_rg_conv_w, m_rg_conv_b, m_rg_w_a, m_rg_b_a, m_rg_w_x, m_rg_b_x, m_rg_lambda, m_rg_w_out, m_ffn_w_up, m_ffn_conv_w, m_ffn_conv_b, m_ffn_w_down, m_mix_pre_g, m_mix_post_g, m_ffn_pre_g, m_ffn_post_g, v_attn_w_qkv, v_attn_w_o, v_rg_w_in, v_rg_conv_w, v_rg_conv_b, v_rg_w_a, v_rg_b_a, v_rg_w_x, v_rg_b_x, v_rg_lambda, v_rg_w_out, v_ffn_w_up, v_ffn_conv_w, v_ffn_conv_b, v_ffn_w_down, v_mix_pre_g, v_mix_post_g, v_ffn_pre_g, v_ffn_post_g):
    w = dict(zip(ORDER, (attn_w_qkv, attn_w_o, rg_w_in, rg_conv_w, rg_conv_b, rg_w_a, rg_b_a, rg_w_x, rg_b_x,
                         rg_lambda, rg_w_out, ffn_w_up, ffn_conv_w, ffn_conv_b, ffn_w_down, mix_pre_g,
                         mix_post_g, ffn_pre_g, ffn_post_g)))
    m = dict(zip(ORDER, (m_attn_w_qkv, m_attn_w_o, m_rg_w_in, m_rg_conv_w, m_rg_conv_b, m_rg_w_a, m_rg_b_a,
                         m_rg_w_x, m_rg_b_x, m_rg_lambda, m_rg_w_out, m_ffn_w_up, m_ffn_conv_w, m_ffn_conv_b,
                         m_ffn_w_down, m_mix_pre_g, m_mix_post_g, m_ffn_pre_g, m_ffn_post_g)))
    v = dict(zip(ORDER, (v_attn_w_qkv, v_attn_w_o, v_rg_w_in, v_rg_conv_w, v_rg_conv_b, v_rg_w_a, v_rg_b_a,
                         v_rg_w_x, v_rg_b_x, v_rg_lambda, v_rg_w_out, v_ffn_w_up, v_ffn_conv_w, v_ffn_conv_b,
                         v_ffn_w_down, v_mix_pre_g, v_mix_post_g, v_ffn_pre_g, v_ffn_post_g)))
    loss, dx, out = _step(x, loss_target, w, m, v)
    return (loss, dx, *[out[("grad", n)] for n in ORDER], *[out[("delta", n)] for n in ORDER],
            *[out[("new_m", n)] for n in ORDER], *[out[("new_v", n)] for n in ORDER])
```

```python
import functools
import math

import jax
import jax.numpy as jnp
from jax import lax
from jax.experimental import pallas as pl
from jax.experimental.pallas import tpu as pltpu

F32 = jnp.float32
BF16 = jnp.bfloat16

D_MODEL = 1024
N_HEADS = 16
HEAD_DIM = 64
D_RNN = 1344
D_RNN_PAD = 1408
RG_BLOCKS = 16
RG_BLOCK_W = 84
RG_CONV_W = 4
RG_C = 8.0
D_FF = 2816
FFN_CONV_W = 3
FFN_COL = 256
NORM_EPS = 1e-6
N_DEV = 8
LANES = 128
HALO = 8
STRIP = 32

ADAM_LR = 0.001
ADAM_B1 = 0.9
ADAM_B2 = 0.999
ADAM_EPS = 1e-08
ADAM_WD = 0.01
ADAM_STEP = 10

ATT_BLK = 256
ATT_HEADS = 4
ATT_STOP_LOG = -110.0
VMEM_LIMIT = 56 * 1024 * 1024

_GELU_C = math.sqrt(2.0 / math.pi)


def _params(sem=None):
    return pltpu.CompilerParams(dimension_semantics=sem, vmem_limit_bytes=VMEM_LIMIT)


def _tile(n, cap, mult=8):
    if n <= cap:
        return n
    best = None
    for t in range(mult, cap + 1, mult):
        if n % t == 0:
            best = t
    assert best is not None, (n, cap, mult)
    return best


def _gelu_and_grad(x):
    x2 = x * x
    th = jnp.tanh(_GELU_C * (x + 0.044715 * (x2 * x)))
    cdf = 0.5 * (1.0 + th)
    gel = x * cdf
    dgel = cdf + 0.5 * x * (1.0 - th * th) * (_GELU_C * (1.0 + 3.0 * 0.044715 * x2))
    return gel, dgel


def _gelu(x):
    th = jnp.tanh(_GELU_C * (x + 0.044715 * (x * x * x)))
    return x * (0.5 * (1.0 + th))


def _matmul(a, b, *, ta=False, tb=False, out_dtype=F32, name):
    if ta:
        kdim, m = a.shape
    else:
        m, kdim = a.shape
    if tb:
        n, kb_ = b.shape
    else:
        kb_, n = b.shape
    assert kdim == kb_, (a.shape, b.shape, ta, tb)
    wide = lambda d: 1408 if d % 1408 == 0 else 1024
    tn = _tile(n, wide(n), 128)
    tk = _tile(kdim, 1024, 128) if ta else _tile(kdim, wide(kdim), 128)
    nk = kdim // tk
    tm = _tile(m, wide(m), 128) if ta else _tile(m, 2048 if nk == 1 else 1024)
    dims = (((0 if ta else 1,), (1 if tb else 0,)), ((), ()))

    def body(a_ref, b_ref, o_ref, *scratch):
        part = lax.dot_general(a_ref[...], b_ref[...], dims, preferred_element_type=F32)
        if nk == 1:
            o_ref[...] = part.astype(out_dtype)
        else:
            (acc_ref,) = scratch
            k = pl.program_id(2)

            @pl.when(k == 0)
            def _():
                acc_ref[...] = part

            @pl.when(k > 0)
            def _():
                acc_ref[...] += part

            @pl.when(k == nk - 1)
            def _():
                o_ref[...] = acc_ref[...].astype(out_dtype)

    a_spec = pl.BlockSpec((tk, tm), lambda i, j, k: (k, i)) if ta else pl.BlockSpec((tm, tk), lambda i, j, k: (i, k))
    b_spec = pl.BlockSpec((tn, tk), lambda i, j, k: (j, k)) if tb else pl.BlockSpec((tk, tn), lambda i, j, k: (k, j))
    return pl.pallas_call(
        body,
        grid=(m // tm, n // tn, nk),
        in_specs=[a_spec, b_spec],
        out_specs=pl.BlockSpec((tm, tn), lambda i, j, k: (i, j)),
        out_shape=jax.ShapeDtypeStruct((m, n), out_dtype),
        scratch_shapes=[pltpu.VMEM((tm, tn), F32)] if nk > 1 else [],
        compiler_params=_params(("parallel", "parallel", "arbitrary")),
        name=name,
    )(a, b)


def _rms(xv, g):
    inv = lax.rsqrt(jnp.mean(xv * xv, axis=-1, keepdims=True) + NORM_EPS)
    return xv * inv * g


def _rms_fwd(x, g, name):
    s, d = x.shape
    t = _tile(s, 512)

    def body(x_ref, g_ref, o_ref):
        o_ref[...] = _rms(x_ref[...], g_ref[...]).astype(BF16)

    return pl.pallas_call(
        body, grid=(s // t,),
        in_specs=[pl.BlockSpec((t, d), lambda i: (i, 0)), pl.BlockSpec((1, d), lambda i: (0, 0))],
        out_specs=pl.BlockSpec((t, d), lambda i: (i, 0)),
        out_shape=jax.ShapeDtypeStruct((s, d), BF16),
        compiler_params=_params(("parallel",)), name=name,
    )(x, g)


def _resid_norm(x, f, post_g, pre_g, name):
    s, d = x.shape
    t = _tile(s, 512)

    def body(x_ref, f_ref, pg_ref, ng_ref, xo_ref, hn_ref):
        xn = x_ref[...] + _rms(f_ref[...], pg_ref[...])
        xo_ref[...] = xn
        hn_ref[...] = _rms(xn, ng_ref[...]).astype(BF16)

    row = pl.BlockSpec((t, d), lambda i: (i, 0))
    vec = pl.BlockSpec((1, d), lambda i: (0, 0))
    return pl.pallas_call(
        body, grid=(s // t,), in_specs=[row, row, vec, vec], out_specs=[row, row],
        out_shape=[jax.ShapeDtypeStruct((s, d), F32), jax.ShapeDtypeStruct((s, d), BF16)],
        compiler_params=_params(("parallel",)), name=name,
    )(x, f, post_g, pre_g)


def _resid_loss(x, f, post_g, tgt, name):
    s, d = x.shape
    t = _tile(s, 512)

    def body(x_ref, f_ref, pg_ref, t_ref, dy_ref, loss_ref):
        @pl.when(pl.program_id(0) == 0)
        def _():
            loss_ref[...] = jnp.zeros_like(loss_ref)

        err = x_ref[...] + _rms(f_ref[...], pg_ref[...]) - t_ref[...]
        dy_ref[...] = err * (1.0 / d)
        sq = jnp.sum(jnp.sum(err * err, axis=1, keepdims=True), axis=0, keepdims=True)
        loss_ref[...] += sq

    row = pl.BlockSpec((t, d), lambda i: (i, 0))
    vec = pl.BlockSpec((1, d), lambda i: (0, 0))
    return pl.pallas_call(
        body, grid=(s // t,), in_specs=[row, row, vec, row],
        out_specs=[row, pl.BlockSpec((8, LANES), lambda i: (0, 0))],
        out_shape=[jax.ShapeDtypeStruct((s, d), F32), jax.ShapeDtypeStruct((8, LANES), F32)],
        compiler_params=_params(("arbitrary",)), name=name,
    )(x, f, post_g, tgt)


def _rms_bwd(xin, g, dy, add, out_dtype, name):
    s, d = xin.shape
    t = _tile(s, 512)
    has_add = add is not None

    def body(*refs):
        if has_add:
            x_ref, g_ref, dy_ref, add_ref, dx_ref, dg_ref = refs
        else:
            x_ref, g_ref, dy_ref, dx_ref, dg_ref = refs

        @pl.when(pl.program_id(0) == 0)
        def _():
            dg_ref[...] = jnp.zeros_like(dg_ref)

        xv = x_ref[...]
        dyv = dy_ref[...].astype(F32)
        inv = lax.rsqrt(jnp.mean(xv * xv, axis=-1, keepdims=True) + NORM_EPS)
        xh = xv * inv
        gy = dyv * g_ref[...]
        dx = inv * (gy - xh * jnp.mean(xh * gy, axis=-1, keepdims=True))
        if has_add:
            dx = dx + add_ref[...]
        dx_ref[...] = dx.astype(out_dtype)
        dg_ref[...] += jnp.sum(dyv * xh, axis=0, keepdims=True)

    row = pl.BlockSpec((t, d), lambda i: (i, 0))
    vec = pl.BlockSpec((1, d), lambda i: (0, 0))
    args = [xin, g, dy] + ([add] if has_add else [])
    specs = [row, vec, row] + ([row] if has_add else [])
    return pl.pallas_call(
        body, grid=(s // t,), in_specs=specs, out_specs=[row, vec],
        out_shape=[jax.ShapeDtypeStruct((s, d), out_dtype), jax.ShapeDtypeStruct((1, d), F32)],
        compiler_params=_params(("arbitrary",)), name=name,
    )(*args)


_NT = (((1,), (1,)), ((), ()))
_NN = (((1,), (0,)), ((), ()))
_TN = (((0,), (0,)), ((), ()))


def _tri_dot(x, tri):
    hi = x.astype(BF16)
    lo = (x - hi.astype(F32)).astype(BF16)
    out = lax.dot_general(hi, tri, _NN, preferred_element_type=F32)
    return out + lax.dot_general(lo, tri, _NN, preferred_element_type=F32)


def _log_sigmoids(z):
    lb = jnp.minimum(z, 0.0) - jnp.log(1.0 + jnp.exp(-jnp.abs(z)))
    return lb, lb - z


def _att_blk(s):
    return min(ATT_BLK, s)


def _attn_fwd(qkv, name):
    s = qkv.shape[0]
    blk = _att_blk(s)
    nq = s // blk
    nh = ATT_HEADS
    width = nh * HEAD_DIM
    ngrp = N_HEADS // nh
    hs = range(nh)
    scale = HEAD_DIM ** -0.5

    def body(q_ref, k_ref, v_ref, o_ref, l_ref, first_ref):
        qi = pl.program_id(1)
        row = lax.broadcasted_iota(jnp.int32, (blk, blk), 0)
        col = lax.broadcasted_iota(jnp.int32, (blk, blk), 1)
        causal = col < row
        tri = (row > col).astype(BF16)
        lane_head = lax.broadcasted_iota(jnp.int32, (blk, width), 1) // HEAD_DIM
        masks = [lane_head == hh for hh in hs]
        qs = q_ref[...] * scale
        qh = [jnp.where(m, qs, jnp.zeros_like(qs)) for m in masks]

        def block(kb, carry, diag):
            rsums, acc = carry
            r0 = pl.multiple_of(kb * blk, blk)
            k2 = k_ref[pl.ds(r0, blk), :]
            v2 = v_ref[pl.ds(r0, blk), :]
            z = [lax.dot_general(qh[hh], k2, _NT, preferred_element_type=F32) for hh in hs]
            lbm = [_log_sigmoids(z[hh]) for hh in hs]
            lb = [p[0] for p in lbm]
            lm = [jnp.where(causal, p[1], 0.0) if diag else p[1] for p in lbm]
            cs = [_tri_dot(lm[hh], tri) for hh in hs]
            w = [jnp.exp(lb[hh] + cs[hh] + rsums[hh]) for hh in hs]
            if diag:
                w = [jnp.where(causal, w[hh], 0.0) for hh in hs]
            for hh in hs:
                vh = jnp.where(masks[hh], v2, jnp.zeros_like(v2))
                acc = acc + lax.dot_general(w[hh].astype(BF16), vh, _NN, preferred_element_type=F32)
            return tuple(rsums[hh] + (cs[hh][:, 0:1] + lm[hh][:, 0:1]) for hh in hs), acc

        def largest(rsums):
            m = jnp.max(rsums[0])
            for hh in range(1, nh):
                m = jnp.maximum(m, jnp.max(rsums[hh]))
            return m

        def more(c):
            return jnp.logical_and(c[0] < qi, c[1] > ATT_STOP_LOG)

        def step(c):
            rsums, acc = block(qi - 1 - c[0], (c[2], c[3]), False)
            return c[0] + 1, largest(rsums), rsums, acc

        zero = jnp.zeros((blk, 1), F32)
        rsums, acc = block(qi, ((zero,) * nh, jnp.zeros((blk, width), F32)), True)
        done, _, rsums, acc = lax.while_loop(more, step, (jnp.int32(0), largest(rsums), rsums, acc))
        first_ref[pl.program_id(0) * nq + qi] = (qi - done).astype(F32)
        o_ref[...] = acc.astype(BF16)
        lout = rsums[nh - 1]
        for hh in range(nh - 2, -1, -1):
            lout = jnp.where(masks[hh], rsums[hh], lout)
        l_ref[...] = lout

    return pl.pallas_call(
        body, grid=(ngrp, nq),
        in_specs=[pl.BlockSpec((blk, width), lambda h, i: (i, h)),
                  pl.BlockSpec((s, width), lambda h, i: (0, ngrp + h)),
                  pl.BlockSpec((s, width), lambda h, i: (0, 2 * ngrp + h))],
        out_specs=[pl.BlockSpec((blk, width), lambda h, i: (i, h)), pl.BlockSpec((blk, width), lambda h, i: (i, h)),
                   pl.BlockSpec(memory_space=pltpu.SMEM)],
        out_shape=[jax.ShapeDtypeStruct((s, D_MODEL), BF16), jax.ShapeDtypeStruct((s, D_MODEL), F32),
                   jax.ShapeDtypeStruct((ngrp * nq,), F32)],
        compiler_params=_params(("arbitrary", "arbitrary")), name=name,
    )(qkv, qkv, qkv)


def _attn_bwd(qkv, do, lsum, first, name):
    s = qkv.shape[0]
    blk = _att_blk(s)
    nq = s // blk
    nh = ATT_HEADS
    width = nh * HEAD_DIM
    ngrp = N_HEADS // nh
    hs = range(nh)
    scale = HEAD_DIM ** -0.5

    def body(first_ref, q_ref, k_ref, v_ref, do_ref, l_ref, dq_ref, dk_ref, dv_ref):
        qi = pl.program_id(1)
        start = first_ref[pl.program_id(0) * nq + qi].astype(jnp.int32)

        @pl.when(qi == 0)
        def _():
            dk_ref[...] = jnp.zeros_like(dk_ref)
            dv_ref[...] = jnp.zeros_like(dv_ref)

        row = lax.broadcasted_iota(jnp.int32, (blk, blk), 0)
        col = lax.broadcasted_iota(jnp.int32, (blk, blk), 1)
        causal = col < row
        tri_in = (row <= col).astype(BF16)
        tri_ex = (row < col).astype(BF16)
        lane_head = lax.broadcasted_iota(jnp.int32, (blk, width), 1) // HEAD_DIM
        masks = [lane_head == hh for hh in hs]
        q2 = q_ref[...]
        qs = q2 * scale
        do2 = do_ref[...]
        l2 = l_ref[...]
        qh = [jnp.where(m, q2, jnp.zeros_like(q2)) for m in masks]
        qsh = [jnp.where(m, qs, jnp.zeros_like(qs)) for m in masks]
        doh = [jnp.where(m, do2, jnp.zeros_like(do2)) for m in masks]
        ltot = [l2[:, hh * HEAD_DIM:hh * HEAD_DIM + 1] for hh in hs]
        last = slice(blk - 1, blk)

        def block(kb, carry, diag):
            r0 = pl.multiple_of(kb * blk, blk)
            k2 = k_ref[pl.ds(r0, blk), :]
            v2 = v_ref[pl.ds(r0, blk), :]
            dq, pre_lm, pre_da = carry
            mask = (lambda a: jnp.where(causal, a, 0.0)) if diag else (lambda a: a)
            z = [lax.dot_general(qsh[hh], k2, _NT, preferred_element_type=F32) for hh in hs]
            dw = [lax.dot_general(doh[hh], v2, _NT, preferred_element_type=F32) for hh in hs]
            lbm = [_log_sigmoids(z[hh]) for hh in hs]
            lb = [p[0] for p in lbm]
            lm = [mask(p[1]) for p in lbm]
            pin = [_tri_dot(lm[hh], tri_in) for hh in hs]
            w = [mask(jnp.exp(lb[hh] + (ltot[hh] - pre_lm[hh]) - pin[hh])) for hh in hs]
            da = [w[hh] * dw[hh] for hh in hs]
            cex = [_tri_dot(da[hh], tri_ex) for hh in hs]
            dzb = [(mask(da[hh] - (da[hh] + (cex[hh] + pre_da[hh])) * jnp.exp(lb[hh])) * scale).astype(BF16)
                   for hh in hs]
            dk_blk = jnp.zeros((blk, width), F32)
            dv_blk = jnp.zeros((blk, width), F32)
            for hh in hs:
                kh = jnp.where(masks[hh], k2, jnp.zeros_like(k2))
                dq = dq + lax.dot_general(dzb[hh], kh, _NN, preferred_element_type=F32)
                dk_blk = dk_blk + lax.dot_general(dzb[hh], qh[hh], _TN, preferred_element_type=F32)
                dv_blk = dv_blk + lax.dot_general(w[hh].astype(BF16), doh[hh], _TN, preferred_element_type=F32)
            dk_ref[pl.ds(r0, blk), :] += dk_blk
            dv_ref[pl.ds(r0, blk), :] += dv_blk
            return (dq, tuple(pre_lm[hh] + pin[hh][:, last] for hh in hs),
                    tuple(pre_da[hh] + (cex[hh][:, last] + da[hh][:, last]) for hh in hs))

        zero = jnp.zeros((blk, 1), F32)
        init = (jnp.zeros((blk, width), F32), (zero,) * nh, (zero,) * nh)
        carry = lax.fori_loop(start, qi, lambda i, c: block(i, c, False), init)
        dq, _, _ = block(qi, carry, True)
        dq_ref[...] = dq.astype(BF16)

    qblk = pl.BlockSpec((blk, width), lambda h, i: (i, h))
    once = pl.Buffered(1)
    full = pl.BlockSpec((s, width), lambda h, i: (0, h), pipeline_mode=once)
    return pl.pallas_call(
        body, grid=(ngrp, nq),
        in_specs=[pl.BlockSpec(memory_space=pltpu.SMEM), qblk,
                  pl.BlockSpec((s, width), lambda h, i: (0, ngrp + h), pipeline_mode=once),
                  pl.BlockSpec((s, width), lambda h, i: (0, 2 * ngrp + h), pipeline_mode=once),
                  qblk, qblk],
        out_specs=[qblk, full, full],
        out_shape=[jax.ShapeDtypeStruct((s, D_MODEL), BF16), jax.ShapeDtypeStruct((s, D_MODEL), F32),
                   jax.ShapeDtypeStruct((s, D_MODEL), F32)],
        compiler_params=_params(("arbitrary", "arbitrary")), name=name,
    )(first, qkv, qkv, qkv, do, lsum)


def _prev_halo_spec(t, c, col):
    return pl.BlockSpec((HALO, c), lambda *g: (jnp.maximum(g[0] * (t // HALO) - 1, 0), col(*g)))


def _conv_taps(scr, w_ref, b_ref, kw, rows, lead):
    out = b_ref[...] + w_ref[kw - 1:kw, :] * scr[pl.ds(lead, rows), :]
    for k in range(kw - 1):
        out = out + w_ref[k:k + 1, :] * scr[pl.ds(lead - (kw - 1) + k, rows), :]
    return out


def _ffn_act_fwd(pre, cw, cb, name):
    s, c2 = pre.shape
    t = _tile(s, 512)
    cb2 = 2 * FFN_COL
    ncol = c2 // cb2
    kw = FFN_CONV_W

    def body(x_ref, halo_ref, w_ref, b_ref, o_ref, scr):
        i = pl.program_id(0)
        scr[0:HALO, :] = jnp.where(i > 0, halo_ref[...], 0.0)
        scr[HALO:HALO + t, :] = x_ref[...]
        w = [w_ref[k:k + 1, :] for k in range(kw)]
        b = b_ref[...]
        for r0 in range(0, t, STRIP):
            rows = min(STRIP, t - r0)
            u = b + w[kw - 1] * scr[HALO + r0:HALO + r0 + rows, :]
            for k in range(kw - 1):
                lo = HALO - (kw - 1) + k + r0
                u = u + w[k] * scr[lo:lo + rows, :]
            o_ref[r0:r0 + rows, :] = (_gelu(u[:, :FFN_COL]) * u[:, FFN_COL:]).astype(BF16)

    return pl.pallas_call(
        body, grid=(s // t, ncol),
        in_specs=[pl.BlockSpec((t, cb2), lambda i, j: (i, j)),
                  _prev_halo_spec(t, cb2, lambda i, j: j),
                  pl.BlockSpec((FFN_CONV_W, cb2), lambda i, j: (0, j)),
                  pl.BlockSpec((1, cb2), lambda i, j: (0, j))],
        out_specs=pl.BlockSpec((t, FFN_COL), lambda i, j: (i, j)),
        out_shape=jax.ShapeDtypeStruct((s, c2 // 2), BF16),
        scratch_shapes=[pltpu.VMEM((t + HALO, cb2), F32)],
        compiler_params=_params(("parallel", "parallel")), name=name,
    )(pre, pre, cw, cb)


def _ffn_act_bwd(pre, dact, cw, cb, name):
    s, c2 = pre.shape
    t = _tile(s, 512)
    cb2 = 2 * FFN_COL
    ncol = c2 // cb2
    nt = s // t
    kw = FFN_CONV_W
    ext = t + HALO

    def body(x_ref, prev_ref, next_ref, da_ref, dan_ref, w_ref, b_ref, dx_ref, dw_ref, db_ref, xs, dus):
        i = pl.program_id(1)

        @pl.when(i == 0)
        def _():
            dw_ref[...] = jnp.zeros_like(dw_ref)
            db_ref[...] = jnp.zeros_like(db_ref)

        xs[0:HALO, :] = jnp.where(i > 0, prev_ref[...], 0.0)
        xs[HALO:HALO + t, :] = x_ref[...]
        xs[HALO + t:HALO + ext, :] = next_ref[...]
        w = [w_ref[k:k + 1, :] for k in range(kw)]
        b = b_ref[...]
        for r0 in range(0, ext, STRIP):
            rows = min(STRIP, ext - r0)
            u = b + w[kw - 1] * xs[HALO + r0:HALO + r0 + rows, :]
            for k in range(kw - 1):
                lo = HALO - (kw - 1) + k + r0
                u = u + w[k] * xs[lo:lo + rows, :]
            gel, dgel = _gelu_and_grad(u[:, :FFN_COL])
            if r0 < t:
                dav = da_ref[r0:r0 + rows, :]
            else:
                dav = jnp.where(i < nt - 1, dan_ref[...], 0.0)
            dus[r0:r0 + rows, :FFN_COL] = dav * u[:, FFN_COL:] * dgel
            dus[r0:r0 + rows, FFN_COL:] = dav * gel
        fold = lambda a: a.reshape(a.shape[0] // 8, 8, cb2).sum(axis=0)
        db = jnp.zeros((8, cb2), F32)
        dw = [jnp.zeros((8, cb2), F32) for _ in range(kw)]
        for r0 in range(0, t, STRIP):
            rows = min(STRIP, t - r0)
            du = dus[r0:r0 + rows, :]
            dx = w[kw - 1] * du
            for k in range(kw - 1):
                lo = r0 + kw - 1 - k
                dx = dx + w[k] * dus[lo:lo + rows, :]
            dx_ref[r0:r0 + rows, :] = dx.astype(BF16)
            db = db + fold(du)
            for k in range(kw):
                lo = HALO - (kw - 1) + k + r0
                dw[k] = dw[k] + fold(du * xs[lo:lo + rows, :])
        db_ref[...] += jnp.sum(db, axis=0, keepdims=True)
        for k in range(kw):
            dw_ref[k:k + 1, :] += jnp.sum(dw[k], axis=0, keepdims=True)

    nxt = lambda c: pl.BlockSpec((HALO, c), lambda j, i: (jnp.minimum((i + 1) * (t // HALO), s // HALO - 1), j))
    return pl.pallas_call(
        body, grid=(ncol, nt),
        in_specs=[pl.BlockSpec((t, cb2), lambda j, i: (i, j)),
                  pl.BlockSpec((HALO, cb2), lambda j, i: (jnp.maximum(i * (t // HALO) - 1, 0), j)),
                  nxt(cb2),
                  pl.BlockSpec((t, FFN_COL), lambda j, i: (i, j)),
                  nxt(FFN_COL),
                  pl.BlockSpec((kw, cb2), lambda j, i: (0, j)),
                  pl.BlockSpec((1, cb2), lambda j, i: (0, j))],
        out_specs=[pl.BlockSpec((t, cb2), lambda j, i: (i, j)),
                   pl.BlockSpec((kw, cb2), lambda j, i: (0, j)),
                   pl.BlockSpec((1, cb2), lambda j, i: (0, j))],
        out_shape=[jax.ShapeDtypeStruct((s, c2), BF16), jax.ShapeDtypeStruct((kw, c2), F32),
                   jax.ShapeDtypeStruct((1, c2), F32)],
        scratch_shapes=[pltpu.VMEM((ext + HALO, cb2), F32), pltpu.VMEM((ext, cb2), F32)],
        compiler_params=_params(("parallel", "arbitrary")), name=name,
    )(pre, pre, pre, dact, dact, cw, cb)


def _rg_conv_fwd(proj, cw, cb, name):
    s = proj.shape[0]
    c = D_RNN_PAD
    t = _tile(s, 256)

    def body(x_ref, halo_ref, w_ref, b_ref, o_ref, ob_ref, scr):
        i = pl.program_id(0)
        scr[0:HALO, :] = jnp.where(i > 0, halo_ref[...], 0.0)
        scr[HALO:HALO + t, :] = x_ref[...]
        u = _conv_taps(scr, w_ref, b_ref, RG_CONV_W, t, HALO)
        o_ref[...] = u
        ob_ref[...] = u.astype(BF16)

    row = pl.BlockSpec((t, c), lambda i: (i, 0))
    return pl.pallas_call(
        body, grid=(s // t,),
        in_specs=[pl.BlockSpec((t, c), lambda i: (i, 1)),
                  _prev_halo_spec(t, c, lambda i: 1),
                  pl.BlockSpec((RG_CONV_W, c), lambda i: (0, 0)),
                  pl.BlockSpec((1, c), lambda i: (0, 0))],
        out_specs=[row, row],
        out_shape=[jax.ShapeDtypeStruct((s, c), F32), jax.ShapeDtypeStruct((s, c), BF16)],
        scratch_shapes=[pltpu.VMEM((t + HALO, c), F32)],
        compiler_params=_params(("parallel",)), name=name,
    )(proj, proj, cw, cb)


def _neg_expm1(x):
    y = jnp.exp(x)
    ly = jnp.log(y)
    safe = jnp.where(ly == 0.0, 1.0, ly)
    em = jnp.where(y == 1.0, x, (y - 1.0) * x / safe)
    em = jnp.where(x < -30.0, -1.0, em)
    return -em


def _rg_gates(pre_ax, rec, ba, bx, lam):
    c = D_RNN_PAD
    r = jax.nn.sigmoid(pre_ax[:, :c] + ba)
    ig = jax.nn.sigmoid(pre_ax[:, c:] + bx)
    sp = jnp.maximum(-lam, 0.0) + jnp.log1p(jnp.exp(-jnp.abs(lam)))
    log_a = (-RG_C) * r * sp
    a = jnp.exp(log_a)
    m2 = _neg_expm1(2.0 * log_a)
    mult = jnp.sqrt(m2)
    return r, ig, sp, a, m2, mult


def _rg_scan_fwd(pre_ax, rec, proj, ba, bx, lam, name):
    s = rec.shape[0]
    c = D_RNN_PAD
    t = _tile(s, 256)

    def body(pax_ref, rec_ref, gate_ref, ba_ref, bx_ref, lam_ref, h_ref, gh_ref, a_scr, u_scr, carry):
        @pl.when(pl.program_id(0) == 0)
        def _():
            carry[...] = jnp.zeros_like(carry)

        recv = rec_ref[...]
        _, ig, _, a, _, mult = _rg_gates(pax_ref[...], recv, ba_ref[...], bx_ref[...], lam_ref[...])
        a_scr[...] = a
        u_scr[...] = mult * ig * recv
        rowid = lax.broadcasted_iota(jnp.int32, (8, c), 0)

        def group(gi, h):
            r0 = pl.multiple_of(gi * 8, 8)
            a8 = a_scr[pl.ds(r0, 8), :]
            u8 = u_scr[pl.ds(r0, 8), :]
            out = jnp.zeros((8, c), F32)
            for j in range(8):
                h = jnp.broadcast_to(a8[j:j + 1, :], (8, c)) * h + jnp.broadcast_to(u8[j:j + 1, :], (8, c))
                out = jnp.where(rowid == j, h, out)
            h_ref[pl.ds(r0, 8), :] = out
            return h

        carry[...] = lax.fori_loop(0, t // 8, group, carry[...])
        gh_ref[...] = (_gelu(gate_ref[...]) * h_ref[...]).astype(BF16)

    row = pl.BlockSpec((t, c), lambda i: (i, 0))
    vec = pl.BlockSpec((1, c), lambda i: (0, 0))
    return pl.pallas_call(
        body, grid=(s // t,),
        in_specs=[pl.BlockSpec((t, 2 * c), lambda i: (i, 0)), row, row, vec, vec, vec],
        out_specs=[row, row],
        out_shape=[jax.ShapeDtypeStruct((s, c), F32), jax.ShapeDtypeStruct((s, c), BF16)],
        scratch_shapes=[pltpu.VMEM((t, c), F32), pltpu.VMEM((t, c), F32), pltpu.VMEM((8, c), F32)],
        compiler_params=_params(("arbitrary",)), name=name,
    )(pre_ax, rec, proj, ba, bx, lam)


def _rg_scan_bwd(dgh, proj, h, pre_ax, rec, ba, bx, lam, name):
    s = rec.shape[0]
    c = D_RNN_PAD
    t = _tile(s, 256)
    nt = s // t

    def body(dgh_ref, gate_ref, h_ref, hprev_ref, pax_ref, rec_ref, ba_ref, bx_ref, lam_ref,
             dgate_ref, dpax_ref, drec_ref, dba_ref, dbx_ref, dlam_ref, a_scr, dh_scr, g_scr, hp_scr, carry):
        i = pl.program_id(0)
        ti = nt - 1 - i

        @pl.when(i == 0)
        def _():
            carry[...] = jnp.zeros_like(carry)
            dba_ref[...] = jnp.zeros_like(dba_ref)
            dbx_ref[...] = jnp.zeros_like(dbx_ref)
            dlam_ref[...] = jnp.zeros_like(dlam_ref)

        recv = rec_ref[...]
        lam = lam_ref[...]
        r, ig, sp, a, m2, mult = _rg_gates(pax_ref[...], recv, ba_ref[...], bx_ref[...], lam)
        gel, dgel = _gelu_and_grad(gate_ref[...])
        dghv = dgh_ref[...]
        hv = h_ref[...]
        dgate_ref[...] = (dghv * hv * dgel).astype(BF16)
        a_scr[...] = a
        dh_scr[...] = dghv * gel
        rowid = lax.broadcasted_iota(jnp.int32, (8, c), 0)

        def group(gi, cr):
            r0 = pl.multiple_of((t // 8 - 1 - gi) * 8, 8)
            a8 = a_scr[pl.ds(r0, 8), :]
            d8 = dh_scr[pl.ds(r0, 8), :]
            out = jnp.zeros((8, c), F32)
            for j in range(7, -1, -1):
                g = jnp.broadcast_to(d8[j:j + 1, :], (8, c)) + cr
                out = jnp.where(rowid == j, g, out)
                cr = jnp.broadcast_to(a8[j:j + 1, :], (8, c)) * g
            g_scr[pl.ds(r0, 8), :] = out
            return cr

        carry[...] = lax.fori_loop(0, t // 8, group, carry[...])
        g = g_scr[...]
        hp_scr[0:HALO, :] = jnp.where(ti > 0, hprev_ref[...], 0.0)
        hp_scr[HALO:HALO + t, :] = hv
        da = g * hp_scr[pl.ds(HALO - 1, t), :]
        dmult = g * ig * recv
        dig = g * mult * recv
        drec_ref[...] = g * mult * ig
        dm2 = dmult * 0.5 / mult
        dloga = da * a - 2.0 * (1.0 - m2) * dm2
        dr = dloga * ((-RG_C) * sp)
        dsp = jnp.sum(dloga * ((-RG_C) * r), axis=0, keepdims=True)
        dpa = dr * r * (1.0 - r)
        dpx = dig * ig * (1.0 - ig)
        dpax_ref[:, :c] = dpa.astype(BF16)
        dpax_ref[:, c:] = dpx.astype(BF16)
        dba_ref[...] += jnp.sum(dpa, axis=0, keepdims=True)
        dbx_ref[...] += jnp.sum(dpx, axis=0, keepdims=True)
        dlam_ref[...] += dsp * (-jnp.exp(-(lam + sp)))

    row = pl.BlockSpec((t, c), lambda i: (nt - 1 - i, 0))
    vec = pl.BlockSpec((1, c), lambda i: (0, 0))
    return pl.pallas_call(
        body, grid=(nt,),
        in_specs=[row, row, row,
                  pl.BlockSpec((HALO, c), lambda i: (jnp.maximum((nt - 1 - i) * (t // HALO) - 1, 0), 0)),
                  pl.BlockSpec((t, 2 * c), lambda i: (nt - 1 - i, 0)), row, vec, vec, vec],
        out_specs=[row, pl.BlockSpec((t, 2 * c), lambda i: (nt - 1 - i, 0)), row, vec, vec, vec],
        out_shape=[jax.ShapeDtypeStruct((s, c), BF16), jax.ShapeDtypeStruct((s, 2 * c), BF16),
                   jax.ShapeDtypeStruct((s, c), F32), jax.ShapeDtypeStruct((1, c), F32),
                   jax.ShapeDtypeStruct((1, c), F32), jax.ShapeDtypeStruct((1, c), F32)],
        scratch_shapes=[pltpu.VMEM((t, c), F32), pltpu.VMEM((t, c), F32), pltpu.VMEM((t, c), F32),
                        pltpu.VMEM((t + HALO, c), F32), pltpu.VMEM((8, c), F32)],
        compiler_params=_params(("arbitrary",)), name=name,
    )(dgh, proj, h, h, pre_ax, rec, ba, bx, lam)


def _rg_conv_bwd(drec_a, drec_b, proj, cw, name):
    s = drec_a.shape[0]
    c = D_RNN_PAD
    t = _tile(s, 256)
    nt = s // t
    kw = RG_CONV_W
    ext = t + HALO

    def body(da_ref, dan_ref, db_ref, dbn_ref, x_ref, prev_ref, w_ref, dx_ref, dw_ref, dbias_ref, xs, dus):
        i = pl.program_id(0)

        @pl.when(i == 0)
        def _():
            dw_ref[...] = jnp.zeros_like(dw_ref)
            dbias_ref[...] = jnp.zeros_like(dbias_ref)

        xs[0:HALO, :] = jnp.where(i > 0, prev_ref[...], 0.0)
        xs[HALO:HALO + t, :] = x_ref[...]
        du = da_ref[...] + db_ref[...]
        dus[0:t, :] = du
        dus[t:ext, :] = jnp.where(i < nt - 1, dan_ref[...] + dbn_ref[...], 0.0)
        dx = w_ref[kw - 1:kw, :] * du
        for k in range(kw - 1):
            dx = dx + w_ref[k:k + 1, :] * dus[pl.ds(kw - 1 - k, t), :]
        dx_ref[...] = dx.astype(BF16)
        dbias_ref[...] += jnp.sum(du, axis=0, keepdims=True)
        for k in range(kw):
            dw_ref[k:k + 1, :] += jnp.sum(du * xs[pl.ds(HALO - (kw - 1) + k, t), :], axis=0, keepdims=True)

    row = pl.BlockSpec((t, c), lambda i: (i, 0))
    nxt = pl.BlockSpec((HALO, c), lambda i: (jnp.minimum((i + 1) * (t // HALO), s // HALO - 1), 0))
    return pl.pallas_call(
        body, grid=(nt,),
        in_specs=[row, nxt, row, nxt,
                  pl.BlockSpec((t, c), lambda i: (i, 1)),
                  _prev_halo_spec(t, c, lambda i: 1),
                  pl.BlockSpec((kw, c), lambda i: (0, 0))],
        out_specs=[row, pl.BlockSpec((kw, c), lambda i: (0, 0)), pl.BlockSpec((1, c), lambda i: (0, 0))],
        out_shape=[jax.ShapeDtypeStruct((s, c), BF16), jax.ShapeDtypeStruct((kw, c), F32),
                   jax.ShapeDtypeStruct((1, c), F32)],
        scratch_shapes=[pltpu.VMEM((t + HALO, c), F32), pltpu.VMEM((ext, c), F32)],
        compiler_params=_params(("arbitrary",)), name=name,
    )(drec_a, drec_a, drec_b, drec_b, proj, proj, cw)


N_CHIP = 4


def _chip_exchange(operands, name):
    nop = len(operands)
    sames = [same for _, same in operands]

    def body(*refs):
        srcs = refs[:nop]
        outs = refs[nop:2 * nop]
        send_sems, recv_sems, local_sems = refs[2 * nop:]
        x, y, c = lax.axis_index("x"), lax.axis_index("y"), lax.axis_index("c")
        mine = 2 * x + y
        local = []
        for o in range(nop):
            cp = pltpu.make_async_copy(srcs[o] if sames[o] else srcs[o].at[mine], outs[o].at[mine], local_sems.at[o])
            cp.start()
            local.append(cp)
        copies = []
        for k in range(1, N_CHIP):
            px = (x + (k >> 1)) % 2
            py = (y + (k & 1)) % 2
            chip = 2 * px + py
            for o in range(nop):
                sem = (k - 1) * nop + o
                src = srcs[o] if sames[o] else srcs[o].at[chip]
                send = pltpu.make_async_remote_copy(
                    src_ref=src, dst_ref=outs[o].at[mine], send_sem=send_sems.at[sem], recv_sem=recv_sems.at[sem],
                    device_id=(px, py, c), device_id_type=pl.DeviceIdType.MESH)
                send.start()
                recv = pltpu.make_async_remote_copy(
                    src_ref=src, dst_ref=outs[o].at[chip], send_sem=send_sems.at[sem], recv_sem=recv_sems.at[sem],
                    device_id=(px, py, c), device_id_type=pl.DeviceIdType.MESH)
                copies.append((send, recv))
        for send, recv in copies:
            send.wait_send()
            recv.wait_recv()
        for cp in local:
            cp.wait()

    anyspec = pl.BlockSpec(memory_space=pl.ANY)
    shapes = [jax.ShapeDtypeStruct((N_CHIP,) + (arr.shape if same else arr.shape[1:]), arr.dtype)
              for arr, same in operands]
    return pl.pallas_call(
        body, in_specs=[anyspec] * nop, out_specs=[anyspec] * nop, out_shape=shapes,
        scratch_shapes=[pltpu.SemaphoreType.DMA(((N_CHIP - 1) * nop,)), pltpu.SemaphoreType.DMA(((N_CHIP - 1) * nop,)),
                        pltpu.SemaphoreType.DMA((nop,))],
        name=name,
    )(*[arr for arr, _ in operands])


def _core_swap(operands, name):
    nop = len(operands)
    wholes = [whole for _, whole in operands]

    def body(*refs):
        srcs = refs[:nop]
        outs = refs[nop:2 * nop]
        send_sems, recv_sems = refs[2 * nop:]
        x, y, c = lax.axis_index("x"), lax.axis_index("y"), lax.axis_index("c")
        other = 1 - c
        copies = []
        for o in range(nop):
            cp = pltpu.make_async_remote_copy(
                src_ref=srcs[o] if wholes[o] else srcs[o].at[other], dst_ref=outs[o],
                send_sem=send_sems.at[o], recv_sem=recv_sems.at[o],
                device_id=(x, y, other), device_id_type=pl.DeviceIdType.MESH)
            cp.start()
            copies.append(cp)
        for cp in copies:
            cp.wait()

    anyspec = pl.BlockSpec(memory_space=pl.ANY)
    shapes = [jax.ShapeDtypeStruct(arr.shape if whole else arr.shape[1:], arr.dtype) for arr, whole in operands]
    return pl.pallas_call(
        body, in_specs=[anyspec] * nop, out_specs=[anyspec] * nop, out_shape=shapes,
        scratch_shapes=[pltpu.SemaphoreType.DMA((nop,)), pltpu.SemaphoreType.DMA((nop,))],
        name=name,
    )(*[arr for arr, _ in operands])


def _pair_sum(mine2, theirs, name):
    _, nq, n, cols = mine2.shape
    t = _tile(n, 512, 16)

    def body(a_ref, b_ref, o_ref):
        c = lax.axis_index("c")
        o_ref[...] = (a_ref[c].astype(F32) + b_ref[...].astype(F32)).astype(o_ref.dtype)

    return pl.pallas_call(
        body, grid=(nq, n // t),
        in_specs=[pl.BlockSpec((2, None, t, cols), lambda q, i: (0, q, i, 0)),
                  pl.BlockSpec((None, t, cols), lambda q, i: (q, i, 0))],
        out_specs=pl.BlockSpec((None, t, cols), lambda q, i: (q, i, 0)),
        out_shape=jax.ShapeDtypeStruct(theirs.shape, theirs.dtype),
        compiler_params=_params(("parallel", "parallel")), name=name,
    )(mine2, theirs)


def _core_sum(mine, theirs, name):
    n, _ = mine.shape
    t = _tile(n, 1024)

    def body(a_ref, b_ref, o_ref):
        o_ref[...] = a_ref[...] + b_ref[...]

    row = pl.BlockSpec((t, LANES), lambda i: (i, 0))
    return pl.pallas_call(
        body, grid=(n // t,), in_specs=[row, row], out_specs=row,
        out_shape=jax.ShapeDtypeStruct(mine.shape, mine.dtype),
        compiler_params=_params(("parallel",)), name=name,
    )(mine, theirs)


def _adam(parts, w, m, v, name):
    n, cols = w.shape
    nparts = parts.shape[0]
    t = _tile(n, 512, 16)
    c1 = 1.0 - ADAM_B1 ** ADAM_STEP
    c2 = 1.0 - ADAM_B2 ** ADAM_STEP

    def body(p_ref, w_ref, m_ref, v_ref, g_ref, d_ref, mo_ref, vo_ref):
        g = p_ref[0].astype(F32)
        for k in range(1, nparts):
            g = g + p_ref[k].astype(F32)
        mn = ADAM_B1 * m_ref[...] + (1.0 - ADAM_B1) * g
        vn = ADAM_B2 * v_ref[...] + (1.0 - ADAM_B2) * (g * g)
        m_hat = mn / c1
        v_hat = vn / c2
        g_ref[...] = g
        d_ref[...] = (-ADAM_LR) * (m_hat / (jnp.sqrt(v_hat) + ADAM_EPS) + ADAM_WD * w_ref[...])
        mo_ref[...] = mn
        vo_ref[...] = vn

    row = pl.BlockSpec((t, cols), lambda i: (i, 0))
    out = jax.ShapeDtypeStruct((n, cols), F32)
    return pl.pallas_call(
        body, grid=(n // t,),
        in_specs=[pl.BlockSpec((nparts, t, cols), lambda i: (0, i, 0)), row, row, row],
        out_specs=[row, row, row, row], out_shape=[out, out, out, out],
        compiler_params=_params(("parallel",)), name=name,
    )(parts, w, m, v)


def _pack(pieces, row_mult, lead=False):
    if lead:
        flat = jnp.concatenate([p.reshape(N_DEV, -1) for p in pieces], axis=1)
        n = flat.shape[1]
    else:
        flat = jnp.concatenate([p.reshape(-1) for p in pieces])
        n = flat.shape[0]
    group = row_mult * LANES
    total = -(-n // group) * group
    if lead:
        flat = jnp.pad(flat, ((0, 0), (0, total - n)))
        return flat.reshape(N_DEV, total // LANES, LANES)
    flat = jnp.pad(flat, (0, total - n))
    return flat.reshape(total // LANES, LANES)


def _unpack(buf, shapes, lead=False):
    out = []
    off = 0
    flat = buf.reshape(N_DEV, -1) if lead else buf.reshape(-1)
    for shp in shapes:
        size = math.prod(shp)
        if lead:
            out.append(flat[:, off:off + size].reshape((N_DEV,) + tuple(shp)))
        else:
            out.append(flat[off:off + size].reshape(shp))
        off += size
    return out


BIG = ["attn_w_qkv", "attn_w_o", "rg_w_in", "rg_w_out", "ffn_w_up", "ffn_w_down"]
SMALL = ["rg_conv_w", "rg_conv_b", "rg_b_a", "rg_b_x", "rg_lambda", "ffn_conv_w"]
REPL = ["rg_w_a", "rg_w_x", "ffn_conv_b", "mix_pre_g", "mix_post_g", "ffn_pre_g", "ffn_post_g"]
ORDER = ["attn_w_qkv", "attn_w_o", "rg_w_in", "rg_conv_w", "rg_conv_b", "rg_w_a", "rg_b_a", "rg_w_x", "rg_b_x",
         "rg_lambda", "rg_w_out", "ffn_w_up", "ffn_conv_w", "ffn_conv_b", "ffn_w_down", "mix_pre_g",
         "mix_post_g", "ffn_pre_g", "ffn_post_g"]
SHARD_AXIS = {"attn_w_qkv": 2, "attn_w_o": 1, "rg_w_in": 2, "rg_w_out": 1, "ffn_w_up": 2, "ffn_w_down": 1,
              "rg_conv_w": 2, "rg_conv_b": 1, "rg_b_a": 1, "rg_b_x": 1, "rg_lambda": 1, "ffn_conv_w": 2}
SMALL_ROWS = 64
REPL_ROWS = 128


def _assemble(stacked, axis):
    moved = jnp.moveaxis(stacked, 0, axis)
    shp = list(moved.shape)
    shp[axis:axis + 2] = [shp[axis] * shp[axis + 1]]
    return moved.reshape(shp)


def _split(full, axis):
    shp = list(full.shape)
    shp[axis:axis + 1] = [N_DEV, shp[axis] // N_DEV]
    return jnp.moveaxis(full.reshape(shp), axis, 0)


def _block_diag_pad(w):
    out = jnp.zeros((D_RNN_PAD, D_RNN_PAD), w.dtype)
    for n in range(RG_BLOCKS):
        o = n * RG_BLOCK_W
        out = lax.dynamic_update_slice(out, w[n], (o, o))
    return out


def _block_diag_take(dense):
    return jnp.stack([dense[n * RG_BLOCK_W:(n + 1) * RG_BLOCK_W, n * RG_BLOCK_W:(n + 1) * RG_BLOCK_W]
                      for n in range(RG_BLOCKS)])


def _pad_cols(a, width):
    return jnp.pad(a, ((0, 0), (0, width - a.shape[1])))


def _interleave(a):
    lead = a.shape[:-1]
    nb = D_FF // FFN_COL
    return jnp.swapaxes(a.reshape(lead + (2, nb, FFN_COL)), -3, -2).reshape(lead + (2 * D_FF,))


def _deinterleave(a):
    lead = a.shape[:-1]
    nb = D_FF // FFN_COL
    return jnp.swapaxes(a.reshape(lead + (nb, 2, FFN_COL)), -3, -2).reshape(lead + (2 * D_FF,))


def _ffn_fwd(hn, wp, i):
    pre = _matmul(hn, wp["up"][i], name=f"ffn{i}_up")
    act = _ffn_act_fwd(pre, wp["ffn_cw"][i], wp["ffn_cb"][i], name=f"ffn{i}_act")
    f = _matmul(act, wp["down"][i], name=f"ffn{i}_down")
    return pre, act, f


def _ffn_bwd(df, hn, pre, act, wp, i):
    dact = _matmul(df, wp["down"][i], tb=True, name=f"ffn{i}_dact")
    d_down = _matmul(act, df, ta=True, name=f"ffn{i}_ddown")
    dpre, dcw, dcb = _ffn_act_bwd(pre, dact, wp["ffn_cw"][i], wp["ffn_cb"][i], name=f"ffn{i}_act_bwd")
    d_up = _matmul(hn, dpre, ta=True, name=f"ffn{i}_dup")
    dhn = _matmul(dpre, wp["up"][i], tb=True, name=f"ffn{i}_dhn")
    return dhn, d_up, d_down, dcw, dcb


def _local_step(x, tgt, wp):
    g = wp["gains"]
    gr = {}

    hn0 = _rms_fwd(x, g["mix_pre"][0], name="l0_mix_pre")
    qkv = _matmul(hn0, wp["qkv"], out_dtype=BF16, name="attn_qkv")
    o, lsum, first = _attn_fwd(qkv, name="attn_fwd")
    f0 = _matmul(o, wp["wo"], name="attn_out")
    x1, hn1 = _resid_norm(x, f0, g["mix_post"][0], g["ffn_pre"][0], name="l0_mix_post")
    pre0, act0, f1 = _ffn_fwd(hn1, wp, 0)
    x2, hn2 = _resid_norm(x1, f1, g["ffn_post"][0], g["mix_pre"][1], name="l0_ffn_post")
    proj = _matmul(hn2, wp["w_in"], name="rg_in")
    rec, rec_b = _rg_conv_fwd(proj, wp["rg_cw"], wp["rg_cb"], name="rg_conv")
    pre_ax = _matmul(rec_b, wp["w_ax"], name="rg_gates")
    h, gh = _rg_scan_fwd(pre_ax, rec, proj, wp["rg_ba"], wp["rg_bx"], wp["rg_lam"], name="rg_scan")
    f2 = _matmul(gh, wp["w_out"], name="rg_out")
    x3, hn3 = _resid_norm(x2, f2, g["mix_post"][1], g["ffn_pre"][1], name="l1_mix_post")
    pre1, act1, f3 = _ffn_fwd(hn3, wp, 1)
    dy, sq = _resid_loss(x3, f3, g["ffn_post"][1], tgt, name="l1_ffn_post_loss")

    ffn_dup, ffn_ddown, ffn_dcw, ffn_dcb = [None, None], [None, None], [None, None], [None, None]
    d_ffn_post, d_ffn_pre, d_mix_post, d_mix_pre = [None, None], [None, None], [None, None], [None, None]

    df3, d_ffn_post[1] = _rms_bwd(f3, g["ffn_post"][1], dy, None, BF16, name="l1_ffn_post_bwd")
    dhn3, ffn_dup[1], ffn_ddown[1], ffn_dcw[1], ffn_dcb[1] = _ffn_bwd(df3, hn3, pre1, act1, wp, 1)
    dx3, d_ffn_pre[1] = _rms_bwd(x3, g["ffn_pre"][1], dhn3, dy, F32, name="l1_ffn_pre_bwd")

    df2, d_mix_post[1] = _rms_bwd(f2, g["mix_post"][1], dx3, None, BF16, name="l1_mix_post_bwd")
    dgh = _matmul(df2, wp["w_out"], tb=True, name="rg_dgh")
    gr["w_out"] = _matmul(gh, df2, ta=True, name="rg_dwout")
    dgate, dpax, drec_u, gr["rg_ba"], gr["rg_bx"], gr["rg_lam"] = _rg_scan_bwd(
        dgh, proj, h, pre_ax, rec, wp["rg_ba"], wp["rg_bx"], wp["rg_lam"], name="rg_scan_bwd")
    drec_g = _matmul(dpax, wp["w_ax"], tb=True, name="rg_drec")
    gr["w_ax"] = _matmul(rec_b, dpax, ta=True, name="rg_dwax")
    dproj_rec, gr["rg_cw"], gr["rg_cb"] = _rg_conv_bwd(drec_u, drec_g, proj, wp["rg_cw"], name="rg_conv_bwd")
    dproj = jnp.concatenate([dgate, dproj_rec], axis=1)
    gr["w_in"] = _matmul(hn2, dproj, ta=True, name="rg_dwin")
    dhn2 = _matmul(dproj, wp["w_in"], tb=True, name="rg_dhn")
    dx2, d_mix_pre[1] = _rms_bwd(x2, g["mix_pre"][1], dhn2, dx3, F32, name="l1_mix_pre_bwd")

    df1, d_ffn_post[0] = _rms_bwd(f1, g["ffn_post"][0], dx2, None, BF16, name="l0_ffn_post_bwd")
    dhn1, ffn_dup[0], ffn_ddown[0], ffn_dcw[0], ffn_dcb[0] = _ffn_bwd(df1, hn1, pre0, act0, wp, 0)
    dx1, d_ffn_pre[0] = _rms_bwd(x1, g["ffn_pre"][0], dhn1, dx2, F32, name="l0_ffn_pre_bwd")

    df0, d_mix_post[0] = _rms_bwd(f0, g["mix_post"][0], dx1, None, BF16, name="l0_mix_post_bwd")
    do = _matmul(df0, wp["wo"], tb=True, out_dtype=BF16, name="attn_do")
    gr["wo"] = _matmul(o, df0, ta=True, name="attn_dwo")
    dq, dk, dv = _attn_bwd(qkv, do, lsum, first, name="attn_bwd")
    dqkv = jnp.concatenate([dq, dk.astype(BF16), dv.astype(BF16)], axis=1)
    gr["qkv"] = _matmul(hn0, dqkv, ta=True, name="attn_dwqkv")
    dhn0 = _matmul(dqkv, wp["qkv"], tb=True, name="attn_dhn")
    dx0, d_mix_pre[0] = _rms_bwd(x, g["mix_pre"][0], dhn0, dx1, F32, name="l0_mix_pre_bwd")

    gr["up"] = ffn_dup
    gr["down"] = ffn_ddown
    gr["ffn_cw"] = ffn_dcw
    gr["ffn_cb"] = ffn_dcb
    gr["gains"] = {"mix_pre": d_mix_pre, "mix_post": d_mix_post, "ffn_pre": d_ffn_pre, "ffn_post": d_ffn_post}
    return sq[0, 0], dx0, gr


def _prepare(full):
    c, cp = D_RNN, D_RNN_PAD
    w_in = full["rg_w_in"][0]
    w_a = _block_diag_pad(full["rg_w_a"][0].astype(BF16))
    w_x = _block_diag_pad(full["rg_w_x"][0].astype(BF16))
    vec = lambda a: _pad_cols(a.reshape(1, c), cp)
    return {
        "qkv": full["attn_w_qkv"][0],
        "wo": full["attn_w_o"][0],
        "w_in": jnp.concatenate([_pad_cols(w_in[:, :c], cp), _pad_cols(w_in[:, c:], cp)], axis=1),
        "w_ax": jnp.concatenate([w_a, w_x], axis=1),
        "w_out": jnp.pad(full["rg_w_out"][0], ((0, cp - c), (0, 0))),
        "rg_cw": _pad_cols(full["rg_conv_w"][0], cp),
        "rg_cb": vec(full["rg_conv_b"][0]),
        "rg_ba": vec(full["rg_b_a"][0]),
        "rg_bx": vec(full["rg_b_x"][0]),
        "rg_lam": vec(full["rg_lambda"][0]),
        "up": [_interleave(full["ffn_w_up"][i]) for i in range(2)],
        "down": [full["ffn_w_down"][i] for i in range(2)],
        "ffn_cw": [_interleave(full["ffn_conv_w"][i]) for i in range(2)],
        "ffn_cb": [_interleave(full["ffn_conv_b"][i].reshape(1, -1)) for i in range(2)],
        "gains": {k: [full[k + "_g"][i].reshape(1, D_MODEL) for i in range(2)]
                  for k in ("mix_pre", "mix_post", "ffn_pre", "ffn_post")},
    }


def _natural_grads(gr):
    c, cp = D_RNN, D_RNN_PAD
    gg = gr["gains"]
    stack2 = lambda pair: jnp.stack([pair[0].reshape(-1), pair[1].reshape(-1)])
    return {
        "attn_w_qkv": gr["qkv"][None],
        "attn_w_o": gr["wo"][None],
        "rg_w_in": jnp.concatenate([gr["w_in"][:, :c], gr["w_in"][:, cp:cp + c]], axis=1)[None],
        "rg_conv_w": gr["rg_cw"][:, :c][None],
        "rg_conv_b": gr["rg_cb"][:, :c],
        "rg_w_a": _block_diag_take(gr["w_ax"][:, :cp])[None],
        "rg_b_a": gr["rg_ba"][:, :c],
        "rg_w_x": _block_diag_take(gr["w_ax"][:, cp:])[None],
        "rg_b_x": gr["rg_bx"][:, :c],
        "rg_lambda": gr["rg_lam"][:, :c],
        "rg_w_out": gr["w_out"][:c][None],
        "ffn_w_up": jnp.stack([_deinterleave(gr["up"][i]) for i in range(2)]),
        "ffn_conv_w": jnp.stack([_deinterleave(gr["ffn_cw"][i]) for i in range(2)]),
        "ffn_conv_b": jnp.stack([_deinterleave(gr["ffn_cb"][i]).reshape(-1) for i in range(2)]),
        "ffn_w_down": jnp.stack(gr["down"]),
        "mix_pre_g": stack2(gg["mix_pre"]),
        "mix_post_g": stack2(gg["mix_post"]),
        "ffn_pre_g": stack2(gg["ffn_pre"]),
        "ffn_post_g": stack2(gg["ffn_post"]),
    }


def _rows(a):
    return a.reshape(-1, a.shape[-1])


def _step(x, loss_target, w, m, v):
    small_shapes = [w[n].shape for n in SMALL]
    repl_shapes = [w[n].shape for n in REPL]
    wsmall = _pack([w[n] for n in SMALL], SMALL_ROWS)
    on_chips = _chip_exchange([(_rows(w[n].astype(BF16)), True) for n in BIG] + [(wsmall, True)],
                              name="gather_weights_chips")
    from_sibling = _core_swap([(a, True) for a in on_chips], name="gather_weights_cores")
    south = lax.axis_index("c") == 0

    def by_device(mine, theirs):
        pair = jnp.stack([jnp.where(south, mine, theirs), jnp.where(south, theirs, mine)], axis=1)
        return pair.reshape((N_DEV,) + mine.shape[1:])

    full = {}
    for n, a, b in zip(BIG, on_chips, from_sibling):
        full[n] = _assemble(by_device(a, b).reshape((N_DEV,) + w[n].shape), SHARD_AXIS[n])
    for n, st in zip(SMALL, _unpack(by_device(on_chips[-1], from_sibling[-1]), small_shapes, lead=True)):
        full[n] = _assemble(st, SHARD_AXIS[n])
    for n in REPL:
        full[n] = w[n]

    sq, dx, gr = _local_step(x[0], loss_target[0], _prepare(full))
    grads = _natural_grads(gr)

    by_core = lambda a: jnp.swapaxes(a.reshape((N_CHIP, 2) + a.shape[1:]), 0, 1)
    pbig = [by_core(_split(grads[n], SHARD_AXIS[n]).astype(BF16).reshape((N_DEV,) + _rows(w[n]).shape)) for n in BIG]
    psmall = by_core(_pack([_split(grads[n], SHARD_AXIS[n]) for n in SMALL], SMALL_ROWS, lead=True))
    prepl = _pack([grads[n] for n in REPL], REPL_ROWS)
    swapped = _core_swap([(p, False) for p in pbig] + [(psmall, False), (prepl, True)], name="reduce_grads_cores")
    sums = [_pair_sum(p, o, name="chip_sum_" + n) for n, p, o in zip(BIG, pbig, swapped)]
    sums.append(_pair_sum(psmall, swapped[-2], name="chip_sum_small"))
    trepl = _core_sum(prepl, swapped[-1], name="chip_sum_repl")
    reduced = _chip_exchange([(t, False) for t in sums] + [(trepl, True)], name="reduce_grads_chips")

    out = {}
    kinds = ("grad", "delta", "new_m", "new_v")
    for n, parts in zip(BIG, reduced):
        res = _adam(parts, _rows(w[n]), _rows(m[n]), _rows(v[n]), name="adamw_" + n)
        for kind, buf in zip(kinds, res):
            out[(kind, n)] = buf.reshape(w[n].shape)
    for names, parts, rows, shapes, tag in ((SMALL, reduced[-2], SMALL_ROWS, small_shapes, "small"),
                                            (REPL, reduced[-1], REPL_ROWS, repl_shapes, "repl")):
        packs = [_pack([d[n] for n in names], rows) for d in (w, m, v)]
        res = _adam(parts, *packs, name="adamw_" + tag)
        for kind, buf in zip(kinds, res):
            for n, a in zip(names, _unpack(buf, shapes)):
                out[(kind, n)] = a

    loss = lax.psum(sq * (0.5 / D_MODEL), ("x", "y", "c"))
    return loss, dx[None], out


def kernel(x, attn_w_qkv, attn_w_o, rg_w_in, rg_conv_w, rg_conv_b, rg_w_a, rg_b_a, rg_w_x, rg_b_x, rg_lambda, rg_w_out, ffn_w_up, ffn_conv_w, ffn_conv_b, ffn_w_down, mix_pre_g, mix_post_g, ffn_pre_g, ffn_post_g, loss_target, m_attn_w_qkv, m_attn_w_o, m_rg_w_in, m_rg_conv_w, m_rg_conv_b, m_rg_w_a, m_rg_b_a, m_rg_w_x, m_rg_b_x, m_rg_lambda, m_rg_w_out, m_ffn_w_up, m_ffn_conv_w, m_ffn_conv_b, m_ffn_w_down, m_mix_pre_g, m_mix_post_g, m_ffn_pre_g, m_ffn_post_g, v_attn_w_qkv, v_attn_w_o, v_rg_w_in, v_rg_conv_w, v_rg_conv_b, v_rg_w_a, v_rg_b_a, v_rg_w_x, v_rg_b_x, v_rg_lambda, v_rg_w_out, v_ffn_w_up, v_ffn_conv_w, v_ffn_conv_b, v_ffn_w_down, v_mix_pre_g, v_mix_post_g, v_ffn_pre_g, v_ffn_post_g):
    w = dict(zip(ORDER, (attn_w_qkv, attn_w_o, rg_w_in, rg_conv_w, rg_conv_b, rg_w_a, rg_b_a, rg_w_x, rg_b_x,
                         rg_lambda, rg_w_out, ffn_w_up, ffn_conv_w, ffn_conv_b, ffn_w_down, mix_pre_g,
                         mix_post_g, ffn_pre_g, ffn_post_g)))
    m = dict(zip(ORDER, (m_attn_w_qkv, m_attn_w_o, m_rg_w_in, m_rg_conv_w, m_rg_conv_b, m_rg_w_a, m_rg_b_a,
                         m_rg_w_x, m_rg_b_x, m_rg_lambda, m_rg_w_out, m_ffn_w_up, m_ffn_conv_w, m_ffn_conv_b,
                         m_ffn_w_down, m_mix_pre_g, m_mix_post_g, m_ffn_pre_g, m_ffn_post_g)))
    v = dict(zip(ORDER, (v_attn_w_qkv, v_attn_w_o, v_rg_w_in, v_rg_conv_w, v_rg_conv_b, v_rg_w_a, v_rg_b_a,
                         v_rg_w_x, v_rg_b_x, v_rg_lambda, v_rg_w_out, v_ffn_w_up, v_ffn_conv_w, v_ffn_conv_b,
                         v_ffn_w_down, v_mix_pre_g, v_mix_post_g, v_ffn_pre_g, v_ffn_post_g)))
    loss, dx, out = _step(x, loss_target, w, m, v)
    return (loss, dx, *[out[("grad", n)] for n in ORDER], *[out[("delta", n)] for n in ORDER],
            *[out[("new_m", n)] for n in ORDER], *[out[("new_v", n)] for n in ORDER])
```

```python
import functools
import math

import jax
import jax.numpy as jnp
from jax import lax
from jax.experimental import pallas as pl
from jax.experimental.pallas import tpu as pltpu

F32 = jnp.float32
BF16 = jnp.bfloat16

D_MODEL = 1024
N_HEADS = 16
HEAD_DIM = 64
D_RNN = 1344
D_RNN_PAD = 1408
RG_BLOCKS = 16
RG_BLOCK_W = 84
RG_CONV_W = 4
RG_C = 8.0
D_FF = 2816
FFN_CONV_W = 3
FFN_COL = 256
NORM_EPS = 1e-6
N_DEV = 8
LANES = 128
HALO = 8
STRIP = 32

ADAM_LR = 0.001
ADAM_B1 = 0.9
ADAM_B2 = 0.999
ADAM_EPS = 1e-08
ADAM_WD = 0.01
ADAM_STEP = 10

ATT_BLK = 256
ATT_HEADS = 4
ATT_STOP_LOG = -110.0
VMEM_LIMIT = 56 * 1024 * 1024

_GELU_C = math.sqrt(2.0 / math.pi)


def _params(sem=None):
    return pltpu.CompilerParams(dimension_semantics=sem, vmem_limit_bytes=VMEM_LIMIT)


def _tile(n, cap, mult=8):
    if n <= cap:
        return n
    best = None
    for t in range(mult, cap + 1, mult):
        if n % t == 0:
            best = t
    assert best is not None, (n, cap, mult)
    return best


def _gelu_and_grad(x):
    x2 = x * x
    th = jnp.tanh(_GELU_C * (x + 0.044715 * (x2 * x)))
    cdf = 0.5 * (1.0 + th)
    gel = x * cdf
    dgel = cdf + 0.5 * x * (1.0 - th * th) * (_GELU_C * (1.0 + 3.0 * 0.044715 * x2))
    return gel, dgel


def _gelu(x):
    th = jnp.tanh(_GELU_C * (x + 0.044715 * (x * x * x)))
    return x * (0.5 * (1.0 + th))


def _matmul(a, b, *, ta=False, tb=False, out_dtype=F32, name):
    if ta:
        kdim, m = a.shape
    else:
        m, kdim = a.shape
    if tb:
        n, kb_ = b.shape
    else:
        kb_, n = b.shape
    assert kdim == kb_, (a.shape, b.shape, ta, tb)
    wide = lambda d: 1408 if d % 1408 == 0 else 1024
    tn = _tile(n, wide(n), 128)
    tk = _tile(kdim, 1024, 128) if ta else _tile(kdim, wide(kdim), 128)
    nk = kdim // tk
    tm = _tile(m, wide(m), 128) if ta else _tile(m, 2048 if nk == 1 else 1024)
    dims = (((0 if ta else 1,), (1 if tb else 0,)), ((), ()))

    def body(a_ref, b_ref, o_ref, *scratch):
        part = lax.dot_general(a_ref[...], b_ref[...], dims, preferred_element_type=F32)
        if nk == 1:
            o_ref[...] = part.astype(out_dtype)
        else:
            (acc_ref,) = scratch
            k = pl.program_id(2)

            @pl.when(k == 0)
            def _():
                acc_ref[...] = part

            @pl.when(k > 0)
            def _():
                acc_ref[...] += part

            @pl.when(k == nk - 1)
            def _():
                o_ref[...] = acc_ref[...].astype(out_dtype)

    a_spec = pl.BlockSpec((tk, tm), lambda i, j, k: (k, i)) if ta else pl.BlockSpec((tm, tk), lambda i, j, k: (i, k))
    b_spec = pl.BlockSpec((tn, tk), lambda i, j, k: (j, k)) if tb else pl.BlockSpec((tk, tn), lambda i, j, k: (k, j))
    return pl.pallas_call(
        body,
        grid=(m // tm, n // tn, nk),
        in_specs=[a_spec, b_spec],
        out_specs=pl.BlockSpec((tm, tn), lambda i, j, k: (i, j)),
        out_shape=jax.ShapeDtypeStruct((m, n), out_dtype),
        scratch_shapes=[pltpu.VMEM((tm, tn), F32)] if nk > 1 else [],
        compiler_params=_params(("parallel", "parallel", "arbitrary")),
        name=name,
    )(a, b)


def _rms(xv, g):
    inv = lax.rsqrt(jnp.mean(xv * xv, axis=-1, keepdims=True) + NORM_EPS)
    return xv * inv * g


def _rms_fwd(x, g, name):
    s, d = x.shape
    t = _tile(s, 1024)

    def body(x_ref, g_ref, o_ref):
        o_ref[...] = _rms(x_ref[...], g_ref[...]).astype(BF16)

    return pl.pallas_call(
        body, grid=(s // t,),
        in_specs=[pl.BlockSpec((t, d), lambda i: (i, 0)), pl.BlockSpec((1, d), lambda i: (0, 0))],
        out_specs=pl.BlockSpec((t, d), lambda i: (i, 0)),
        out_shape=jax.ShapeDtypeStruct((s, d), BF16),
        compiler_params=_params(("parallel",)), name=name,
    )(x, g)


def _resid_norm(x, f, post_g, pre_g, name):
    s, d = x.shape
    t = _tile(s, 1024)

    def body(x_ref, f_ref, pg_ref, ng_ref, xo_ref, hn_ref):
        xn = x_ref[...] + _rms(f_ref[...], pg_ref[...])
        xo_ref[...] = xn
        hn_ref[...] = _rms(xn, ng_ref[...]).astype(BF16)

    row = pl.BlockSpec((t, d), lambda i: (i, 0))
    vec = pl.BlockSpec((1, d), lambda i: (0, 0))
    return pl.pallas_call(
        body, grid=(s // t,), in_specs=[row, row, vec, vec], out_specs=[row, row],
        out_shape=[jax.ShapeDtypeStruct((s, d), F32), jax.ShapeDtypeStruct((s, d), BF16)],
        compiler_params=_params(("parallel",)), name=name,
    )(x, f, post_g, pre_g)


def _resid_loss(x, f, post_g, tgt, name):
    s, d = x.shape
    t = _tile(s, 1024)

    def body(x_ref, f_ref, pg_ref, t_ref, dy_ref, loss_ref):
        @pl.when(pl.program_id(0) == 0)
        def _():
            loss_ref[...] = jnp.zeros_like(loss_ref)

        err = x_ref[...] + _rms(f_ref[...], pg_ref[...]) - t_ref[...]
        dy_ref[...] = err * (1.0 / d)
        sq = jnp.sum(jnp.sum(err * err, axis=1, keepdims=True), axis=0, keepdims=True)
        loss_ref[...] += sq

    row = pl.BlockSpec((t, d), lambda i: (i, 0))
    vec = pl.BlockSpec((1, d), lambda i: (0, 0))
    return pl.pallas_call(
        body, grid=(s // t,), in_specs=[row, row, vec, row],
        out_specs=[row, pl.BlockSpec((8, LANES), lambda i: (0, 0))],
        out_shape=[jax.ShapeDtypeStruct((s, d), F32), jax.ShapeDtypeStruct((8, LANES), F32)],
        compiler_params=_params(("arbitrary",)), name=name,
    )(x, f, post_g, tgt)


def _rms_bwd(xin, g, dy, add, out_dtype, name):
    s, d = xin.shape
    t = _tile(s, 1024)
    has_add = add is not None

    def body(*refs):
        if has_add:
            x_ref, g_ref, dy_ref, add_ref, dx_ref, dg_ref = refs
        else:
            x_ref, g_ref, dy_ref, dx_ref, dg_ref = refs

        @pl.when(pl.program_id(0) == 0)
        def _():
            dg_ref[...] = jnp.zeros_like(dg_ref)

        xv = x_ref[...]
        dyv = dy_ref[...].astype(F32)
        inv = lax.rsqrt(jnp.mean(xv * xv, axis=-1, keepdims=True) + NORM_EPS)
        xh = xv * inv
        gy = dyv * g_ref[...]
        dx = inv * (gy - xh * jnp.mean(xh * gy, axis=-1, keepdims=True))
        if has_add:
            dx = dx + add_ref[...]
        dx_ref[...] = dx.astype(out_dtype)
        dg_ref[...] += jnp.sum(dyv * xh, axis=0, keepdims=True)

    row = pl.BlockSpec((t, d), lambda i: (i, 0))
    vec = pl.BlockSpec((1, d), lambda i: (0, 0))
    args = [xin, g, dy] + ([add] if has_add else [])
    specs = [row, vec, row] + ([row] if has_add else [])
    return pl.pallas_call(
        body, grid=(s // t,), in_specs=specs, out_specs=[row, vec],
        out_shape=[jax.ShapeDtypeStruct((s, d), out_dtype), jax.ShapeDtypeStruct((1, d), F32)],
        compiler_params=_params(("arbitrary",)), name=name,
    )(*args)


_NT = (((1,), (1,)), ((), ()))
_NN = (((1,), (0,)), ((), ()))
_TN = (((0,), (0,)), ((), ()))


def _tri_dot(x, tri):
    hi = x.astype(BF16)
    lo = (x - hi.astype(F32)).astype(BF16)
    out = lax.dot_general(hi, tri, _NN, preferred_element_type=F32)
    return out + lax.dot_general(lo, tri, _NN, preferred_element_type=F32)


def _log_sigmoids(z):
    lb = jnp.minimum(z, 0.0) - jnp.log(1.0 + jnp.exp(-jnp.abs(z)))
    return lb, lb - z


def _att_blk(s):
    return min(ATT_BLK, s)


def _attn_fwd(qkv, name):
    s = qkv.shape[0]
    blk = _att_blk(s)
    nq = s // blk
    nh = ATT_HEADS
    width = nh * HEAD_DIM
    ngrp = N_HEADS // nh
    hs = range(nh)
    scale = HEAD_DIM ** -0.5

    def body(q_ref, k_ref, v_ref, o_ref, l_ref, first_ref):
        qi = pl.program_id(1)
        row = lax.broadcasted_iota(jnp.int32, (blk, blk), 0)
        col = lax.broadcasted_iota(jnp.int32, (blk, blk), 1)
        causal = col < row
        tri = (row > col).astype(BF16)
        lane_head = lax.broadcasted_iota(jnp.int32, (blk, width), 1) // HEAD_DIM
        masks = [lane_head == hh for hh in hs]
        qs = q_ref[...] * scale
        qh = [jnp.where(m, qs, jnp.zeros_like(qs)) for m in masks]

        def block(kb, carry, diag):
            rsums, acc = carry
            r0 = pl.multiple_of(kb * blk, blk)
            k2 = k_ref[pl.ds(r0, blk), :]
            v2 = v_ref[pl.ds(r0, blk), :]
            z = [lax.dot_general(qh[hh], k2, _NT, preferred_element_type=F32) for hh in hs]
            lbm = [_log_sigmoids(z[hh]) for hh in hs]
            lb = [p[0] for p in lbm]
            lm = [jnp.where(causal, p[1], 0.0) if diag else p[1] for p in lbm]
            cs = [_tri_dot(lm[hh], tri) for hh in hs]
            w = [jnp.exp(lb[hh] + cs[hh] + rsums[hh]) for hh in hs]
            if diag:
                w = [jnp.where(causal, w[hh], 0.0) for hh in hs]
            for hh in hs:
                vh = jnp.where(masks[hh], v2, jnp.zeros_like(v2))
                acc = acc + lax.dot_general(w[hh].astype(BF16), vh, _NN, preferred_element_type=F32)
            return tuple(rsums[hh] + (cs[hh][:, 0:1] + lm[hh][:, 0:1]) for hh in hs), acc

        def largest(rsums):
            m = jnp.max(rsums[0])
            for hh in range(1, nh):
                m = jnp.maximum(m, jnp.max(rsums[hh]))
            return m

        def more(c):
            return jnp.logical_and(c[0] < qi, c[1] > ATT_STOP_LOG)

        def step(c):
            rsums, acc = block(qi - 1 - c[0], (c[2], c[3]), False)
            return c[0] + 1, largest(rsums), rsums, acc

        zero = jnp.zeros((blk, 1), F32)
        rsums, acc = block(qi, ((zero,) * nh, jnp.zeros((blk, width), F32)), True)
        done, _, rsums, acc = lax.while_loop(more, step, (jnp.int32(0), largest(rsums), rsums, acc))
        first_ref[pl.program_id(0) * nq + qi] = (qi - done).astype(F32)
        o_ref[...] = acc.astype(BF16)
        lout = rsums[nh - 1]
        for hh in range(nh - 2, -1, -1):
            lout = jnp.where(masks[hh], rsums[hh], lout)
        l_ref[...] = lout

    return pl.pallas_call(
        body, grid=(ngrp, nq),
        in_specs=[pl.BlockSpec((blk, width), lambda h, i: (i, h)),
                  pl.BlockSpec((s, width), lambda h, i: (0, ngrp + h)),
                  pl.BlockSpec((s, width), lambda h, i: (0, 2 * ngrp + h))],
        out_specs=[pl.BlockSpec((blk, width), lambda h, i: (i, h)), pl.BlockSpec((blk, width), lambda h, i: (i, h)),
                   pl.BlockSpec(memory_space=pltpu.SMEM)],
        out_shape=[jax.ShapeDtypeStruct((s, D_MODEL), BF16), jax.ShapeDtypeStruct((s, D_MODEL), F32),
                   jax.ShapeDtypeStruct((ngrp * nq,), F32)],
        compiler_params=_params(("arbitrary", "arbitrary")), name=name,
    )(qkv, qkv, qkv)


def _attn_bwd(qkv, do, lsum, first, name):
    s = qkv.shape[0]
    blk = _att_blk(s)
    nq = s // blk
    nh = ATT_HEADS
    width = nh * HEAD_DIM
    ngrp = N_HEADS // nh
    hs = range(nh)
    scale = HEAD_DIM ** -0.5

    def body(first_ref, q_ref, k_ref, v_ref, do_ref, l_ref, dq_ref, dk_ref, dv_ref):
        qi = pl.program_id(1)
        start = first_ref[pl.program_id(0) * nq + qi].astype(jnp.int32)

        @pl.when(qi == 0)
        def _():
            dk_ref[...] = jnp.zeros_like(dk_ref)
            dv_ref[...] = jnp.zeros_like(dv_ref)

        row = lax.broadcasted_iota(jnp.int32, (blk, blk), 0)
        col = lax.broadcasted_iota(jnp.int32, (blk, blk), 1)
        causal = col < row
        tri_in = (row <= col).astype(BF16)
        tri_ex = (row < col).astype(BF16)
        lane_head = lax.broadcasted_iota(jnp.int32, (blk, width), 1) // HEAD_DIM
        masks = [lane_head == hh for hh in hs]
        q2 = q_ref[...]
        qs = q2 * scale
        do2 = do_ref[...]
        l2 = l_ref[...]
        qh = [jnp.where(m, q2, jnp.zeros_like(q2)) for m in masks]
        qsh = [jnp.where(m, qs, jnp.zeros_like(qs)) for m in masks]
        doh = [jnp.where(m, do2, jnp.zeros_like(do2)) for m in masks]
        ltot = [l2[:, hh * HEAD_DIM:hh * HEAD_DIM + 1] for hh in hs]
        last = slice(blk - 1, blk)

        def block(kb, carry, diag):
            r0 = pl.multiple_of(kb * blk, blk)
            k2 = k_ref[pl.ds(r0, blk), :]
            v2 = v_ref[pl.ds(r0, blk), :]
            dq, pre_lm, pre_da = carry
            mask = (lambda a: jnp.where(causal, a, 0.0)) if diag else (lambda a: a)
            z = [lax.dot_general(qsh[hh], k2, _NT, preferred_element_type=F32) for hh in hs]
            dw = [lax.dot_general(doh[hh], v2, _NT, preferred_element_type=F32) for hh in hs]
            lbm = [_log_sigmoids(z[hh]) for hh in hs]
            lb = [p[0] for p in lbm]
            lm = [mask(p[1]) for p in lbm]
            pin = [_tri_dot(lm[hh], tri_in) for hh in hs]
            w = [mask(jnp.exp(lb[hh] + (ltot[hh] - pre_lm[hh]) - pin[hh])) for hh in hs]
            da = [w[hh] * dw[hh] for hh in hs]
            cex = [_tri_dot(da[hh], tri_ex) for hh in hs]
            dzb = [(mask(da[hh] - (da[hh] + (cex[hh] + pre_da[hh])) * jnp.exp(lb[hh])) * scale).astype(BF16)
                   for hh in hs]
            dk_blk = jnp.zeros((blk, width), F32)
            dv_blk = jnp.zeros((blk, width), F32)
            for hh in hs:
                kh = jnp.where(masks[hh], k2, jnp.zeros_like(k2))
                dq = dq + lax.dot_general(dzb[hh], kh, _NN, preferred_element_type=F32)
                dk_blk = dk_blk + lax.dot_general(dzb[hh], qh[hh], _TN, preferred_element_type=F32)
                dv_blk = dv_blk + lax.dot_general(w[hh].astype(BF16), doh[hh], _TN, preferred_element_type=F32)
            dk_ref[pl.ds(r0, blk), :] += dk_blk
            dv_ref[pl.ds(r0, blk), :] += dv_blk
            return (dq, tuple(pre_lm[hh] + pin[hh][:, last] for hh in hs),
                    tuple(pre_da[hh] + (cex[hh][:, last] + da[hh][:, last]) for hh in hs))

        zero = jnp.zeros((blk, 1), F32)
        init = (jnp.zeros((blk, width), F32), (zero,) * nh, (zero,) * nh)
        carry = lax.fori_loop(start, qi, lambda i, c: block(i, c, False), init)
        dq, _, _ = block(qi, carry, True)
        dq_ref[...] = dq.astype(BF16)

    qblk = pl.BlockSpec((blk, width), lambda h, i: (i, h))
    once = pl.Buffered(1)
    full = pl.BlockSpec((s, width), lambda h, i: (0, h), pipeline_mode=once)
    return pl.pallas_call(
        body, grid=(ngrp, nq),
        in_specs=[pl.BlockSpec(memory_space=pltpu.SMEM), qblk,
                  pl.BlockSpec((s, width), lambda h, i: (0, ngrp + h), pipeline_mode=once),
                  pl.BlockSpec((s, width), lambda h, i: (0, 2 * ngrp + h), pipeline_mode=once),
                  qblk, qblk],
        out_specs=[qblk, full, full],
        out_shape=[jax.ShapeDtypeStruct((s, D_MODEL), BF16), jax.ShapeDtypeStruct((s, D_MODEL), F32),
                   jax.ShapeDtypeStruct((s, D_MODEL), F32)],
        compiler_params=_params(("arbitrary", "arbitrary")), name=name,
    )(first, qkv, qkv, qkv, do, lsum)


def _prev_halo_spec(t, c, col):
    return pl.BlockSpec((HALO, c), lambda *g: (jnp.maximum(g[0] * (t // HALO) - 1, 0), col(*g)))


def _conv_taps(scr, w_ref, b_ref, kw, rows, lead):
    out = b_ref[...] + w_ref[kw - 1:kw, :] * scr[pl.ds(lead, rows), :]
    for k in range(kw - 1):
        out = out + w_ref[k:k + 1, :] * scr[pl.ds(lead - (kw - 1) + k, rows), :]
    return out


def _ffn_act_fwd(pre, cw, cb, name):
    s, c2 = pre.shape
    t = _tile(s, 1024)
    cb2 = 2 * FFN_COL
    ncol = c2 // cb2
    kw = FFN_CONV_W

    def body(x_ref, halo_ref, w_ref, b_ref, o_ref, scr):
        i = pl.program_id(0)
        scr[0:HALO, :] = jnp.where(i > 0, halo_ref[...], 0.0)
        scr[HALO:HALO + t, :] = x_ref[...]
        w = [w_ref[k:k + 1, :] for k in range(kw)]
        b = b_ref[...]
        for r0 in range(0, t, STRIP):
            rows = min(STRIP, t - r0)
            u = b + w[kw - 1] * scr[HALO + r0:HALO + r0 + rows, :]
            for k in range(kw - 1):
                lo = HALO - (kw - 1) + k + r0
                u = u + w[k] * scr[lo:lo + rows, :]
            o_ref[r0:r0 + rows, :] = (_gelu(u[:, :FFN_COL]) * u[:, FFN_COL:]).astype(BF16)

    return pl.pallas_call(
        body, grid=(s // t, ncol),
        in_specs=[pl.BlockSpec((t, cb2), lambda i, j: (i, j)),
                  _prev_halo_spec(t, cb2, lambda i, j: j),
                  pl.BlockSpec((FFN_CONV_W, cb2), lambda i, j: (0, j)),
                  pl.BlockSpec((1, cb2), lambda i, j: (0, j))],
        out_specs=pl.BlockSpec((t, FFN_COL), lambda i, j: (i, j)),
        out_shape=jax.ShapeDtypeStruct((s, c2 // 2), BF16),
        scratch_shapes=[pltpu.VMEM((t + HALO, cb2), F32)],
        compiler_params=_params(("parallel", "parallel")), name=name,
    )(pre, pre, cw, cb)


def _ffn_act_bwd(pre, dact, cw, cb, name):
    s, c2 = pre.shape
    t = _tile(s, 1024)
    cb2 = 2 * FFN_COL
    ncol = c2 // cb2
    nt = s // t
    kw = FFN_CONV_W
    ext = t + HALO

    def body(x_ref, prev_ref, next_ref, da_ref, dan_ref, w_ref, b_ref, dx_ref, dw_ref, db_ref, xs, dus):
        i = pl.program_id(1)

        @pl.when(i == 0)
        def _():
            dw_ref[...] = jnp.zeros_like(dw_ref)
            db_ref[...] = jnp.zeros_like(db_ref)

        xs[0:HALO, :] = jnp.where(i > 0, prev_ref[...], 0.0)
        xs[HALO:HALO + t, :] = x_ref[...]
        xs[HALO + t:HALO + ext, :] = next_ref[...]
        w = [w_ref[k:k + 1, :] for k in range(kw)]
        b = b_ref[...]
        for r0 in range(0, ext, STRIP):
            rows = min(STRIP, ext - r0)
            u = b + w[kw - 1] * xs[HALO + r0:HALO + r0 + rows, :]
            for k in range(kw - 1):
                lo = HALO - (kw - 1) + k + r0
                u = u + w[k] * xs[lo:lo + rows, :]
            gel, dgel = _gelu_and_grad(u[:, :FFN_COL])
            if r0 < t:
                dav = da_ref[r0:r0 + rows, :]
            else:
                dav = jnp.where(i < nt - 1, dan_ref[...], 0.0)
            dus[r0:r0 + rows, :FFN_COL] = dav * u[:, FFN_COL:] * dgel
            dus[r0:r0 + rows, FFN_COL:] = dav * gel
        fold = lambda a: a.reshape(a.shape[0] // 8, 8, cb2).sum(axis=0)
        db = jnp.zeros((8, cb2), F32)
        dw = [jnp.zeros((8, cb2), F32) for _ in range(kw)]
        for r0 in range(0, t, STRIP):
            rows = min(STRIP, t - r0)
            du = dus[r0:r0 + rows, :]
            dx = w[kw - 1] * du
            for k in range(kw - 1):
                lo = r0 + kw - 1 - k
                dx = dx + w[k] * dus[lo:lo + rows, :]
            dx_ref[r0:r0 + rows, :] = dx.astype(BF16)
            db = db + fold(du)
            for k in range(kw):
                lo = HALO - (kw - 1) + k + r0
                dw[k] = dw[k] + fold(du * xs[lo:lo + rows, :])
        db_ref[...] += jnp.sum(db, axis=0, keepdims=True)
        for k in range(kw):
            dw_ref[k:k + 1, :] += jnp.sum(dw[k], axis=0, keepdims=True)

    nxt = lambda c: pl.BlockSpec((HALO, c), lambda j, i: (jnp.minimum((i + 1) * (t // HALO), s // HALO - 1), j))
    return pl.pallas_call(
        body, grid=(ncol, nt),
        in_specs=[pl.BlockSpec((t, cb2), lambda j, i: (i, j)),
                  pl.BlockSpec((HALO, cb2), lambda j, i: (jnp.maximum(i * (t // HALO) - 1, 0), j)),
                  nxt(cb2),
                  pl.BlockSpec((t, FFN_COL), lambda j, i: (i, j)),
                  nxt(FFN_COL),
                  pl.BlockSpec((kw, cb2), lambda j, i: (0, j)),
                  pl.BlockSpec((1, cb2), lambda j, i: (0, j))],
        out_specs=[pl.BlockSpec((t, cb2), lambda j, i: (i, j)),
                   pl.BlockSpec((kw, cb2), lambda j, i: (0, j)),
                   pl.BlockSpec((1, cb2), lambda j, i: (0, j))],
        out_shape=[jax.ShapeDtypeStruct((s, c2), BF16), jax.ShapeDtypeStruct((kw, c2), F32),
                   jax.ShapeDtypeStruct((1, c2), F32)],
        scratch_shapes=[pltpu.VMEM((ext + HALO, cb2), F32), pltpu.VMEM((ext, cb2), F32)],
        compiler_params=_params(("parallel", "arbitrary")), name=name,
    )(pre, pre, pre, dact, dact, cw, cb)


def _rg_conv_fwd(proj, cw, cb, name):
    s = proj.shape[0]
    c = D_RNN_PAD
    t = _tile(s, 256)

    def body(x_ref, halo_ref, w_ref, b_ref, o_ref, ob_ref, scr):
        i = pl.program_id(0)
        scr[0:HALO, :] = jnp.where(i > 0, halo_ref[...], 0.0)
        scr[HALO:HALO + t, :] = x_ref[...]
        u = _conv_taps(scr, w_ref, b_ref, RG_CONV_W, t, HALO)
        o_ref[...] = u
        ob_ref[...] = u.astype(BF16)

    row = pl.BlockSpec((t, c), lambda i: (i, 0))
    return pl.pallas_call(
        body, grid=(s // t,),
        in_specs=[pl.BlockSpec((t, c), lambda i: (i, 1)),
                  _prev_halo_spec(t, c, lambda i: 1),
                  pl.BlockSpec((RG_CONV_W, c), lambda i: (0, 0)),
                  pl.BlockSpec((1, c), lambda i: (0, 0))],
        out_specs=[row, row],
        out_shape=[jax.ShapeDtypeStruct((s, c), F32), jax.ShapeDtypeStruct((s, c), BF16)],
        scratch_shapes=[pltpu.VMEM((t + HALO, c), F32)],
        compiler_params=_params(("parallel",)), name=name,
    )(proj, proj, cw, cb)


def _neg_expm1(x):
    y = jnp.exp(x)
    ly = jnp.log(y)
    safe = jnp.where(ly == 0.0, 1.0, ly)
    em = jnp.where(y == 1.0, x, (y - 1.0) * x / safe)
    em = jnp.where(x < -30.0, -1.0, em)
    return -em


def _rg_gates(pre_ax, rec, ba, bx, lam):
    c = D_RNN_PAD
    r = jax.nn.sigmoid(pre_ax[:, :c] + ba)
    ig = jax.nn.sigmoid(pre_ax[:, c:] + bx)
    sp = jnp.maximum(-lam, 0.0) + jnp.log1p(jnp.exp(-jnp.abs(lam)))
    log_a = (-RG_C) * r * sp
    a = jnp.exp(log_a)
    m2 = _neg_expm1(2.0 * log_a)
    mult = jnp.sqrt(m2)
    return r, ig, sp, a, m2, mult


def _rg_scan_fwd(pre_ax, rec, proj, ba, bx, lam, name):
    s = rec.shape[0]
    c = D_RNN_PAD
    t = _tile(s, 256)

    def body(pax_ref, rec_ref, gate_ref, ba_ref, bx_ref, lam_ref, h_ref, gh_ref, a_scr, u_scr, carry):
        @pl.when(pl.program_id(0) == 0)
        def _():
            carry[...] = jnp.zeros_like(carry)

        recv = rec_ref[...]
        _, ig, _, a, _, mult = _rg_gates(pax_ref[...], recv, ba_ref[...], bx_ref[...], lam_ref[...])
        a_scr[...] = a
        u_scr[...] = mult * ig * recv
        rowid = lax.broadcasted_iota(jnp.int32, (8, c), 0)

        def group(gi, h):
            r0 = pl.multiple_of(gi * 8, 8)
            a8 = a_scr[pl.ds(r0, 8), :]
            u8 = u_scr[pl.ds(r0, 8), :]
            out = jnp.zeros((8, c), F32)
            for j in range(8):
                h = jnp.broadcast_to(a8[j:j + 1, :], (8, c)) * h + jnp.broadcast_to(u8[j:j + 1, :], (8, c))
                out = jnp.where(rowid == j, h, out)
            h_ref[pl.ds(r0, 8), :] = out
            return h

        carry[...] = lax.fori_loop(0, t // 8, group, carry[...])
        gh_ref[...] = (_gelu(gate_ref[...]) * h_ref[...]).astype(BF16)

    row = pl.BlockSpec((t, c), lambda i: (i, 0))
    vec = pl.BlockSpec((1, c), lambda i: (0, 0))
    return pl.pallas_call(
        body, grid=(s // t,),
        in_specs=[pl.BlockSpec((t, 2 * c), lambda i: (i, 0)), row, row, vec, vec, vec],
        out_specs=[row, row],
        out_shape=[jax.ShapeDtypeStruct((s, c), F32), jax.ShapeDtypeStruct((s, c), BF16)],
        scratch_shapes=[pltpu.VMEM((t, c), F32), pltpu.VMEM((t, c), F32), pltpu.VMEM((8, c), F32)],
        compiler_params=_params(("arbitrary",)), name=name,
    )(pre_ax, rec, proj, ba, bx, lam)


def _rg_scan_bwd(dgh, proj, h, pre_ax, rec, ba, bx, lam, name):
    s = rec.shape[0]
    c = D_RNN_PAD
    t = _tile(s, 256)
    nt = s // t

    def body(dgh_ref, gate_ref, h_ref, hprev_ref, pax_ref, rec_ref, ba_ref, bx_ref, lam_ref,
             dgate_ref, dpax_ref, drec_ref, dba_ref, dbx_ref, dlam_ref, a_scr, dh_scr, g_scr, hp_scr, carry):
        i = pl.program_id(0)
        ti = nt - 1 - i

        @pl.when(i == 0)
        def _():
            carry[...] = jnp.zeros_like(carry)
            dba_ref[...] = jnp.zeros_like(dba_ref)
            dbx_ref[...] = jnp.zeros_like(dbx_ref)
            dlam_ref[...] = jnp.zeros_like(dlam_ref)

        recv = rec_ref[...]
        lam = lam_ref[...]
        r, ig, sp, a, m2, mult = _rg_gates(pax_ref[...], recv, ba_ref[...], bx_ref[...], lam)
        gel, dgel = _gelu_and_grad(gate_ref[...])
        dghv = dgh_ref[...]
        hv = h_ref[...]
        dgate_ref[...] = (dghv * hv * dgel).astype(BF16)
        a_scr[...] = a
        dh_scr[...] = dghv * gel
        rowid = lax.broadcasted_iota(jnp.int32, (8, c), 0)

        def group(gi, cr):
            r0 = pl.multiple_of((t // 8 - 1 - gi) * 8, 8)
            a8 = a_scr[pl.ds(r0, 8), :]
            d8 = dh_scr[pl.ds(r0, 8), :]
            out = jnp.zeros((8, c), F32)
            for j in range(7, -1, -1):
                g = jnp.broadcast_to(d8[j:j + 1, :], (8, c)) + cr
                out = jnp.where(rowid == j, g, out)
                cr = jnp.broadcast_to(a8[j:j + 1, :], (8, c)) * g
            g_scr[pl.ds(r0, 8), :] = out
            return cr

        carry[...] = lax.fori_loop(0, t // 8, group, carry[...])
        g = g_scr[...]
        hp_scr[0:HALO, :] = jnp.where(ti > 0, hprev_ref[...], 0.0)
        hp_scr[HALO:HALO + t, :] = hv
        da = g * hp_scr[pl.ds(HALO - 1, t), :]
        dmult = g * ig * recv
        dig = g * mult * recv
        drec_ref[...] = g * mult * ig
        dm2 = dmult * 0.5 / mult
        dloga = da * a - 2.0 * (1.0 - m2) * dm2
        dr = dloga * ((-RG_C) * sp)
        dsp = jnp.sum(dloga * ((-RG_C) * r), axis=0, keepdims=True)
        dpa = dr * r * (1.0 - r)
        dpx = dig * ig * (1.0 - ig)
        dpax_ref[:, :c] = dpa.astype(BF16)
        dpax_ref[:, c:] = dpx.astype(BF16)
        dba_ref[...] += jnp.sum(dpa, axis=0, keepdims=True)
        dbx_ref[...] += jnp.sum(dpx, axis=0, keepdims=True)
        dlam_ref[...] += dsp * (-jnp.exp(-(lam + sp)))

    row = pl.BlockSpec((t, c), lambda i: (nt - 1 - i, 0))
    vec = pl.BlockSpec((1, c), lambda i: (0, 0))
    return pl.pallas_call(
        body, grid=(nt,),
        in_specs=[row, row, row,
                  pl.BlockSpec((HALO, c), lambda i: (jnp.maximum((nt - 1 - i) * (t // HALO) - 1, 0), 0)),
                  pl.BlockSpec((t, 2 * c), lambda i: (nt - 1 - i, 0)), row, vec, vec, vec],
        out_specs=[row, pl.BlockSpec((t, 2 * c), lambda i: (nt - 1 - i, 0)), row, vec, vec, vec],
        out_shape=[jax.ShapeDtypeStruct((s, c), BF16), jax.ShapeDtypeStruct((s, 2 * c), BF16),
                   jax.ShapeDtypeStruct((s, c), F32), jax.ShapeDtypeStruct((1, c), F32),
                   jax.ShapeDtypeStruct((1, c), F32), jax.ShapeDtypeStruct((1, c), F32)],
        scratch_shapes=[pltpu.VMEM((t, c), F32), pltpu.VMEM((t, c), F32), pltpu.VMEM((t, c), F32),
                        pltpu.VMEM((t + HALO, c), F32), pltpu.VMEM((8, c), F32)],
        compiler_params=_params(("arbitrary",)), name=name,
    )(dgh, proj, h, h, pre_ax, rec, ba, bx, lam)


def _rg_conv_bwd(drec_a, drec_b, proj, cw, name):
    s = drec_a.shape[0]
    c = D_RNN_PAD
    t = _tile(s, 256)
    nt = s // t
    kw = RG_CONV_W
    ext = t + HALO

    def body(da_ref, dan_ref, db_ref, dbn_ref, x_ref, prev_ref, w_ref, dx_ref, dw_ref, dbias_ref, xs, dus):
        i = pl.program_id(0)

        @pl.when(i == 0)
        def _():
            dw_ref[...] = jnp.zeros_like(dw_ref)
            dbias_ref[...] = jnp.zeros_like(dbias_ref)

        xs[0:HALO, :] = jnp.where(i > 0, prev_ref[...], 0.0)
        xs[HALO:HALO + t, :] = x_ref[...]
        du = da_ref[...] + db_ref[...]
        dus[0:t, :] = du
        dus[t:ext, :] = jnp.where(i < nt - 1, dan_ref[...] + dbn_ref[...], 0.0)
        dx = w_ref[kw - 1:kw, :] * du
        for k in range(kw - 1):
            dx = dx + w_ref[k:k + 1, :] * dus[pl.ds(kw - 1 - k, t), :]
        dx_ref[...] = dx.astype(BF16)
        dbias_ref[...] += jnp.sum(du, axis=0, keepdims=True)
        for k in range(kw):
            dw_ref[k:k + 1, :] += jnp.sum(du * xs[pl.ds(HALO - (kw - 1) + k, t), :], axis=0, keepdims=True)

    row = pl.BlockSpec((t, c), lambda i: (i, 0))
    nxt = pl.BlockSpec((HALO, c), lambda i: (jnp.minimum((i + 1) * (t // HALO), s // HALO - 1), 0))
    return pl.pallas_call(
        body, grid=(nt,),
        in_specs=[row, nxt, row, nxt,
                  pl.BlockSpec((t, c), lambda i: (i, 1)),
                  _prev_halo_spec(t, c, lambda i: 1),
                  pl.BlockSpec((kw, c), lambda i: (0, 0))],
        out_specs=[row, pl.BlockSpec((kw, c), lambda i: (0, 0)), pl.BlockSpec((1, c), lambda i: (0, 0))],
        out_shape=[jax.ShapeDtypeStruct((s, c), BF16), jax.ShapeDtypeStruct((kw, c), F32),
                   jax.ShapeDtypeStruct((1, c), F32)],
        scratch_shapes=[pltpu.VMEM((t + HALO, c), F32), pltpu.VMEM((ext, c), F32)],
        compiler_params=_params(("arbitrary",)), name=name,
    )(drec_a, drec_a, drec_b, drec_b, proj, proj, cw)


N_CHIP = 4


def _chip_exchange(operands, name):
    nop = len(operands)
    sames = [same for _, same in operands]

    def body(*refs):
        srcs = refs[:nop]
        outs = refs[nop:2 * nop]
        send_sems, recv_sems, local_sems = refs[2 * nop:]
        x, y, c = lax.axis_index("x"), lax.axis_index("y"), lax.axis_index("c")
        mine = 2 * x + y
        local = []
        for o in range(nop):
            cp = pltpu.make_async_copy(srcs[o] if sames[o] else srcs[o].at[mine], outs[o].at[mine], local_sems.at[o])
            cp.start()
            local.append(cp)
        copies = []
        for k in range(1, N_CHIP):
            px = (x + (k >> 1)) % 2
            py = (y + (k & 1)) % 2
            chip = 2 * px + py
            for o in range(nop):
                sem = (k - 1) * nop + o
                src = srcs[o] if sames[o] else srcs[o].at[chip]
                send = pltpu.make_async_remote_copy(
                    src_ref=src, dst_ref=outs[o].at[mine], send_sem=send_sems.at[sem], recv_sem=recv_sems.at[sem],
                    device_id=(px, py, c), device_id_type=pl.DeviceIdType.MESH)
                send.start()
                recv = pltpu.make_async_remote_copy(
                    src_ref=src, dst_ref=outs[o].at[chip], send_sem=send_sems.at[sem], recv_sem=recv_sems.at[sem],
                    device_id=(px, py, c), device_id_type=pl.DeviceIdType.MESH)
                copies.append((send, recv))
        for send, recv in copies:
            send.wait_send()
            recv.wait_recv()
        for cp in local:
            cp.wait()

    anyspec = pl.BlockSpec(memory_space=pl.ANY)
    shapes = [jax.ShapeDtypeStruct((N_CHIP,) + (arr.shape if same else arr.shape[1:]), arr.dtype)
              for arr, same in operands]
    return pl.pallas_call(
        body, in_specs=[anyspec] * nop, out_specs=[anyspec] * nop, out_shape=shapes,
        scratch_shapes=[pltpu.SemaphoreType.DMA(((N_CHIP - 1) * nop,)), pltpu.SemaphoreType.DMA(((N_CHIP - 1) * nop,)),
                        pltpu.SemaphoreType.DMA((nop,))],
        name=name,
    )(*[arr for arr, _ in operands])


def _core_swap(operands, name):
    nop = len(operands)
    wholes = [whole for _, whole in operands]

    def body(*refs):
        srcs = refs[:nop]
        outs = refs[nop:2 * nop]
        send_sems, recv_sems = refs[2 * nop:]
        x, y, c = lax.axis_index("x"), lax.axis_index("y"), lax.axis_index("c")
        other = 1 - c
        copies = []
        for o in range(nop):
            cp = pltpu.make_async_remote_copy(
                src_ref=srcs[o] if wholes[o] else srcs[o].at[other], dst_ref=outs[o],
                send_sem=send_sems.at[o], recv_sem=recv_sems.at[o],
                device_id=(x, y, other), device_id_type=pl.DeviceIdType.MESH)
            cp.start()
            copies.append(cp)
        for cp in copies:
            cp.wait()

    anyspec = pl.BlockSpec(memory_space=pl.ANY)
    shapes = [jax.ShapeDtypeStruct(arr.shape if whole else arr.shape[1:], arr.dtype) for arr, whole in operands]
    return pl.pallas_call(
        body, in_specs=[anyspec] * nop, out_specs=[anyspec] * nop, out_shape=shapes,
        scratch_shapes=[pltpu.SemaphoreType.DMA((nop,)), pltpu.SemaphoreType.DMA((nop,))],
        name=name,
    )(*[arr for arr, _ in operands])


def _pair_sum(mine2, theirs, name):
    _, nq, n, cols = mine2.shape
    t = _tile(n, 512, 16)

    def body(a_ref, b_ref, o_ref):
        c = lax.axis_index("c")
        o_ref[...] = (a_ref[c].astype(F32) + b_ref[...].astype(F32)).astype(o_ref.dtype)

    return pl.pallas_call(
        body, grid=(nq, n // t),
        in_specs=[pl.BlockSpec((2, None, t, cols), lambda q, i: (0, q, i, 0)),
                  pl.BlockSpec((None, t, cols), lambda q, i: (q, i, 0))],
        out_specs=pl.BlockSpec((None, t, cols), lambda q, i: (q, i, 0)),
        out_shape=jax.ShapeDtypeStruct(theirs.shape, theirs.dtype),
        compiler_params=_params(("parallel", "parallel")), name=name,
    )(mine2, theirs)


def _core_sum(mine, theirs, name):
    n, _ = mine.shape
    t = _tile(n, 1024)

    def body(a_ref, b_ref, o_ref):
        o_ref[...] = a_ref[...] + b_ref[...]

    row = pl.BlockSpec((t, LANES), lambda i: (i, 0))
    return pl.pallas_call(
        body, grid=(n // t,), in_specs=[row, row], out_specs=row,
        out_shape=jax.ShapeDtypeStruct(mine.shape, mine.dtype),
        compiler_params=_params(("parallel",)), name=name,
    )(mine, theirs)


def _adam(parts, w, m, v, name):
    n, cols = w.shape
    nparts = parts.shape[0]
    t = _tile(n, 512, 16)
    c1 = 1.0 - ADAM_B1 ** ADAM_STEP
    c2 = 1.0 - ADAM_B2 ** ADAM_STEP

    def body(p_ref, w_ref, m_ref, v_ref, g_ref, d_ref, mo_ref, vo_ref):
        g = p_ref[0].astype(F32)
        for k in range(1, nparts):
            g = g + p_ref[k].astype(F32)
        mn = ADAM_B1 * m_ref[...] + (1.0 - ADAM_B1) * g
        vn = ADAM_B2 * v_ref[...] + (1.0 - ADAM_B2) * (g * g)
        m_hat = mn / c1
        v_hat = vn / c2
        g_ref[...] = g
        d_ref[...] = (-ADAM_LR) * (m_hat / (jnp.sqrt(v_hat) + ADAM_EPS) + ADAM_WD * w_ref[...])
        mo_ref[...] = mn
        vo_ref[...] = vn

    row = pl.BlockSpec((t, cols), lambda i: (i, 0))
    out = jax.ShapeDtypeStruct((n, cols), F32)
    return pl.pallas_call(
        body, grid=(n // t,),
        in_specs=[pl.BlockSpec((nparts, t, cols), lambda i: (0, i, 0)), row, row, row],
        out_specs=[row, row, row, row], out_shape=[out, out, out, out],
        compiler_params=_params(("parallel",)), name=name,
    )(parts, w, m, v)


def _pack(pieces, row_mult, lead=False):
    if lead:
        flat = jnp.concatenate([p.reshape(N_DEV, -1) for p in pieces], axis=1)
        n = flat.shape[1]
    else:
        flat = jnp.concatenate([p.reshape(-1) for p in pieces])
        n = flat.shape[0]
    group = row_mult * LANES
    total = -(-n // group) * group
    if lead:
        flat = jnp.pad(flat, ((0, 0), (0, total - n)))
        return flat.reshape(N_DEV, total // LANES, LANES)
    flat = jnp.pad(flat, (0, total - n))
    return flat.reshape(total // LANES, LANES)


def _unpack(buf, shapes, lead=False):
    out = []
    off = 0
    flat = buf.reshape(N_DEV, -1) if lead else buf.reshape(-1)
    for shp in shapes:
        size = math.prod(shp)
        if lead:
            out.append(flat[:, off:off + size].reshape((N_DEV,) + tuple(shp)))
        else:
            out.append(flat[off:off + size].reshape(shp))
        off += size
    return out


BIG = ["attn_w_qkv", "attn_w_o", "rg_w_in", "rg_w_out", "ffn_w_up", "ffn_w_down"]
SMALL = ["rg_conv_w", "rg_conv_b", "rg_b_a", "rg_b_x", "rg_lambda", "ffn_conv_w"]
REPL = ["rg_w_a", "rg_w_x", "ffn_conv_b", "mix_pre_g", "mix_post_g", "ffn_pre_g", "ffn_post_g"]
ORDER = ["attn_w_qkv", "attn_w_o", "rg_w_in", "rg_conv_w", "rg_conv_b", "rg_w_a", "rg_b_a", "rg_w_x", "rg_b_x",
         "rg_lambda", "rg_w_out", "ffn_w_up", "ffn_conv_w", "ffn_conv_b", "ffn_w_down", "mix_pre_g",
         "mix_post_g", "ffn_pre_g", "ffn_post_g"]
SHARD_AXIS = {"attn_w_qkv": 2, "attn_w_o": 1, "rg_w_in": 2, "rg_w_out": 1, "ffn_w_up": 2, "ffn_w_down": 1,
              "rg_conv_w": 2, "rg_conv_b": 1, "rg_b_a": 1, "rg_b_x": 1, "rg_lambda": 1, "ffn_conv_w": 2}
SMALL_ROWS = 64
REPL_ROWS = 128


def _assemble(stacked, axis):
    moved = jnp.moveaxis(stacked, 0, axis)
    shp = list(moved.shape)
    shp[axis:axis + 2] = [shp[axis] * shp[axis + 1]]
    return moved.reshape(shp)


def _split(full, axis):
    shp = list(full.shape)
    shp[axis:axis + 1] = [N_DEV, shp[axis] // N_DEV]
    return jnp.moveaxis(full.reshape(shp), axis, 0)


def _block_diag_pad(w):
    out = jnp.zeros((D_RNN_PAD, D_RNN_PAD), w.dtype)
    for n in range(RG_BLOCKS):
        o = n * RG_BLOCK_W
        out = lax.dynamic_update_slice(out, w[n], (o, o))
    return out


def _block_diag_take(dense):
    return jnp.stack([dense[n * RG_BLOCK_W:(n + 1) * RG_BLOCK_W, n * RG_BLOCK_W:(n + 1) * RG_BLOCK_W]
                      for n in range(RG_BLOCKS)])


def _pad_cols(a, width):
    return jnp.pad(a, ((0, 0), (0, width - a.shape[1])))


def _interleave(a):
    lead = a.shape[:-1]
    nb = D_FF // FFN_COL
    return jnp.swapaxes(a.reshape(lead + (2, nb, FFN_COL)), -3, -2).reshape(lead + (2 * D_FF,))


def _deinterleave(a):
    lead = a.shape[:-1]
    nb = D_FF // FFN_COL
    return jnp.swapaxes(a.reshape(lead + (nb, 2, FFN_COL)), -3, -2).reshape(lead + (2 * D_FF,))


def _ffn_fwd(hn, wp, i):
    pre = _matmul(hn, wp["up"][i], name=f"ffn{i}_up")
    act = _ffn_act_fwd(pre, wp["ffn_cw"][i], wp["ffn_cb"][i], name=f"ffn{i}_act")
    f = _matmul(act, wp["down"][i], name=f"ffn{i}_down")
    return pre, act, f


def _ffn_bwd(df, hn, pre, act, wp, i):
    dact = _matmul(df, wp["down"][i], tb=True, name=f"ffn{i}_dact")
    d_down = _matmul(act, df, ta=True, name=f"ffn{i}_ddown")
    dpre, dcw, dcb = _ffn_act_bwd(pre, dact, wp["ffn_cw"][i], wp["ffn_cb"][i], name=f"ffn{i}_act_bwd")
    d_up = _matmul(hn, dpre, ta=True, name=f"ffn{i}_dup")
    dhn = _matmul(dpre, wp["up"][i], tb=True, name=f"ffn{i}_dhn")
    return dhn, d_up, d_down, dcw, dcb


def _local_step(x, tgt, wp):
    g = wp["gains"]
    gr = {}

    hn0 = _rms_fwd(x, g["mix_pre"][0], name="l0_mix_pre")
    qkv = _matmul(hn0, wp["qkv"], out_dtype=BF16, name="attn_qkv")
    o, lsum, first = _attn_fwd(qkv, name="attn_fwd")
    f0 = _matmul(o, wp["wo"], name="attn_out")
    x1, hn1 = _resid_norm(x, f0, g["mix_post"][0], g["ffn_pre"][0], name="l0_mix_post")
    pre0, act0, f1 = _ffn_fwd(hn1, wp, 0)
    x2, hn2 = _resid_norm(x1, f1, g["ffn_post"][0], g["mix_pre"][1], name="l0_ffn_post")
    proj = _matmul(hn2, wp["w_in"], name="rg_in")
    rec, rec_b = _rg_conv_fwd(proj, wp["rg_cw"], wp["rg_cb"], name="rg_conv")
    pre_ax = _matmul(rec_b, wp["w_ax"], name="rg_gates")
    h, gh = _rg_scan_fwd(pre_ax, rec, proj, wp["rg_ba"], wp["rg_bx"], wp["rg_lam"], name="rg_scan")
    f2 = _matmul(gh, wp["w_out"], name="rg_out")
    x3, hn3 = _resid_norm(x2, f2, g["mix_post"][1], g["ffn_pre"][1], name="l1_mix_post")
    pre1, act1, f3 = _ffn_fwd(hn3, wp, 1)
    dy, sq = _resid_loss(x3, f3, g["ffn_post"][1], tgt, name="l1_ffn_post_loss")

    ffn_dup, ffn_ddown, ffn_dcw, ffn_dcb = [None, None], [None, None], [None, None], [None, None]
    d_ffn_post, d_ffn_pre, d_mix_post, d_mix_pre = [None, None], [None, None], [None, None], [None, None]

    df3, d_ffn_post[1] = _rms_bwd(f3, g["ffn_post"][1], dy, None, BF16, name="l1_ffn_post_bwd")
    dhn3, ffn_dup[1], ffn_ddown[1], ffn_dcw[1], ffn_dcb[1] = _ffn_bwd(df3, hn3, pre1, act1, wp, 1)
    dx3, d_ffn_pre[1] = _rms_bwd(x3, g["ffn_pre"][1], dhn3, dy, F32, name="l1_ffn_pre_bwd")

    df2, d_mix_post[1] = _rms_bwd(f2, g["mix_post"][1], dx3, None, BF16, name="l1_mix_post_bwd")
    dgh = _matmul(df2, wp["w_out"], tb=True, name="rg_dgh")
    gr["w_out"] = _matmul(gh, df2, ta=True, name="rg_dwout")
    dgate, dpax, drec_u, gr["rg_ba"], gr["rg_bx"], gr["rg_lam"] = _rg_scan_bwd(
        dgh, proj, h, pre_ax, rec, wp["rg_ba"], wp["rg_bx"], wp["rg_lam"], name="rg_scan_bwd")
    drec_g = _matmul(dpax, wp["w_ax"], tb=True, name="rg_drec")
    gr["w_ax"] = _matmul(rec_b, dpax, ta=True, name="rg_dwax")
    dproj_rec, gr["rg_cw"], gr["rg_cb"] = _rg_conv_bwd(drec_u, drec_g, proj, wp["rg_cw"], name="rg_conv_bwd")
    dproj = jnp.concatenate([dgate, dproj_rec], axis=1)
    gr["w_in"] = _matmul(hn2, dproj, ta=True, name="rg_dwin")
    dhn2 = _matmul(dproj, wp["w_in"], tb=True, name="rg_dhn")
    dx2, d_mix_pre[1] = _rms_bwd(x2, g["mix_pre"][1], dhn2, dx3, F32, name="l1_mix_pre_bwd")

    df1, d_ffn_post[0] = _rms_bwd(f1, g["ffn_post"][0], dx2, None, BF16, name="l0_ffn_post_bwd")
    dhn1, ffn_dup[0], ffn_ddown[0], ffn_dcw[0], ffn_dcb[0] = _ffn_bwd(df1, hn1, pre0, act0, wp, 0)
    dx1, d_ffn_pre[0] = _rms_bwd(x1, g["ffn_pre"][0], dhn1, dx2, F32, name="l0_ffn_pre_bwd")

    df0, d_mix_post[0] = _rms_bwd(f0, g["mix_post"][0], dx1, None, BF16, name="l0_mix_post_bwd")
    do = _matmul(df0, wp["wo"], tb=True, out_dtype=BF16, name="attn_do")
    gr["wo"] = _matmul(o, df0, ta=True, name="attn_dwo")
    dq, dk, dv = _attn_bwd(qkv, do, lsum, first, name="attn_bwd")
    dqkv = jnp.concatenate([dq, dk.astype(BF16), dv.astype(BF16)], axis=1)
    gr["qkv"] = _matmul(hn0, dqkv, ta=True, name="attn_dwqkv")
    dhn0 = _matmul(dqkv, wp["qkv"], tb=True, name="attn_dhn")
    dx0, d_mix_pre[0] = _rms_bwd(x, g["mix_pre"][0], dhn0, dx1, F32, name="l0_mix_pre_bwd")

    gr["up"] = ffn_dup
    gr["down"] = ffn_ddown
    gr["ffn_cw"] = ffn_dcw
    gr["ffn_cb"] = ffn_dcb
    gr["gains"] = {"mix_pre": d_mix_pre, "mix_post": d_mix_post, "ffn_pre": d_ffn_pre, "ffn_post": d_ffn_post}
    return sq[0, 0], dx0, gr


def _prepare(full):
    c, cp = D_RNN, D_RNN_PAD
    w_in = full["rg_w_in"][0]
    w_a = _block_diag_pad(full["rg_w_a"][0].astype(BF16))
    w_x = _block_diag_pad(full["rg_w_x"][0].astype(BF16))
    vec = lambda a: _pad_cols(a.reshape(1, c), cp)
    return {
        "qkv": full["attn_w_qkv"][0],
        "wo": full["attn_w_o"][0],
        "w_in": jnp.concatenate([_pad_cols(w_in[:, :c], cp), _pad_cols(w_in[:, c:], cp)], axis=1),
        "w_ax": jnp.concatenate([w_a, w_x], axis=1),
        "w_out": jnp.pad(full["rg_w_out"][0], ((0, cp - c), (0, 0))),
        "rg_cw": _pad_cols(full["rg_conv_w"][0], cp),
        "rg_cb": vec(full["rg_conv_b"][0]),
        "rg_ba": vec(full["rg_b_a"][0]),
        "rg_bx": vec(full["rg_b_x"][0]),
        "rg_lam": vec(full["rg_lambda"][0]),
        "up": [_interleave(full["ffn_w_up"][i]) for i in range(2)],
        "down": [full["ffn_w_down"][i] for i in range(2)],
        "ffn_cw": [_interleave(full["ffn_conv_w"][i]) for i in range(2)],
        "ffn_cb": [_interleave(full["ffn_conv_b"][i].reshape(1, -1)) for i in range(2)],
        "gains": {k: [full[k + "_g"][i].reshape(1, D_MODEL) for i in range(2)]
                  for k in ("mix_pre", "mix_post", "ffn_pre", "ffn_post")},
    }


def _natural_grads(gr):
    c, cp = D_RNN, D_RNN_PAD
    gg = gr["gains"]
    stack2 = lambda pair: jnp.stack([pair[0].reshape(-1), pair[1].reshape(-1)])
    return {
        "attn_w_qkv": gr["qkv"][None],
        "attn_w_o": gr["wo"][None],
        "rg_w_in": jnp.concatenate([gr["w_in"][:, :c], gr["w_in"][:, cp:cp + c]], axis=1)[None],
        "rg_conv_w": gr["rg_cw"][:, :c][None],
        "rg_conv_b": gr["rg_cb"][:, :c],
        "rg_w_a": _block_diag_take(gr["w_ax"][:, :cp])[None],
        "rg_b_a": gr["rg_ba"][:, :c],
        "rg_w_x": _block_diag_take(gr["w_ax"][:, cp:])[None],
        "rg_b_x": gr["rg_bx"][:, :c],
        "rg_lambda": gr["rg_lam"][:, :c],
        "rg_w_out": gr["w_out"][:c][None],
        "ffn_w_up": jnp.stack([_deinterleave(gr["up"][i]) for i in range(2)]),
        "ffn_conv_w": jnp.stack([_deinterleave(gr["ffn_cw"][i]) for i in range(2)]),
        "ffn_conv_b": jnp.stack([_deinterleave(gr["ffn_cb"][i]).reshape(-1) for i in range(2)]),
        "ffn_w_down": jnp.stack(gr["down"]),
        "mix_pre_g": stack2(gg["mix_pre"]),
        "mix_post_g": stack2(gg["mix_post"]),
        "ffn_pre_g": stack2(gg["ffn_pre"]),
        "ffn_post_g": stack2(gg["ffn_post"]),
    }


def _rows(a):
    return a.reshape(-1, a.shape[-1])


def _step(x, loss_target, w, m, v):
    small_shapes = [w[n].shape for n in SMALL]
    repl_shapes = [w[n].shape for n in REPL]
    wsmall = _pack([w[n] for n in SMALL], SMALL_ROWS)
    on_chips = _chip_exchange([(_rows(w[n].astype(BF16)), True) for n in BIG] + [(wsmall, True)],
                              name="gather_weights_chips")
    from_sibling = _core_swap([(a, True) for a in on_chips], name="gather_weights_cores")
    south = lax.axis_index("c") == 0

    def by_device(mine, theirs):
        pair = jnp.stack([jnp.where(south, mine, theirs), jnp.where(south, theirs, mine)], axis=1)
        return pair.reshape((N_DEV,) + mine.shape[1:])

    full = {}
    for n, a, b in zip(BIG, on_chips, from_sibling):
        full[n] = _assemble(by_device(a, b).reshape((N_DEV,) + w[n].shape), SHARD_AXIS[n])
    for n, st in zip(SMALL, _unpack(by_device(on_chips[-1], from_sibling[-1]), small_shapes, lead=True)):
        full[n] = _assemble(st, SHARD_AXIS[n])
    for n in REPL:
        full[n] = w[n]

    sq, dx, gr = _local_step(x[0], loss_target[0], _prepare(full))
    grads = _natural_grads(gr)

    by_core = lambda a: jnp.swapaxes(a.reshape((N_CHIP, 2) + a.shape[1:]), 0, 1)
    pbig = [by_core(_split(grads[n], SHARD_AXIS[n]).astype(BF16).reshape((N_DEV,) + _rows(w[n]).shape)) for n in BIG]
    psmall = by_core(_pack([_split(grads[n], SHARD_AXIS[n]) for n in SMALL], SMALL_ROWS, lead=True))
    prepl = _pack([grads[n] for n in REPL], REPL_ROWS)
    swapped = _core_swap([(p, False) for p in pbig] + [(psmall, False), (prepl, True)], name="reduce_grads_cores")
    sums = [_pair_sum(p, o, name="chip_sum_" + n) for n, p, o in zip(BIG, pbig, swapped)]
    sums.append(_pair_sum(psmall, swapped[-2], name="chip_sum_small"))
    trepl = _core_sum(prepl, swapped[-1], name="chip_sum_repl")
    reduced = _chip_exchange([(t, False) for t in sums] + [(trepl, True)], name="reduce_grads_chips")

    out = {}
    kinds = ("grad", "delta", "new_m", "new_v")
    for n, parts in zip(BIG, reduced):
        res = _adam(parts, _rows(w[n]), _rows(m[n]), _rows(v[n]), name="adamw_" + n)
        for kind, buf in zip(kinds, res):
            out[(kind, n)] = buf.reshape(w[n].shape)
    for names, parts, rows, shapes, tag in ((SMALL, reduced[-2], SMALL_ROWS, small_shapes, "small"),
                                            (REPL, reduced[-1], REPL_ROWS, repl_shapes, "repl")):
        packs = [_pack([d[n] for n in names], rows) for d in (w, m, v)]
        res = _adam(parts, *packs, name="adamw_" + tag)
        for kind, buf in zip(kinds, res):
            for n, a in zip(names, _unpack(buf, shapes)):
                out[(kind, n)] = a

    loss = lax.psum(sq * (0.5 / D_MODEL), ("x", "y", "c"))
    return loss, dx[None], out


def kernel(x, attn_w_qkv, attn_w_o, rg_w_in, rg_conv_w, rg_conv_b, rg_w_a, rg_b_a, rg_w_x, rg_b_x, rg_lambda, rg_w_out, ffn_w_up, ffn_conv_w, ffn_conv_b, ffn_w_down, mix_pre_g, mix_post_g, ffn_pre_g, ffn_post_g, loss_target, m_attn_w_qkv, m_attn_w_o, m_rg_w_in, m_rg_conv_w, m_rg_conv_b, m_rg_w_a, m_rg_b_a, m_rg_w_x, m_rg_b_x, m_rg_lambda, m_rg_w_out, m_ffn_w_up, m_ffn_conv_w, m_ffn_conv_b, m_ffn_w_down, m_mix_pre_g, m_mix_post_g, m_ffn_pre_g, m_ffn_post_g, v_attn_w_qkv, v_attn_w_o, v_rg_w_in, v_rg_conv_w, v_rg_conv_b, v_rg_w_a, v_rg_b_a, v_rg_w_x, v_rg_b_x, v_rg_lambda, v_rg_w_out, v_ffn_w_up, v_ffn_conv_w, v_ffn_conv_b, v_ffn_w_down, v_mix_pre_g, v_mix_post_g, v_ffn_pre_g, v_ffn_post_g):
    w = dict(zip(ORDER, (attn_w_qkv, attn_w_o, rg_w_in, rg_conv_w, rg_conv_b, rg_w_a, rg_b_a, rg_w_x, rg_b_x,
                         rg_lambda, rg_w_out, ffn_w_up, ffn_conv_w, ffn_conv_b, ffn_w_down, mix_pre_g,
                         mix_post_g, ffn_pre_g, ffn_post_g)))
    m = dict(zip(ORDER, (m_attn_w_qkv, m_attn_w_o, m_rg_w_in, m_rg_conv_w, m_rg_conv_b, m_rg_w_a, m_rg_b_a,
                         m_rg_w_x, m_rg_b_x, m_rg_lambda, m_rg_w_out, m_ffn_w_up, m_ffn_conv_w, m_ffn_conv_b,
                         m_ffn_w_down, m_mix_pre_g, m_mix_post_g, m_ffn_pre_g, m_ffn_post_g)))
    v = dict(zip(ORDER, (v_attn_w_qkv, v_attn_w_o, v_rg_w_in, v_rg_conv_w, v_rg_conv_b, v_rg_w_a, v_rg_b_a,
                         v_rg_w_x, v_rg_b_x, v_rg_lambda, v_rg_w_out, v_ffn_w_up, v_ffn_conv_w, v_ffn_conv_b,
                         v_ffn_w_down, v_mix_pre_g, v_mix_post_g, v_ffn_pre_g, v_ffn_post_g)))
    loss, dx, out = _step(x, loss_target, w, m, v)
    return (loss, dx, *[out[("grad", n)] for n in ORDER], *[out[("delta", n)] for n in ORDER],
            *[out[("new_m", n)] for n in ORDER], *[out[("new_v", n)] for n in ORDER])
```
